```python
import functools
import jax
import jax.numpy as jnp
from jax import lax
import numpy as np

D_MODEL = 1024
BATCH = 8
SEQ = 4096
DEPTH = 1
DEC_BATCH = 128
DEC_SEQ = 8
PAST_LEN = 8192
PAGE_SIZE = 128

N_META = 16
N_HEADS = 16
HEAD_DIM = 64
N_KV_HEADS = 4
ATT_WIDTH = N_HEADS * HEAD_DIM
KV_WIDTH = N_KV_HEADS * HEAD_DIM
IDX_HEADS = 8
IDX_DIM = 64
TOPK_MAX = 256
Q_BLOCK = 128
GLA_HEADS = 4
GLA_DK = D_MODEL // 2 // GLA_HEADS
GLA_DV = D_MODEL // GLA_HEADS
GLA_KW = GLA_HEADS * GLA_DK
GLA_VW = GLA_HEADS * GLA_DV
GLA_GATE_RANK = 16
GLA_TAU = 16.0
GLA_CHUNK = 64
N_BRANCHES = 2
D_FF = 4 * D_MODEL
LN_EPS = 1e-5
DEEPNORM_ALPHA = (2 * DEPTH) ** 0.25
DEEPNORM_BETA = (8 * DEPTH) ** -0.25
IN_SIZES = (ATT_WIDTH, KV_WIDTH, KV_WIDTH, IDX_HEADS * IDX_DIM, IDX_DIM, IDX_HEADS,
            GLA_KW, GLA_KW, GLA_VW, GLA_VW, GLA_GATE_RANK, N_BRANCHES * D_MODEL)
IN_WIDTH = sum(IN_SIZES)

kernel_name = 'dsa_gla_gated_hybrid_step'


def layer_norm(x, g, b):
    xf = x.astype(jnp.float32)
    mu = jnp.mean(xf, -1, keepdims=True)
    var = jnp.mean(jnp.square(xf - mu), -1, keepdims=True)
    y = (xf - mu) * lax.rsqrt(var + LN_EPS) * g.astype(jnp.float32) + b.astype(jnp.float32)
    return y.astype(x.dtype)


def in_proj(h, w_in, w_gla_a2, b_gla_a):
    B, T, _ = h.shape
    points = [int(p) for p in np.cumsum(IN_SIZES)[:-1]]
    q, k, v, qi, ki, wi, gq, gk, gv, og, a1, gt = jnp.split(h @ w_in, points, axis=-1)
    loga = jax.nn.log_sigmoid((a1 @ w_gla_a2).astype(jnp.float32) + b_gla_a.astype(jnp.float32)) / GLA_TAU
    gates = jax.nn.sigmoid(gt.astype(jnp.float32)).reshape(B, T, N_BRANCHES, D_MODEL)
    return (q.reshape(B, T, N_HEADS, HEAD_DIM), k.reshape(B, T, N_KV_HEADS, HEAD_DIM),
            v.reshape(B, T, N_KV_HEADS, HEAD_DIM), qi.reshape(B, T, IDX_HEADS, IDX_DIM), ki, wi,
            gq.reshape(B, T, GLA_HEADS, GLA_DK), gk.reshape(B, T, GLA_HEADS, GLA_DK),
            gv.reshape(B, T, GLA_HEADS, GLA_DV), og, loga.reshape(B, T, GLA_HEADS, GLA_DK), gates)


def indexer_scores(qi, wi, ki):
    s = jnp.einsum('bqhd,bld->bqhl', qi, ki).astype(jnp.float32) * IDX_DIM ** -0.5
    w = wi.astype(jnp.float32) * IDX_HEADS ** -0.5
    return jnp.einsum('bqhl,bqh->bql', jax.nn.relu(s), w)


def gathered_attention(q, k_sel, v_sel, valid):
    B, Q, H, hd = q.shape
    qg = q.reshape(B, Q, N_KV_HEADS, H // N_KV_HEADS, hd)
    s = jnp.einsum('bqngd,bqknd->bqngk', qg, k_sel).astype(jnp.float32) * hd ** -0.5
    s = jnp.where(valid[:, :, None, None, :], s, -jnp.inf)
    p = jax.nn.softmax(s, axis=-1).astype(v_sel.dtype)
    return jnp.einsum('bqngk,bqknd->bqngd', p, v_sel).reshape(B, Q, H * hd)


def dsa_prompt(q, k, v, qi, wi, ki, topk):
    B, T = q.shape[:2]
    n_blk = -(-T // Q_BLOCK)
    pad = n_blk * Q_BLOCK - T

    def blocks(a):
        a = jnp.pad(a, [(0, 0), (0, pad)] + [(0, 0)] * (a.ndim - 2))
        return jnp.moveaxis(a.reshape((B, n_blk, Q_BLOCK) + a.shape[2:]), 1, 0)

    key_pos = jnp.arange(T)
    take_rows = jax.vmap(lambda rows, idx: rows[idx])

    def one_block(xs):
        qb, qib, wib, start = xs
        pos = start + jnp.arange(Q_BLOCK)
        causal = key_pos[None, :] <= pos[:, None]
        sc = jnp.where(causal[None], indexer_scores(qib, wib, ki), -jnp.inf)
        _, idx = lax.top_k(sc, topk)
        valid = idx <= pos[None, :, None]
        return gathered_attention(qb, take_rows(k, idx), take_rows(v, idx), valid)

    out = lax.map(one_block, (blocks(q), blocks(qi), blocks(wi), jnp.arange(n_blk) * Q_BLOCK))
    return jnp.moveaxis(out, 0, 1).reshape(B, n_blk * Q_BLOCK, -1)[:, :T]


def dsa_sample(q, k, v, qi, wi, ki, cache_k, cache_v, cache_kidx, page_table, topk):
    DB, S = q.shape[:2]
    past = page_table.shape[1] * PAGE_SIZE
    ki_past = cache_kidx[page_table].reshape(DB, past, IDX_DIM)
    ki_all = jnp.concatenate([ki_past, ki.astype(ki_past.dtype)], axis=1)
    pos = past + jnp.arange(S)
    causal = jnp.arange(past + S)[None, :] <= pos[:, None]
    sc = jnp.where(causal[None], indexer_scores(qi, wi, ki_all), -jnp.inf)
    _, idx = lax.top_k(sc, topk)
    valid = idx <= pos[None, :, None]
    is_new = (idx >= past)[..., None, None]
    past_idx = jnp.minimum(idx, past - 1)
    phys = jax.vmap(lambda pt, i: pt[i])(page_table, past_idx // PAGE_SIZE)
    off = past_idx % PAGE_SIZE
    new_idx = jnp.clip(idx - past, 0, S - 1)
    take_rows = jax.vmap(lambda rows, i: rows[i])
    k_sel = jnp.where(is_new, take_rows(k, new_idx).astype(cache_k.dtype), cache_k[phys, off])
    v_sel = jnp.where(is_new, take_rows(v, new_idx).astype(cache_v.dtype), cache_v[phys, off])
    return gathered_attention(q, k_sel, v_sel, valid)


def gla_chunked(q, k, v, loga, s0, chunk):
    B, T, H, DK = q.shape
    DV = v.shape[-1]
    n = T // chunk
    f32 = jnp.float32
    q = (q.astype(f32) * DK ** -0.5).reshape(B, n, chunk, H, DK)
    k = k.astype(f32).reshape(B, n, chunk, H, DK)
    v = v.astype(f32).reshape(B, n, chunk, H, DV)
    b = jnp.cumsum(loga.astype(f32).reshape(B, n, chunk, H, DK), axis=2)
    b_last = b[:, :, -1:]
    q_d = q * jnp.exp(b)
    k_d = k * jnp.exp(-b)
    causal = jnp.tril(jnp.ones((chunk, chunk), bool))
    attn = jnp.where(causal, jnp.einsum('bnchd,bnshd->bnhcs', q_d, k_d), 0.0)
    o_intra = jnp.einsum('bnhcs,bnshv->bnchv', attn, v)
    u = jnp.einsum('bnchd,bnchv->nbhdv', k * jnp.exp(b_last - b), v)
    decay = jnp.exp(jnp.moveaxis(b_last[:, :, 0], 1, 0))[..., None]

    def step(s, xs):
        d, uc = xs
        return d * s + uc, s

    s_final, s_prev = lax.scan(step, s0.astype(f32), (decay, u))
    o_inter = jnp.einsum('bnchd,nbhdv->bnchv', q_d, s_prev)
    return (o_intra + o_inter).reshape(B, T, H, DV), s_final


def gla_prompt(q, k, v, loga):
    B, T, H, _ = q.shape
    pad = (-N_META) % GLA_CHUNK
    fp = lambda a: jnp.pad(a, ((0, 0), (pad, 0), (0, 0), (0, 0)))
    s0 = jnp.zeros((B, H, GLA_DK, GLA_DV), jnp.float32)
    o, s = gla_chunked(fp(q), fp(k), fp(v), fp(loga), s0, GLA_CHUNK)
    return o[:, pad:], s


def gla_sample(q, k, v, loga, s0):
    return gla_chunked(q, k, v, loga, s0, q.shape[1])


def token_mixer(h, w_in, w_gla_a2, b_gla_a, gla_norm_g, w_proj_attn, w_proj_gla, w_out, attend, recur):
    B, T, _ = h.shape
    q, k, v, qi, ki, wi, gq, gk, gv, og, loga, gates = in_proj(h, w_in, w_gla_a2, b_gla_a)
    att = attend(q, k, v, qi, wi, ki).astype(h.dtype)
    o, s_new = recur(gq, gk, gv, loga)
    o = o * lax.rsqrt(jnp.mean(jnp.square(o), -1, keepdims=True) + LN_EPS) * gla_norm_g.astype(jnp.float32)
    gla = (o.reshape(B, T, GLA_VW) * jax.nn.silu(og.astype(jnp.float32))).astype(h.dtype)
    merged = gates[:, :, 0] * (att @ w_proj_attn) + gates[:, :, 1] * (gla @ w_proj_gla)
    return merged.astype(h.dtype) @ w_out, (k, v, ki, s_new.astype(h.dtype))


def post_norm_block(h, mix, ln1_g, ln1_b, w_ff_up, w_ff_down, ln2_g, ln2_b):
    h = layer_norm(DEEPNORM_ALPHA * h + mix, ln1_g, ln1_b)
    ff = jnp.square(jax.nn.relu(h @ w_ff_up)) @ w_ff_down
    return layer_norm(DEEPNORM_ALPHA * h + ff, ln2_g, ln2_b)


def setup_inputs(seed: int = 0) -> dict:
    key = jax.random.key(seed)
    ks = jax.random.split(key, 24)
    f32 = jnp.float32
    n_pages = PAST_LEN // PAGE_SIZE
    n_pool = (5 * DEC_BATCH * n_pages) // 4
    nrm = lambda kk, shape, scale=1.0: jax.random.normal(kk, shape, f32) * scale
    page_table = jax.random.permutation(ks[6], n_pool)[: DEC_BATCH * n_pages]
    page_table = page_table.reshape(DEC_BATCH, n_pages).astype(jnp.int32)
    return {
        'x_prompt': nrm(ks[0], (BATCH, SEQ, D_MODEL)),
        'x_sample': nrm(ks[1], (DEC_BATCH, DEC_SEQ, D_MODEL)),
        'cache_k': nrm(ks[2], (DEPTH, n_pool, PAGE_SIZE, N_KV_HEADS, HEAD_DIM)),
        'cache_v': nrm(ks[3], (DEPTH, n_pool, PAGE_SIZE, N_KV_HEADS, HEAD_DIM)),
        'cache_kidx': nrm(ks[4], (DEPTH, n_pool, PAGE_SIZE, IDX_DIM)),
        'state_gla': nrm(ks[5], (DEPTH, DEC_BATCH, GLA_HEADS, GLA_DK, GLA_DV)),
        'page_table': page_table,
        'meta_tokens': nrm(ks[7], (N_META, D_MODEL)),
        'ln_in_g': 1.0 + nrm(ks[8], (D_MODEL,), 0.02),
        'ln_in_b': nrm(ks[9], (D_MODEL,), 0.02),
        'w_in': nrm(ks[10], (DEPTH, D_MODEL, IN_WIDTH), D_MODEL ** -0.5),
        'w_gla_a2': nrm(ks[11], (DEPTH, GLA_GATE_RANK, GLA_KW), GLA_GATE_RANK ** -0.5),
        'b_gla_a': nrm(ks[12], (DEPTH, GLA_KW), 0.1),
        'gla_norm_g': 1.0 + nrm(ks[13], (DEPTH, GLA_DV), 0.02),
        'w_proj_attn': nrm(ks[14], (DEPTH, ATT_WIDTH, D_MODEL), ATT_WIDTH ** -0.5),
        'w_proj_gla': nrm(ks[15], (DEPTH, GLA_VW, D_MODEL), GLA_VW ** -0.5),
        'w_out': nrm(ks[16], (DEPTH, D_MODEL, D_MODEL), DEEPNORM_BETA * D_MODEL ** -0.5),
        'ln1_g': 1.0 + nrm(ks[17], (DEPTH, D_MODEL), 0.02),
        'ln1_b': nrm(ks[18], (DEPTH, D_MODEL), 0.02),
        'w_ff_up': nrm(ks[19], (DEPTH, D_MODEL, D_FF), D_MODEL ** -0.5),
        'w_ff_down': nrm(ks[20], (DEPTH, D_FF, D_MODEL), DEEPNORM_BETA * D_FF ** -0.5),
        'ln2_g': 1.0 + nrm(ks[21], (DEPTH, D_MODEL), 0.02),
        'ln2_b': nrm(ks[22], (DEPTH, D_MODEL), 0.02),
    }


def reference(x_prompt, x_sample, cache_k, cache_v, cache_kidx, state_gla, page_table,
              meta_tokens, ln_in_g, ln_in_b, w_in, w_gla_a2, b_gla_a, gla_norm_g,
              w_proj_attn, w_proj_gla, w_out, ln1_g, ln1_b, w_ff_up, w_ff_down, ln2_g, ln2_b):
    B, S_p, D = x_prompt.shape
    S_s = x_sample.shape[1]
    past = page_table.shape[1] * PAGE_SIZE
    topk_prompt = min(TOPK_MAX, S_p // 4)
    topk_sample = min(TOPK_MAX, (past + S_s) // 4)

    meta = jnp.broadcast_to(meta_tokens.astype(x_prompt.dtype)[None], (B, N_META, D))
    hp = layer_norm(jnp.concatenate([meta, x_prompt], axis=1), ln_in_g, ln_in_b)
    hs = layer_norm(x_sample, ln_in_g, ln_in_b)

    kp_l, vp_l, kip_l, sp_l, ks_l, vs_l, kis_l, ss_l = [], [], [], [], [], [], [], []
    for l in range(DEPTH):
        mix_w = (w_in[l], w_gla_a2[l], b_gla_a[l], gla_norm_g[l], w_proj_attn[l], w_proj_gla[l], w_out[l])
        ffn_w = (ln1_g[l], ln1_b[l], w_ff_up[l], w_ff_down[l], ln2_g[l], ln2_b[l])
        mix_p, (kp, vp, kip, sp) = token_mixer(
            hp, *mix_w, attend=functools.partial(dsa_prompt, topk=topk_prompt), recur=gla_prompt)
        hp = post_norm_block(hp, mix_p, *ffn_w)
        attend_s = functools.partial(dsa_sample, cache_k=cache_k[l], cache_v=cache_v[l],
                                     cache_kidx=cache_kidx[l], page_table=page_table, topk=topk_sample)
        recur_s = functools.partial(gla_sample, s0=state_gla[l])
        mix_s, (ks_, vs_, kis_, ss_) = token_mixer(hs, *mix_w, attend=attend_s, recur=recur_s)
        hs = post_norm_block(hs, mix_s, *ffn_w)
        kp_l.append(kp); vp_l.append(vp); kip_l.append(kip); sp_l.append(sp)
        ks_l.append(ks_); vs_l.append(vs_); kis_l.append(kis_); ss_l.append(ss_)

    y_prompt = hp[:, N_META:]
    y_sample = hs
    k_prompt = jnp.stack(kp_l)
    v_prompt = jnp.stack(vp_l)
    kidx_prompt = jnp.stack(kip_l)
    gla_state_prompt = jnp.stack(sp_l)
    k_sample = jnp.stack(ks_l)
    v_sample = jnp.stack(vs_l)
    kidx_sample = jnp.stack(kis_l)
    gla_state_sample = jnp.stack(ss_l)
    return (y_prompt, y_sample, k_prompt, v_prompt, kidx_prompt, gla_state_prompt,
            k_sample, v_sample, kidx_sample, gla_state_sample)
```

```python
import functools
import math

import jax
import jax.numpy as jnp
from jax import lax
from jax.experimental import pallas as pl
from jax.experimental.pallas import tpu as pltpu

D_MODEL = 1024
PAGE_SIZE = 128
N_META = 16
N_HEADS = 16
HEAD_DIM = 64
N_KV_HEADS = 4
GROUP = N_HEADS // N_KV_HEADS
ATT_WIDTH = N_HEADS * HEAD_DIM
KV_WIDTH = N_KV_HEADS * HEAD_DIM
IDX_HEADS = 8
IDX_DIM = 64
TOPK_MAX = 256
GLA_HEADS = 4
GLA_DK = D_MODEL // 2 // GLA_HEADS
GLA_DV = D_MODEL // GLA_HEADS
GLA_KW = GLA_HEADS * GLA_DK
GLA_VW = GLA_HEADS * GLA_DV
GLA_GATE_RANK = 16
GLA_TAU = 16.0
GLA_CHUNK = 64
N_BRANCHES = 2
D_FF = 4 * D_MODEL
LN_EPS = 1e-5
IN_SIZES = (ATT_WIDTH, KV_WIDTH, KV_WIDTH, IDX_HEADS * IDX_DIM, IDX_DIM, IDX_HEADS,
            GLA_KW, GLA_KW, GLA_VW, GLA_VW, GLA_GATE_RANK, N_BRANCHES * D_MODEL)

LANES = 128
SUBLANES = 8
VMEM_LIMIT_BYTES = 56 * 1024 * 1024
Q_BLOCK = LANES
KEY_CHUNK = 3 * LANES
SAMPLE_ROWS = 16
PAGES_PER_STEP = 16
SEL_SEQS = 16
TM_PROJ_GLA = 256
TM_TOKEN = 512
GLA_TOKENS_PER_STEP = 11 * GLA_CHUNK

F32 = jnp.float32
BF16 = jnp.bfloat16
NEG_INF = float("-inf")
INT_MIN = -2 ** 31
NT_DIMS = (((1,), (1,)), ((), ()))
TN_DIMS = (((0,), (0,)), ((), ()))


def _cparams(n_grid):
    return pltpu.CompilerParams(dimension_semantics=("arbitrary",) * n_grid,
                                vmem_limit_bytes=VMEM_LIMIT_BYTES)


def _full_spec(shape):
    nd = len(shape)
    return pl.BlockSpec(shape, lambda *_: (0,) * nd)


def _layer_norm(x, g, b):
    mu = jnp.mean(x, axis=-1, keepdims=True)
    xc = x - mu
    var = jnp.mean(xc * xc, axis=-1, keepdims=True)
    return xc * lax.rsqrt(var + LN_EPS) * g + b


def _ukey_to_f32(u):
    bits = jnp.where(u < 0, u & jnp.int32(0x7FFFFFFF), ~u)
    return lax.bitcast_convert_type(bits, F32)


def _proj_attn_prompt_kernel(x_ref, g_ref, b_ref, wq_ref, wqi_ref, wv_ref, wwi_ref, wrow_ref,
                             qT_ref, qiT_ref, vT_ref, wiT_ref, k_ref, v_ref, ki_ref, kb_ref, kib_ref):
    hb = _layer_norm(x_ref[...], g_ref[...], b_ref[...]).astype(BF16)
    qT_ref[...] = lax.dot_general(wq_ref[...], hb, NT_DIMS, preferred_element_type=F32).astype(BF16)
    qiT_ref[...] = lax.dot_general(wqi_ref[...], hb, NT_DIMS, preferred_element_type=F32).astype(BF16)
    vt = lax.dot_general(wv_ref[...], hb, NT_DIMS, preferred_element_type=F32)
    row = lax.broadcasted_iota(jnp.int32, vt.shape, 0)
    vt = jnp.where(row % LANES == HEAD_DIM, 1.0, vt)
    vT_ref[0] = vt.astype(BF16)
    wi = lax.dot_general(wwi_ref[...], hb, NT_DIMS, preferred_element_type=F32)
    wiT_ref[...] = wi[:IDX_HEADS] * IDX_HEADS ** -0.5
    y = jnp.dot(hb, wrow_ref[...], preferred_element_type=F32)
    k = y[:, :KV_WIDTH]
    ki = y[:, 2 * KV_WIDTH:2 * KV_WIDTH + IDX_DIM]
    k_ref[...] = k
    v_ref[...] = y[:, KV_WIDTH:2 * KV_WIDTH]
    ki_ref[...] = ki
    kb_ref[...] = k.astype(BF16)
    kib_ref[...] = ki.astype(BF16)


def _proj_attn_prompt(x, ln_g, ln_b, w, n_tok):
    tm = KEY_CHUNK
    n_steps = n_tok // tm
    row = lambda n: pl.BlockSpec((tm, n), lambda i: (i, 0))
    col = lambda n: pl.BlockSpec((n, tm), lambda i: (0, i))
    out_shape = (
        jax.ShapeDtypeStruct((ATT_WIDTH, n_tok), BF16),
        jax.ShapeDtypeStruct((IDX_HEADS * IDX_DIM, n_tok), BF16),
        jax.ShapeDtypeStruct((n_steps, N_KV_HEADS * LANES, tm), BF16),
        jax.ShapeDtypeStruct((IDX_HEADS, n_tok), F32),
        jax.ShapeDtypeStruct((n_tok, KV_WIDTH), F32),
        jax.ShapeDtypeStruct((n_tok, KV_WIDTH), F32),
        jax.ShapeDtypeStruct((n_tok, IDX_DIM), F32),
        jax.ShapeDtypeStruct((n_tok, KV_WIDTH), BF16),
        jax.ShapeDtypeStruct((n_tok, IDX_DIM), BF16),
    )
    out_specs = (col(ATT_WIDTH), col(IDX_HEADS * IDX_DIM),
                 pl.BlockSpec((1, N_KV_HEADS * LANES, tm), lambda i: (i, 0, 0)),
                 col(IDX_HEADS), row(KV_WIDTH), row(KV_WIDTH), row(IDX_DIM), row(KV_WIDTH), row(IDX_DIM))
    ws = (w["qT"], w["qiT"], w["vT"], w["wiT"], w["row_p"])
    return pl.pallas_call(
        _proj_attn_prompt_kernel,
        grid=(n_steps,),
        in_specs=[row(D_MODEL), _full_spec(ln_g.shape), _full_spec(ln_b.shape)] + [_full_spec(a.shape) for a in ws],
        out_specs=out_specs,
        out_shape=out_shape,
        compiler_params=_cparams(1),
        name="proj_attn_prompt",
    )(x, ln_g, ln_b, *ws)


def _proj_attn_sample_kernel(x_ref, g_ref, b_ref, w_ref, scale_ref, q_ref, qi_ref, k_ref, v_ref, kiwi_ref):
    hb = _layer_norm(x_ref[...], g_ref[...], b_ref[...]).astype(BF16)
    y = jnp.dot(hb, w_ref[...], preferred_element_type=F32)
    o = 0
    q_ref[...] = y[:, o:o + ATT_WIDTH].astype(BF16)
    o += ATT_WIDTH
    qi_ref[...] = y[:, o:o + IDX_HEADS * IDX_DIM].astype(BF16)
    o += IDX_HEADS * IDX_DIM
    k_ref[...] = y[:, o:o + KV_WIDTH]
    o += KV_WIDTH
    v_ref[...] = y[:, o:o + KV_WIDTH]
    o += KV_WIDTH
    kiwi_ref[...] = y[:, o:o + LANES] * scale_ref[...]


def _proj_attn_sample(x, ln_g, ln_b, w, n_tok):
    tm = _row_tile(n_tok, TM_TOKEN)
    row = lambda n: pl.BlockSpec((tm, n), lambda i: (i, 0))
    lane = lax.iota(jnp.int32, LANES)
    scale = jnp.where((lane >= IDX_DIM) & (lane < IDX_DIM + IDX_HEADS), IDX_HEADS ** -0.5, 1.0).astype(F32)[None]
    out_shape = (
        jax.ShapeDtypeStruct((n_tok, ATT_WIDTH), BF16),
        jax.ShapeDtypeStruct((n_tok, IDX_HEADS * IDX_DIM), BF16),
        jax.ShapeDtypeStruct((n_tok, KV_WIDTH), F32),
        jax.ShapeDtypeStruct((n_tok, KV_WIDTH), F32),
        jax.ShapeDtypeStruct((n_tok, LANES), F32),
    )
    return pl.pallas_call(
        _proj_attn_sample_kernel,
        grid=(n_tok // tm,),
        in_specs=[row(D_MODEL), _full_spec(ln_g.shape), _full_spec(ln_b.shape),
                  _full_spec(w["row_s"].shape), _full_spec(scale.shape)],
        out_specs=(row(ATT_WIDTH), row(IDX_HEADS * IDX_DIM), row(KV_WIDTH), row(KV_WIDTH), row(LANES)),
        out_shape=out_shape,
        compiler_params=_cparams(1),
        name="proj_attn_sample",
    )(x, ln_g, ln_b, w["row_s"], scale)


def _log_sigmoid(x):
    return jnp.minimum(x, 0.0) - jnp.log1p(jnp.exp(-jnp.abs(x)))


def _proj_gla_kernel(x_ref, g_ref, b_ref, wg_ref, wa1_ref, wa2_ref, ba_ref, wgt_ref,
                     gq_ref, gk_ref, gv_ref, og_ref, la_ref, gates_ref):
    hb = _layer_norm(x_ref[...], g_ref[...], b_ref[...]).astype(BF16)
    y = jnp.dot(hb, wg_ref[...], preferred_element_type=F32)
    gq_ref[...] = y[:, :GLA_KW]
    gk_ref[...] = y[:, GLA_KW:2 * GLA_KW]
    gv_ref[...] = y[:, 2 * GLA_KW:2 * GLA_KW + GLA_VW].astype(BF16)
    og_ref[...] = y[:, 2 * GLA_KW + GLA_VW:]
    a1 = jnp.dot(hb, wa1_ref[...], preferred_element_type=F32).astype(BF16)
    z = jnp.dot(a1, wa2_ref[...], preferred_element_type=F32) + ba_ref[...]
    la_ref[...] = _log_sigmoid(z) / GLA_TAU
    gt = jnp.dot(hb, wgt_ref[...], preferred_element_type=F32)
    gates_ref[...] = jax.nn.sigmoid(gt).astype(BF16)


def _proj_gla(x, ln_g, ln_b, w, n_tok):
    tm = _row_tile(n_tok, TM_PROJ_GLA)
    row = lambda n: pl.BlockSpec((tm, n), lambda i: (i, 0))
    ws = (w["gla"], w["a1"], w["a2"], w["ba"], w["gt"])
    out_shape = (
        jax.ShapeDtypeStruct((n_tok, GLA_KW), F32),
        jax.ShapeDtypeStruct((n_tok, GLA_KW), F32),
        jax.ShapeDtypeStruct((n_tok, GLA_VW), BF16),
        jax.ShapeDtypeStruct((n_tok, GLA_VW), F32),
        jax.ShapeDtypeStruct((n_tok, GLA_KW), F32),
        jax.ShapeDtypeStruct((n_tok, N_BRANCHES * D_MODEL), BF16),
    )
    return pl.pallas_call(
        _proj_gla_kernel,
        grid=(n_tok // tm,),
        in_specs=[row(D_MODEL), _full_spec(ln_g.shape), _full_spec(ln_b.shape)] + [_full_spec(a.shape) for a in ws],
        out_specs=(row(GLA_KW), row(GLA_KW), row(GLA_VW), row(GLA_VW), row(GLA_KW), row(N_BRANCHES * D_MODEL)),
        out_shape=out_shape,
        compiler_params=_cparams(1),
        name="proj_gla",
    )(x, ln_g, ln_b, *ws)


def _dsa_prompt_kernel(qT_ref, qiT_ref, wiT_ref, kb_ref, kib_ref, vT_ref, att_ref,
                       sc_ref, qn_ref, acc_ref, j_ref, *, topk, off, n_pos_bits):
    kc = KEY_CHUNK
    i = pl.program_id(1)
    qpos0 = i * Q_BLOCK
    n_chunks = (qpos0 + Q_BLOCK + kc - 1) // kc
    qpos = qpos0 + lax.broadcasted_iota(jnp.int32, (1, Q_BLOCK), 1)

    def key_pos(c):
        return c * kc + lax.broadcasted_iota(jnp.int32, (kc, Q_BLOCK), 0)

    qn_ref[...] = jnp.zeros(qn_ref.shape, BF16)
    for n in range(N_KV_HEADS):
        for g in range(GROUP):
            h = n * GROUP + g
            qn_ref[n, n * HEAD_DIM:(n + 1) * HEAD_DIM, g * Q_BLOCK:(g + 1) * Q_BLOCK] = (
                qT_ref[h * HEAD_DIM:(h + 1) * HEAD_DIM, :])

    wi = wiT_ref[...]

    def idx_body(c, carry):
        kic = kib_ref[c]
        acc = jnp.zeros((kc, Q_BLOCK), F32)
        for hp in range(IDX_HEADS // 2):
            r0 = 2 * hp * IDX_DIM
            rhs = jnp.concatenate([qiT_ref[r0:r0 + IDX_DIM, :], qiT_ref[r0 + IDX_DIM:r0 + 2 * IDX_DIM, :]], axis=1)
            s = jnp.maximum(jnp.dot(kic, rhs, preferred_element_type=F32), 0.0)
            acc = acc + s[:, :Q_BLOCK] * wi[2 * hp:2 * hp + 1, :] + s[:, Q_BLOCK:] * wi[2 * hp + 1:2 * hp + 2, :]
        kp = key_pos(c)
        valid = (kp >= off) & (kp <= qpos)
        sc_ref[c] = jnp.where(valid, acc, NEG_INF)
        return carry

    lax.fori_loop(0, n_chunks, idx_body, 0)

    def count(pred):
        def body(c, acc8):
            v = jnp.where(pred(c), 1.0, 0.0)
            return acc8 + jnp.sum(v.reshape(kc // SUBLANES, SUBLANES, Q_BLOCK), axis=0)
        acc8 = lax.fori_loop(0, n_chunks, body, jnp.zeros((SUBLANES, Q_BLOCK), F32))
        return jnp.sum(acc8, axis=0, keepdims=True)

    def bit_body(t, res):
        trial = res | jnp.left_shift(jnp.int32(1), 31 - t)
        cand = _ukey_to_f32(trial)
        cnt = count(lambda c: sc_ref[c] >= cand)
        return jnp.where(cnt >= topk, trial, res)

    res = lax.fori_loop(0, 32, bit_body, jnp.zeros((1, Q_BLOCK), jnp.int32))
    few = (qpos - off + 1) <= topk
    thr = jnp.where(few, NEG_INF, _ukey_to_f32(res))
    cnt_gt = count(lambda c: sc_ref[c] > thr)
    cnt_ge = count(lambda c: sc_ref[c] >= thr)
    n_ties = topk - cnt_gt
    tie_rows = jnp.where(few, 0.0, jnp.where(cnt_ge > topk, 1.0, 0.0))
    j_ref[...] = jnp.full(j_ref.shape, 2 ** 30, jnp.int32)

    @pl.when(jnp.max(tie_rows) > 0.0)
    def _():
        def jbit_body(t, resj):
            trial = resj | jnp.left_shift(jnp.int32(1), n_pos_bits - 1 - t)
            below = count(lambda c: (sc_ref[c] == thr) & (key_pos(c) < trial))
            return jnp.where(below < n_ties, trial, resj)
        j_ref[...] = lax.fori_loop(0, n_pos_bits, jbit_body, jnp.zeros((1, Q_BLOCK), jnp.int32))

    jmax = j_ref[...]

    def bias_body(c, carry):
        sc = sc_ref[c]
        kp = key_pos(c)
        valid = (kp >= off) & (kp <= qpos)
        sel = (sc > thr) | ((sc == thr) & (kp <= jmax))
        sc_ref[c] = jnp.where(valid & sel, 0.0, NEG_INF)
        return carry

    lax.fori_loop(0, n_chunks, bias_body, 0)

    def scores(c, n):
        bias = sc_ref[c]
        bias4 = jnp.concatenate([bias] * GROUP, axis=1)
        return jnp.dot(kb_ref[c], qn_ref[n], preferred_element_type=F32) + bias4

    def max_body(c, ms):
        out = []
        for n in range(N_KV_HEADS):
            s = scores(c, n)
            out.append(jnp.maximum(ms[n], jnp.max(s.reshape(kc // SUBLANES, SUBLANES, GROUP * Q_BLOCK), axis=0)))
        return tuple(out)

    m0 = tuple(jnp.full((SUBLANES, GROUP * Q_BLOCK), NEG_INF, F32) for _ in range(N_KV_HEADS))
    ms = lax.fori_loop(0, n_chunks, max_body, m0)
    ms = tuple(jnp.maximum(jnp.max(m, axis=0, keepdims=True), -1e30) for m in ms)

    acc_ref[...] = jnp.zeros(acc_ref.shape, F32)

    def pv_body(c, carry):
        vt = vT_ref[c]
        for n in range(N_KV_HEADS):
            p = jnp.exp(scores(c, n) - ms[n]).astype(BF16)
            acc_ref[n] += jnp.dot(vt[n * LANES:(n + 1) * LANES, :], p, preferred_element_type=F32)
        return carry

    lax.fori_loop(0, n_chunks, pv_body, 0)

    heads = []
    for n in range(N_KV_HEADS):
        a = acc_ref[n]
        denom = a[HEAD_DIM:HEAD_DIM + 1, :]
        o = a[:HEAD_DIM, :] / jnp.where(denom > 0.0, denom, 1.0)
        for g in range(GROUP):
            heads.append(o[:, g * Q_BLOCK:(g + 1) * Q_BLOCK])
    att_ref[...] = jnp.concatenate(heads, axis=0).T.astype(BF16)


def _dsa_prompt(qT, qiT, wiT, kb3, kib3, vT3, *, batch, t_pad, topk, off):
    kc = KEY_CHUNK
    nqb = t_pad // Q_BLOCK
    ncb = t_pad // kc
    n_tok = batch * t_pad
    qcol = lambda n: pl.BlockSpec((n, Q_BLOCK), lambda b, i: (0, b * nqb + i))
    kern = functools.partial(_dsa_prompt_kernel, topk=topk, off=off,
                             n_pos_bits=max(1, math.ceil(math.log2(t_pad))))
    return pl.pallas_call(
        kern,
        grid=(batch, nqb),
        in_specs=[qcol(ATT_WIDTH), qcol(IDX_HEADS * IDX_DIM), qcol(IDX_HEADS),
                  pl.BlockSpec((ncb, kc, KV_WIDTH), lambda b, i: (b, 0, 0)),
                  pl.BlockSpec((ncb, kc, IDX_DIM), lambda b, i: (b, 0, 0)),
                  pl.BlockSpec((ncb, N_KV_HEADS * LANES, kc), lambda b, i: (b, 0, 0))],
        out_specs=pl.BlockSpec((Q_BLOCK, ATT_WIDTH), lambda b, i: (b * nqb + i, 0)),
        out_shape=jax.ShapeDtypeStruct((n_tok, ATT_WIDTH), BF16),
        scratch_shapes=[pltpu.VMEM((ncb, kc, Q_BLOCK), F32),
                        pltpu.VMEM((N_KV_HEADS, KV_WIDTH, GROUP * Q_BLOCK), BF16),
                        pltpu.VMEM((N_KV_HEADS, LANES, GROUP * Q_BLOCK), F32),
                        pltpu.VMEM((1, Q_BLOCK), jnp.int32)],
        compiler_params=_cparams(2),
        name="dsa_prompt",
    )(qT, qiT, wiT, kb3, kib3, vT3)


def _idx_sample_kernel(pt_ref, qi_ref, wcol_ref, kinew_ref, *rest):
    npg = PAGES_PER_STEP
    pages = rest[:npg]
    past_ref, new_ref = rest[npg], rest[npg + 1]
    s_q = new_ref.shape[0]
    qi = qi_ref[...]
    wcol = wcol_ref[...]

    def page_scores(keys_bf16):
        s = lax.dot_general(qi, keys_bf16, NT_DIMS, preferred_element_type=F32)
        s = jnp.maximum(s, 0.0) * wcol
        return jnp.sum(s.reshape(IDX_HEADS, s_q, LANES), axis=0)

    for p in range(npg):
        past_ref[:, p * PAGE_SIZE:(p + 1) * PAGE_SIZE] = page_scores(pages[p][...].astype(BF16))

    @pl.when(pl.program_id(1) == 0)
    def _():
        s = page_scores(kinew_ref[...])
        qrow = lax.broadcasted_iota(jnp.int32, s.shape, 0)
        kcol = lax.broadcasted_iota(jnp.int32, s.shape, 1)
        new_ref[...] = jnp.where(kcol <= qrow, s, NEG_INF)


def _idx_sample(page_table, qi_stack, wcol, kinew, cache_kidx, *, s_q):
    db, n_pages = page_table.shape
    npg = PAGES_PER_STEP
    n_steps = n_pages // npg
    rows = IDX_HEADS * s_q
    page_spec = lambda r: pl.BlockSpec((None, PAGE_SIZE, IDX_DIM), lambda b, c, pt: (pt[b, c * npg + r], 0, 0))
    grid_spec = pltpu.PrefetchScalarGridSpec(
        num_scalar_prefetch=1,
        grid=(db, n_steps),
        in_specs=[pl.BlockSpec((None, rows, IDX_DIM), lambda b, c, pt: (b, 0, 0)),
                  pl.BlockSpec((None, rows, LANES), lambda b, c, pt: (b, 0, 0)),
                  pl.BlockSpec((None, PAGE_SIZE, IDX_DIM), lambda b, c, pt: (b, 0, 0))]
                 + [page_spec(r) for r in range(npg)],
        out_specs=(pl.BlockSpec((None, None, s_q, npg * PAGE_SIZE), lambda b, c, pt: (c, b, 0, 0)),
                   pl.BlockSpec((None, s_q, PAGE_SIZE), lambda b, c, pt: (b, 0, 0))),
    )
    return pl.pallas_call(
        _idx_sample_kernel,
        grid_spec=grid_spec,
        out_shape=(jax.ShapeDtypeStruct((n_steps, db, s_q, npg * PAGE_SIZE), F32),
                   jax.ShapeDtypeStruct((db, s_q, PAGE_SIZE), F32)),
        compiler_params=_cparams(2),
        name="idx_sample",
    )(page_table, qi_stack, wcol, kinew, *([cache_kidx] * npg))


def _sel_sample_kernel(past_ref, new_ref, thr_ref, j_ref, *, topk, n_pos_bits):
    n_c, g, s_q, w = past_ref.shape
    rows = g * s_q
    l_past = n_c * w
    lane_pos = lax.broadcasted_iota(jnp.int32, (rows, LANES), 1)

    def lane_tile_sum(v):
        return functools.reduce(lambda a, b: a + b, [v[:, t * LANES:(t + 1) * LANES] for t in range(v.shape[1] // LANES)])

    def count(pred):
        def body(cc, acc):
            x = past_ref[cc].reshape(rows, w)
            pos = cc * w + lax.broadcasted_iota(jnp.int32, (rows, w), 1)
            return acc + lane_tile_sum(jnp.where(pred(x, pos), 1.0, 0.0))
        acc = lax.fori_loop(0, n_c, body, jnp.zeros((rows, LANES), F32))
        acc = acc + jnp.where(pred(new_ref[...].reshape(rows, PAGE_SIZE), l_past + lane_pos), 1.0, 0.0)
        return jnp.sum(acc, axis=1, keepdims=True)

    def bit_body(t, res):
        trial = res | jnp.left_shift(jnp.int32(1), 31 - t)
        cand = _ukey_to_f32(trial)
        return jnp.where(count(lambda x, pos: x >= cand) >= topk, trial, res)

    res = lax.fori_loop(0, 32, bit_body, jnp.zeros((rows, 1), jnp.int32))
    thr = _ukey_to_f32(res)
    n_ties = topk - count(lambda x, pos: x > thr)

    def jbit_body(t, resj):
        trial = resj | jnp.left_shift(jnp.int32(1), n_pos_bits - 1 - t)
        below = count(lambda x, pos: (x == thr) & (pos < trial))
        return jnp.where(below < n_ties, trial, resj)

    jmax = lax.fori_loop(0, n_pos_bits, jbit_body, jnp.zeros((rows, 1), jnp.int32))
    thr_ref[...] = jnp.broadcast_to(thr, (rows, LANES))
    j_ref[...] = jnp.broadcast_to(jmax, (rows, LANES))


def _sel_sample(sc_past, sc_new, *, topk):
    n_c, db, s_q, w = sc_past.shape
    g = math.gcd(SEL_SEQS, db)
    kern = functools.partial(_sel_sample_kernel, topk=topk,
                             n_pos_bits=max(1, math.ceil(math.log2(n_c * w + PAGE_SIZE))))
    return pl.pallas_call(
        kern,
        grid=(db // g,),
        in_specs=[pl.BlockSpec((n_c, g, s_q, w), lambda i: (0, i, 0, 0)),
                  pl.BlockSpec((g, s_q, PAGE_SIZE), lambda i: (i, 0, 0))],
        out_specs=(pl.BlockSpec((g * s_q, LANES), lambda i: (i, 0)),
                   pl.BlockSpec((g * s_q, LANES), lambda i: (i, 0))),
        out_shape=(jax.ShapeDtypeStruct((db * s_q, LANES), F32),
                   jax.ShapeDtypeStruct((db * s_q, LANES), jnp.int32)),
        compiler_params=_cparams(1),
        name="sel_sample",
    )(sc_past, sc_new)


def _att_sample_kernel(pt_ref, qbd_ref, sc_past_ref, sc_new_ref, thr_ref, j_ref, knew_ref, vnew_ref, *rest,
                       l_past):
    npg = PAGES_PER_STEP
    kpages = rest[:npg]
    vpages = rest[npg:2 * npg]
    att_ref, acc_ref, m_ref, l_ref = rest[2 * npg:]
    c = pl.program_id(1)
    qbd = qbd_ref[...]
    thr = thr_ref[...]
    jmax = j_ref[...]

    @pl.when(c == 0)
    def _():
        acc_ref[...] = jnp.zeros(acc_ref.shape, F32)
        m_ref[...] = jnp.full(m_ref.shape, NEG_INF, F32)
        l_ref[...] = jnp.zeros(l_ref.shape, F32)

    def masked_scores(keys_bf16, sc, pos0):
        pos = pos0 + lax.broadcasted_iota(jnp.int32, sc.shape, 1)
        sel = (sc > thr) | ((sc == thr) & (pos <= jmax))
        bias = jnp.where(sel, 0.0, NEG_INF)
        s = lax.dot_general(qbd, keys_bf16, NT_DIMS, preferred_element_type=F32)
        return s + jnp.concatenate([bias] * N_HEADS, axis=0)

    def accumulate(s_list, v_list):
        m_old = m_ref[...]
        m_new = m_old
        for s in s_list:
            m_new = jnp.maximum(m_new, jnp.max(s, axis=1, keepdims=True))
        m_safe = jnp.maximum(m_new, -1e30)
        alpha = jnp.exp(m_old - m_safe)
        acc = acc_ref[...] * alpha
        l = l_ref[...] * alpha
        for s, v in zip(s_list, v_list):
            p = jnp.exp(s - m_safe)
            l = l + jnp.sum(p, axis=1, keepdims=True)
            acc = acc + jnp.dot(p.astype(BF16), v, preferred_element_type=F32)
        acc_ref[...] = acc
        l_ref[...] = l
        m_ref[...] = m_new

    s_list, v_list = [], []
    for p in range(npg):
        sc = sc_past_ref[:, p * PAGE_SIZE:(p + 1) * PAGE_SIZE]
        pos0 = (c * npg + p) * PAGE_SIZE
        s_list.append(masked_scores(kpages[p][...].astype(BF16), sc, pos0))
        v_list.append(vpages[p][...].astype(BF16))
    accumulate(s_list, v_list)

    @pl.when(c == pl.num_programs(1) - 1)
    def _():
        accumulate([masked_scores(knew_ref[...], sc_new_ref[...], l_past)], [vnew_ref[...]])
        att_ref[...] = acc_ref[...] / l_ref[...]


def _att_sample(page_table, qbd, sc_past, sc_new, thr, jmax, knew, vnew, cache_k, cache_v, *, s_q):
    db, n_pages = page_table.shape
    npg = PAGES_PER_STEP
    n_steps = n_pages // npg
    rows = N_HEADS * s_q
    page_spec = lambda r: pl.BlockSpec((None, PAGE_SIZE, KV_WIDTH), lambda b, c, pt: (pt[b, c * npg + r], 0, 0))
    per_seq = lambda r, n: pl.BlockSpec((None, r, n), lambda b, c, pt: (b, 0, 0))
    grid_spec = pltpu.PrefetchScalarGridSpec(
        num_scalar_prefetch=1,
        grid=(db, n_steps),
        in_specs=[per_seq(rows, KV_WIDTH),
                  pl.BlockSpec((None, None, s_q, npg * PAGE_SIZE), lambda b, c, pt: (c, b, 0, 0)),
                  per_seq(s_q, PAGE_SIZE),
                  pl.BlockSpec((s_q, LANES), lambda b, c, pt: (b, 0)),
                  pl.BlockSpec((s_q, LANES), lambda b, c, pt: (b, 0)),
                  per_seq(PAGE_SIZE, KV_WIDTH), per_seq(PAGE_SIZE, KV_WIDTH)]
                 + [page_spec(r) for r in range(npg)] * 2,
        out_specs=per_seq(rows, KV_WIDTH),
        scratch_shapes=[pltpu.VMEM((rows, KV_WIDTH), F32), pltpu.VMEM((rows, 1), F32), pltpu.VMEM((rows, 1), F32)],
    )
    kern = functools.partial(_att_sample_kernel, l_past=n_pages * PAGE_SIZE)
    return pl.pallas_call(
        kern,
        grid_spec=grid_spec,
        out_shape=jax.ShapeDtypeStruct((db, rows, KV_WIDTH), F32),
        compiler_params=_cparams(2),
        name="att_sample",
    )(page_table, qbd, sc_past, sc_new, thr, jmax, knew, vnew, *([cache_k] * npg), *([cache_v] * npg))


def _gla_kernel(*refs, chunk, n_chunks, off, t_end, has_s0):
    if has_s0:
        gq_ref, gk_ref, gv_ref, og_ref, la_ref, gn_ref, s0_ref, o_ref, sfin_ref, st_ref = refs
    else:
        gq_ref, gk_ref, gv_ref, og_ref, la_ref, gn_ref, o_ref, sfin_ref, st_ref = refs
    j = pl.program_id(1)
    tb = chunk * n_chunks

    @pl.when(j == 0)
    def _():
        for hh in range(GLA_HEADS):
            st_ref[hh] = s0_ref[hh].T if has_s0 else jnp.zeros((GLA_DV, GLA_DK), F32)

    r_i = lax.broadcasted_iota(jnp.int32, (chunk, chunk), 0)
    c_i = lax.broadcasted_iota(jnp.int32, (chunk, chunk), 1)
    causal = r_i >= c_i
    tril = jnp.where(causal, 1.0, 0.0)
    gn = gn_ref[...]

    def chunk_body(c, carry):
        r0 = pl.multiple_of(c * chunk, chunk)
        rows = pl.ds(r0, chunk)
        pos = j * tb + r0 + lax.broadcasted_iota(jnp.int32, (chunk, GLA_KW), 0)
        valid = (pos >= off) & (pos < t_end)
        la = jnp.where(valid, la_ref[rows, :], 0.0)
        k = jnp.where(valid, gk_ref[rows, :], 0.0)
        q = gq_ref[rows, :] * GLA_DK ** -0.5
        v = gv_ref[rows, :]
        b = jnp.dot(tril, la, preferred_element_type=F32, precision=lax.Precision.HIGHEST)
        b_last = b[chunk - 1:chunk, :]
        qd = (q * jnp.exp(b)).astype(BF16)
        kd = (k * jnp.exp(-b)).astype(BF16)
        ke = (k * jnp.exp(b_last - b)).astype(BF16)
        decay = jnp.exp(b_last)
        outs = []
        for hh in range(GLA_HEADS):
            ks = slice(hh * GLA_DK, (hh + 1) * GLA_DK)
            vh = v[:, hh * GLA_DV:(hh + 1) * GLA_DV]
            a = lax.dot_general(qd[:, ks], kd[:, ks], NT_DIMS, preferred_element_type=F32)
            a = jnp.where(causal, a, 0.0).astype(BF16)
            st = st_ref[hh]
            o = (jnp.dot(a, vh, preferred_element_type=F32)
                 + lax.dot_general(qd[:, ks], st.astype(BF16), NT_DIMS, preferred_element_type=F32))
            u_t = lax.dot_general(vh, ke[:, ks], TN_DIMS, preferred_element_type=F32)
            st_ref[hh] = decay[:, ks] * st + u_t
            o = o * lax.rsqrt(jnp.mean(o * o, axis=-1, keepdims=True) + LN_EPS) * gn
            outs.append(o)
        og = og_ref[rows, :]
        o_ref[rows, :] = (jnp.concatenate(outs, axis=1) * (og * jax.nn.sigmoid(og))).astype(BF16)
        return carry

    lax.fori_loop(0, n_chunks, chunk_body, 0)

    @pl.when(j == pl.num_programs(1) - 1)
    def _():
        for hh in range(GLA_HEADS):
            sfin_ref[hh] = st_ref[hh].T


def _gla(gq, gk, gv, og, la, gnorm, s0, *, batch, t_pad, tb, chunk, off, t_end):
    n_steps = t_pad // tb
    tok = lambda n: pl.BlockSpec((tb, n), lambda b, j: (b * n_steps + j, 0))
    state = pl.BlockSpec((None, GLA_HEADS, GLA_DK, GLA_DV), lambda b, j: (b, 0, 0, 0))
    has_s0 = s0 is not None
    kern = functools.partial(_gla_kernel, chunk=chunk, n_chunks=tb // chunk, off=off, t_end=t_end, has_s0=has_s0)
    in_specs = [tok(GLA_KW), tok(GLA_KW), tok(GLA_VW), tok(GLA_VW), tok(GLA_KW), _full_spec(gnorm.shape)]
    args = [gq, gk, gv, og, la, gnorm]
    if has_s0:
        in_specs.append(state)
        args.append(s0)
    return pl.pallas_call(
        kern,
        grid=(batch, n_steps),
        in_specs=in_specs,
        out_specs=(tok(GLA_VW), state),
        out_shape=(jax.ShapeDtypeStruct((batch * t_pad, GLA_VW), BF16),
                   jax.ShapeDtypeStruct((batch, GLA_HEADS, GLA_DK, GLA_DV), F32)),
        scratch_shapes=[pltpu.VMEM((GLA_HEADS, GLA_DV, GLA_DK), F32)],
        compiler_params=_cparams(2),
        name="gla",
    )(*args)


def _merge_kernel(x_ref, att_ref, gla_ref, gates_ref, lng_ref, lnb_ref, wa_ref, wg_ref, wo_ref,
                  l1g_ref, l1b_ref, h1_ref, *, alpha):
    h = _layer_norm(x_ref[...], lng_ref[...], lnb_ref[...])
    pa = jnp.dot(att_ref[...], wa_ref[...], preferred_element_type=F32)
    pg = jnp.dot(gla_ref[...], wg_ref[...], preferred_element_type=F32)
    gates = gates_ref[...].astype(F32)
    merged = gates[:, :D_MODEL] * pa + gates[:, D_MODEL:] * pg
    mix = jnp.dot(merged.astype(BF16), wo_ref[...], preferred_element_type=F32)
    h1_ref[...] = _layer_norm(alpha * h + mix, l1g_ref[...], l1b_ref[...])


def _merge(x, att, gla, gates, ln_g, ln_b, wa, wg, wo, l1g, l1b, *, alpha):
    n_tok = x.shape[0]
    tm = _row_tile(n_tok, TM_TOKEN)
    row = lambda n: pl.BlockSpec((tm, n), lambda i: (i, 0))
    consts = (ln_g, ln_b, wa, wg, wo, l1g, l1b)
    return pl.pallas_call(
        functools.partial(_merge_kernel, alpha=alpha),
        grid=(n_tok // tm,),
        in_specs=[row(D_MODEL), row(ATT_WIDTH), row(GLA_VW), row(N_BRANCHES * D_MODEL)]
                 + [_full_spec(a.shape) for a in consts],
        out_specs=row(D_MODEL),
        out_shape=jax.ShapeDtypeStruct((n_tok, D_MODEL), F32),
        compiler_params=_cparams(1),
        name="merge",
    )(x, att, gla, gates, *consts)


def _ffn_kernel(h_ref, wu_ref, wd_ref, g_ref, b_ref, y_ref, *, alpha, n_split):
    h = h_ref[...]
    hb = h.astype(BF16)
    w = D_FF // n_split
    ff = jnp.zeros(h.shape, F32)
    for s in range(n_split):
        u = jnp.dot(hb, wu_ref[:, s * w:(s + 1) * w], preferred_element_type=F32)
        u = jnp.square(jnp.maximum(u, 0.0)).astype(BF16)
        ff = ff + jnp.dot(u, wd_ref[s * w:(s + 1) * w, :], preferred_element_type=F32)
    y_ref[...] = _layer_norm(alpha * h + ff, g_ref[...], b_ref[...])


def _ffn(h1, wu, wd, g, b, *, alpha):
    n_tok = h1.shape[0]
    tm = _row_tile(n_tok, TM_TOKEN)
    row = pl.BlockSpec((tm, D_MODEL), lambda i: (i, 0))
    return pl.pallas_call(
        functools.partial(_ffn_kernel, alpha=alpha, n_split=4),
        grid=(n_tok // tm,),
        in_specs=[row] + [_full_spec(a.shape) for a in (wu, wd, g, b)],
        out_specs=row,
        out_shape=jax.ShapeDtypeStruct((n_tok, D_MODEL), F32),
        compiler_params=_cparams(1),
        name="ffn",
    )(h1, wu, wd, g, b)


def _pack_weights(w_in, w_gla_a2, b_gla_a):
    points = []
    acc = 0
    for s in IN_SIZES[:-1]:
        acc += s
        points.append(acc)
    wq, wk, wv, wqi, wki, wwi, wgq, wgk, wgv, wog, wa1, wgt = jnp.split(w_in, points, axis=-1)
    wq = wq * HEAD_DIM ** -0.5
    wqi = wqi * IDX_DIM ** -0.5
    pad_cols = lambda a, n: jnp.pad(a, ((0, 0), (0, n - a.shape[1])))
    wv_heads = wv.T.reshape(N_KV_HEADS, HEAD_DIM, D_MODEL)
    wv_aug = jnp.pad(wv_heads, ((0, 0), (0, LANES - HEAD_DIM), (0, 0))).reshape(N_KV_HEADS * LANES, D_MODEL)
    w = {
        "qT": wq.T, "qiT": wqi.T, "vT": wv_aug,
        "wiT": jnp.pad(wwi.T, ((0, 2 * SUBLANES - IDX_HEADS), (0, 0))),
        "row_p": pad_cols(jnp.concatenate([wk, wv, wki], axis=1), 2 * KV_WIDTH + LANES),
        "row_s": pad_cols(jnp.concatenate([wq, wqi, wk, wv, wki, wwi], axis=1),
                          ATT_WIDTH + IDX_HEADS * IDX_DIM + 2 * KV_WIDTH + LANES),
        "gla": jnp.concatenate([wgq, wgk, wgv, wog], axis=1),
        "a1": pad_cols(wa1, LANES),
        "a2": jnp.pad(w_gla_a2, ((0, LANES - GLA_GATE_RANK), (0, 0))),
        "gt": wgt,
    }
    w = {name: a.astype(BF16) for name, a in w.items()}
    w["ba"] = b_gla_a.astype(F32)[None]
    return w


def _round_up(x, m):
    return -(-x // m) * m


def _row_tile(n, pref, unit=2 * SUBLANES):
    best = unit
    for t in range(unit, min(n, pref) + 1, unit):
        if n % t == 0:
            best = t
    assert n % best == 0
    return best


def kernel(x_prompt, x_sample, cache_k, cache_v, cache_kidx, state_gla, page_table, meta_tokens, ln_in_g, ln_in_b, w_in, w_gla_a2, b_gla_a, gla_norm_g, w_proj_attn, w_proj_gla, w_out, ln1_g, ln1_b, w_ff_up, w_ff_down, ln2_g, ln2_b):
    depth = w_in.shape[0]
    assert depth == 1, "single-layer step only"
    B, S_p, D = x_prompt.shape
    DB, S_s, _ = x_sample.shape
    n_pages = page_table.shape[1]
    past = n_pages * PAGE_SIZE
    assert D == D_MODEL and S_p % GLA_CHUNK == 0 and S_s <= SUBLANES and n_pages % PAGES_PER_STEP == 0
    topk_prompt = min(TOPK_MAX, S_p // 4)
    topk_sample = min(TOPK_MAX, (past + S_s) // 4)
    alpha = (2 * depth) ** 0.25

    row = lambda a: a.astype(F32).reshape(1, -1)
    ln_g, ln_b = row(ln_in_g), row(ln_in_b)
    w = _pack_weights(w_in[0], w_gla_a2[0], b_gla_a[0])
    wa, wg, wo = (a[0].astype(BF16) for a in (w_proj_attn, w_proj_gla, w_out))
    wu, wd = w_ff_up[0].astype(BF16), w_ff_down[0].astype(BF16)
    gnorm = row(gla_norm_g[0])
    l1g, l1b, l2g, l2b = row(ln1_g[0]), row(ln1_b[0]), row(ln2_g[0]), row(ln2_b[0])

    T = S_p + N_META
    off = (-N_META) % GLA_CHUNK
    t_pad = _round_up(off + T, math.lcm(KEY_CHUNK, GLA_CHUNK))
    gla_tb = _row_tile(t_pad, GLA_TOKENS_PER_STEP, GLA_CHUNK)
    n_tok = B * t_pad
    meta = jnp.broadcast_to(meta_tokens.astype(x_prompt.dtype)[None], (B, N_META, D))
    xp = jnp.concatenate([jnp.zeros((B, off, D), x_prompt.dtype), meta, x_prompt,
                          jnp.zeros((B, t_pad - off - T, D), x_prompt.dtype)], axis=1).reshape(n_tok, D)

    qT, qiT, vT3, wiT, k32, v32, ki32, kb, kib = _proj_attn_prompt(xp, ln_g, ln_b, w, n_tok)
    att_p = _dsa_prompt(qT, qiT, wiT,
                        kb.reshape(n_tok // KEY_CHUNK, KEY_CHUNK, KV_WIDTH),
                        kib.reshape(n_tok // KEY_CHUNK, KEY_CHUNK, IDX_DIM), vT3,
                        batch=B, t_pad=t_pad, topk=topk_prompt, off=off)
    gq, gk, gv, og, la, gates_p = _proj_gla(xp, ln_g, ln_b, w, n_tok)
    gla_p, state_p = _gla(gq, gk, gv, og, la, gnorm, None, batch=B, t_pad=t_pad, tb=gla_tb,
                          chunk=GLA_CHUNK, off=off, t_end=off + T)
    h1_p = _merge(xp, att_p, gla_p, gates_p, ln_g, ln_b, wa, wg, wo, l1g, l1b, alpha=alpha)
    y_p = _ffn(h1_p, wu, wd, l2g, l2b, alpha=alpha)

    seq = lambda a: a.reshape((B, t_pad) + a.shape[1:])
    y_prompt = seq(y_p)[:, off + N_META:off + T]
    k_prompt = seq(k32)[:, off:off + T].reshape(1, B, T, N_KV_HEADS, HEAD_DIM)
    v_prompt = seq(v32)[:, off:off + T].reshape(1, B, T, N_KV_HEADS, HEAD_DIM)
    kidx_prompt = seq(ki32)[:, off:off + T][None]
    gla_state_prompt = state_p[None]

    R = SAMPLE_ROWS
    n_tok_s = DB * R
    xs = jnp.pad(x_sample, ((0, 0), (0, R - S_s), (0, 0))).reshape(n_tok_s, D)
    q_s, qi_s, k_s, v_s, kiwi_s = _proj_attn_sample(xs, ln_g, ln_b, w, n_tok_s)
    sseq = lambda a: a.reshape((DB, R) + a.shape[1:])[:, :S_s]
    k_new, v_new = sseq(k_s), sseq(v_s)
    ki_new = sseq(kiwi_s)[..., :IDX_DIM]
    wi_new = sseq(kiwi_s)[..., IDX_DIM:IDX_DIM + IDX_HEADS]
    qi_stack = sseq(qi_s).reshape(DB, S_s, IDX_HEADS, IDX_DIM).transpose(0, 2, 1, 3).reshape(DB, IDX_HEADS * S_s, IDX_DIM)
    wcol = jnp.broadcast_to(wi_new.transpose(0, 2, 1).reshape(DB, IDX_HEADS * S_s, 1), (DB, IDX_HEADS * S_s, LANES))
    q_heads = sseq(q_s).reshape(DB, S_s, N_HEADS, HEAD_DIM).transpose(0, 2, 1, 3)
    kv_of_head = (jnp.arange(N_HEADS) // GROUP)[:, None] == jnp.arange(N_KV_HEADS)[None, :]
    qbd = jnp.where(kv_of_head[None, :, None, :, None], q_heads[:, :, :, None, :], jnp.zeros((), BF16))
    qbd = qbd.reshape(DB, N_HEADS * S_s, KV_WIDTH)
    pad_page = lambda a: jnp.pad(a, ((0, 0), (0, PAGE_SIZE - S_s), (0, 0))).astype(BF16)
    kinew_pg, knew_pg, vnew_pg = pad_page(ki_new), pad_page(k_new), pad_page(v_new)

    ck = cache_k[0].reshape(-1, PAGE_SIZE, KV_WIDTH)
    cv = cache_v[0].reshape(-1, PAGE_SIZE, KV_WIDTH)
    sc_past, sc_new = _idx_sample(page_table, qi_stack, wcol, kinew_pg, cache_kidx[0], s_q=S_s)
    thr, jmax = _sel_sample(sc_past, sc_new, topk=topk_sample)
    o_s = _att_sample(page_table, qbd, sc_past, sc_new, thr, jmax, knew_pg, vnew_pg, ck, cv, s_q=S_s)
    o_s = o_s.reshape(DB, N_KV_HEADS, GROUP, S_s, N_KV_HEADS, HEAD_DIM)
    att_s = jnp.stack([o_s[:, n, :, :, n, :] for n in range(N_KV_HEADS)], axis=1)
    att_s = att_s.transpose(0, 3, 1, 2, 4).reshape(DB, S_s, ATT_WIDTH).astype(BF16)
    att_s = jnp.pad(att_s, ((0, 0), (0, R - S_s), (0, 0))).reshape(n_tok_s, ATT_WIDTH)

    gq, gk, gv, og, la, gates_s = _proj_gla(xs, ln_g, ln_b, w, n_tok_s)
    gla_s, state_s = _gla(gq, gk, gv, og, la, gnorm, state_gla[0], batch=DB, t_pad=R, tb=R,
                          chunk=R, off=0, t_end=S_s)
    h1_s = _merge(xs, att_s, gla_s, gates_s, ln_g, ln_b, wa, wg, wo, l1g, l1b, alpha=alpha)
    y_s = _ffn(h1_s, wu, wd, l2g, l2b, alpha=alpha)

    y_sample = y_s.reshape(DB, R, D)[:, :S_s]
    k_sample = k_new.reshape(1, DB, S_s, N_KV_HEADS, HEAD_DIM)
    v_sample = v_new.reshape(1, DB, S_s, N_KV_HEADS, HEAD_DIM)
    kidx_sample = ki_new[None]
    gla_state_sample = state_s[None]
    return (y_prompt, y_sample, k_prompt, v_prompt, kidx_prompt, gla_state_prompt,
            k_sample, v_sample, kidx_sample, gla_state_sample)
```

```python
import functools
import math

import jax
import jax.numpy as jnp
from jax import lax
from jax.experimental import pallas as pl
from jax.experimental.pallas import tpu as pltpu

D_MODEL = 1024
PAGE_SIZE = 128
N_META = 16
N_HEADS = 16
HEAD_DIM = 64
N_KV_HEADS = 4
GROUP = N_HEADS // N_KV_HEADS
ATT_WIDTH = N_HEADS * HEAD_DIM
KV_WIDTH = N_KV_HEADS * HEAD_DIM
IDX_HEADS = 8
IDX_DIM = 64
TOPK_MAX = 256
GLA_HEADS = 4
GLA_DK = D_MODEL // 2 // GLA_HEADS
GLA_DV = D_MODEL // GLA_HEADS
GLA_KW = GLA_HEADS * GLA_DK
GLA_VW = GLA_HEADS * GLA_DV
GLA_GATE_RANK = 16
GLA_TAU = 16.0
GLA_CHUNK = 64
N_BRANCHES = 2
D_FF = 4 * D_MODEL
LN_EPS = 1e-5
IN_SIZES = (ATT_WIDTH, KV_WIDTH, KV_WIDTH, IDX_HEADS * IDX_DIM, IDX_DIM, IDX_HEADS,
            GLA_KW, GLA_KW, GLA_VW, GLA_VW, GLA_GATE_RANK, N_BRANCHES * D_MODEL)

LANES = 128
SUBLANES = 8
VMEM_LIMIT_BYTES = 56 * 1024 * 1024
COUNT_LANES = 4
Q_BLOCK = LANES
KEY_CHUNK = 3 * LANES
SAMPLE_ROWS = 16
PAGES_PER_STEP = 16
SEL_SEQS = 16
TM_PROJ_GLA = 256
TM_TOKEN = 512
GLA_TOKENS_PER_STEP = 11 * GLA_CHUNK

F32 = jnp.float32
BF16 = jnp.bfloat16
NEG_INF = float("-inf")
INT_MIN = -2 ** 31
NT_DIMS = (((1,), (1,)), ((), ()))
TN_DIMS = (((0,), (0,)), ((), ()))


def _cparams(n_grid):
    return pltpu.CompilerParams(dimension_semantics=("arbitrary",) * n_grid,
                                vmem_limit_bytes=VMEM_LIMIT_BYTES)


def _full_spec(shape):
    nd = len(shape)
    return pl.BlockSpec(shape, lambda *_: (0,) * nd)


def _layer_norm(x, g, b):
    mu = jnp.mean(x, axis=-1, keepdims=True)
    xc = x - mu
    var = jnp.mean(xc * xc, axis=-1, keepdims=True)
    return xc * lax.rsqrt(var + LN_EPS) * g + b


def _ukey_to_f32(u):
    bits = jnp.where(u < 0, u & jnp.int32(0x7FFFFFFF), ~u)
    return lax.bitcast_convert_type(bits, F32)


def _proj_attn_prompt_kernel(x_ref, g_ref, b_ref, wq_ref, wqi_ref, wv_ref, wwi_ref, wrow_ref,
                             qT_ref, qiT_ref, vT_ref, wiT_ref, k_ref, v_ref, ki_ref, kb_ref, kib_ref):
    hb = _layer_norm(x_ref[...], g_ref[...], b_ref[...]).astype(BF16)
    qT_ref[...] = lax.dot_general(wq_ref[...], hb, NT_DIMS, preferred_element_type=F32).astype(BF16)
    qiT_ref[...] = lax.dot_general(wqi_ref[...], hb, NT_DIMS, preferred_element_type=F32).astype(BF16)
    vt = lax.dot_general(wv_ref[...], hb, NT_DIMS, preferred_element_type=F32)
    row = lax.broadcasted_iota(jnp.int32, vt.shape, 0)
    vt = jnp.where(row % LANES == HEAD_DIM, 1.0, vt)
    vT_ref[0] = vt.astype(BF16)
    wi = lax.dot_general(wwi_ref[...], hb, NT_DIMS, preferred_element_type=F32)
    wiT_ref[...] = wi[:IDX_HEADS] * IDX_HEADS ** -0.5
    y = jnp.dot(hb, wrow_ref[...], preferred_element_type=F32)
    k = y[:, :KV_WIDTH]
    ki = y[:, 2 * KV_WIDTH:2 * KV_WIDTH + IDX_DIM]
    k_ref[...] = k
    v_ref[...] = y[:, KV_WIDTH:2 * KV_WIDTH]
    ki_ref[...] = ki
    kb_ref[...] = k.astype(BF16)
    kib_ref[...] = ki.astype(BF16)


def _proj_attn_prompt(x, ln_g, ln_b, w, n_tok):
    tm = KEY_CHUNK
    n_steps = n_tok // tm
    row = lambda n: pl.BlockSpec((tm, n), lambda i: (i, 0))
    col = lambda n: pl.BlockSpec((n, tm), lambda i: (0, i))
    out_shape = (
        jax.ShapeDtypeStruct((ATT_WIDTH, n_tok), BF16),
        jax.ShapeDtypeStruct((IDX_HEADS * IDX_DIM, n_tok), BF16),
        jax.ShapeDtypeStruct((n_steps, N_KV_HEADS * LANES, tm), BF16),
        jax.ShapeDtypeStruct((IDX_HEADS, n_tok), F32),
        jax.ShapeDtypeStruct((n_tok, KV_WIDTH), F32),
        jax.ShapeDtypeStruct((n_tok, KV_WIDTH), F32),
        jax.ShapeDtypeStruct((n_tok, IDX_DIM), F32),
        jax.ShapeDtypeStruct((n_tok, KV_WIDTH), BF16),
        jax.ShapeDtypeStruct((n_tok, IDX_DIM), BF16),
    )
    out_specs = (col(ATT_WIDTH), col(IDX_HEADS * IDX_DIM),
                 pl.BlockSpec((1, N_KV_HEADS * LANES, tm), lambda i: (i, 0, 0)),
                 col(IDX_HEADS), row(KV_WIDTH), row(KV_WIDTH), row(IDX_DIM), row(KV_WIDTH), row(IDX_DIM))
    ws = (w["qT"], w["qiT"], w["vT"], w["wiT"], w["row_p"])
    return pl.pallas_call(
        _proj_attn_prompt_kernel,
        grid=(n_steps,),
        in_specs=[row(D_MODEL), _full_spec(ln_g.shape), _full_spec(ln_b.shape)] + [_full_spec(a.shape) for a in ws],
        out_specs=out_specs,
        out_shape=out_shape,
        compiler_params=_cparams(1),
        name="proj_attn_prompt",
    )(x, ln_g, ln_b, *ws)


def _proj_attn_sample_kernel(x_ref, g_ref, b_ref, w_ref, scale_ref, q_ref, qi_ref, k_ref, v_ref, kiwi_ref):
    hb = _layer_norm(x_ref[...], g_ref[...], b_ref[...]).astype(BF16)
    y = jnp.dot(hb, w_ref[...], preferred_element_type=F32)
    o = 0
    q_ref[...] = y[:, o:o + ATT_WIDTH].astype(BF16)
    o += ATT_WIDTH
    qi_ref[...] = y[:, o:o + IDX_HEADS * IDX_DIM].astype(BF16)
    o += IDX_HEADS * IDX_DIM
    k_ref[...] = y[:, o:o + KV_WIDTH]
    o += KV_WIDTH
    v_ref[...] = y[:, o:o + KV_WIDTH]
    o += KV_WIDTH
    kiwi_ref[...] = y[:, o:o + LANES] * scale_ref[...]


def _proj_attn_sample(x, ln_g, ln_b, w, n_tok):
    tm = _row_tile(n_tok, TM_TOKEN)
    row = lambda n: pl.BlockSpec((tm, n), lambda i: (i, 0))
    lane = lax.iota(jnp.int32, LANES)
    scale = jnp.where((lane >= IDX_DIM) & (lane < IDX_DIM + IDX_HEADS), IDX_HEADS ** -0.5, 1.0).astype(F32)[None]
    out_shape = (
        jax.ShapeDtypeStruct((n_tok, ATT_WIDTH), BF16),
        jax.ShapeDtypeStruct((n_tok, IDX_HEADS * IDX_DIM), BF16),
        jax.ShapeDtypeStruct((n_tok, KV_WIDTH), F32),
        jax.ShapeDtypeStruct((n_tok, KV_WIDTH), F32),
        jax.ShapeDtypeStruct((n_tok, LANES), F32),
    )
    return pl.pallas_call(
        _proj_attn_sample_kernel,
        grid=(n_tok // tm,),
        in_specs=[row(D_MODEL), _full_spec(ln_g.shape), _full_spec(ln_b.shape),
                  _full_spec(w["row_s"].shape), _full_spec(scale.shape)],
        out_specs=(row(ATT_WIDTH), row(IDX_HEADS * IDX_DIM), row(KV_WIDTH), row(KV_WIDTH), row(LANES)),
        out_shape=out_shape,
        compiler_params=_cparams(1),
        name="proj_attn_sample",
    )(x, ln_g, ln_b, w["row_s"], scale)


def _log_sigmoid(x):
    return jnp.minimum(x, 0.0) - jnp.log1p(jnp.exp(-jnp.abs(x)))


def _proj_gla_kernel(x_ref, g_ref, b_ref, wg_ref, wa1_ref, wa2_ref, ba_ref, wgt_ref,
                     gq_ref, gk_ref, gv_ref, og_ref, la_ref, gates_ref):
    hb = _layer_norm(x_ref[...], g_ref[...], b_ref[...]).astype(BF16)
    y = jnp.dot(hb, wg_ref[...], preferred_element_type=F32)
    gq_ref[...] = y[:, :GLA_KW]
    gk_ref[...] = y[:, GLA_KW:2 * GLA_KW]
    gv_ref[...] = y[:, 2 * GLA_KW:2 * GLA_KW + GLA_VW].astype(BF16)
    og_ref[...] = y[:, 2 * GLA_KW + GLA_VW:]
    a1 = jnp.dot(hb, wa1_ref[...], preferred_element_type=F32).astype(BF16)
    z = jnp.dot(a1, wa2_ref[...], preferred_element_type=F32) + ba_ref[...]
    la_ref[...] = _log_sigmoid(z) / GLA_TAU
    gt = jnp.dot(hb, wgt_ref[...], preferred_element_type=F32)
    gates_ref[...] = jax.nn.sigmoid(gt).astype(BF16)


def _proj_gla(x, ln_g, ln_b, w, n_tok):
    tm = _row_tile(n_tok, TM_PROJ_GLA)
    row = lambda n: pl.BlockSpec((tm, n), lambda i: (i, 0))
    ws = (w["gla"], w["a1"], w["a2"], w["ba"], w["gt"])
    out_shape = (
        jax.ShapeDtypeStruct((n_tok, GLA_KW), F32),
        jax.ShapeDtypeStruct((n_tok, GLA_KW), F32),
        jax.ShapeDtypeStruct((n_tok, GLA_VW), BF16),
        jax.ShapeDtypeStruct((n_tok, GLA_VW), F32),
        jax.ShapeDtypeStruct((n_tok, GLA_KW), F32),
        jax.ShapeDtypeStruct((n_tok, N_BRANCHES * D_MODEL), BF16),
    )
    return pl.pallas_call(
        _proj_gla_kernel,
        grid=(n_tok // tm,),
        in_specs=[row(D_MODEL), _full_spec(ln_g.shape), _full_spec(ln_b.shape)] + [_full_spec(a.shape) for a in ws],
        out_specs=(row(GLA_KW), row(GLA_KW), row(GLA_VW), row(GLA_VW), row(GLA_KW), row(N_BRANCHES * D_MODEL)),
        out_shape=out_shape,
        compiler_params=_cparams(1),
        name="proj_gla",
    )(x, ln_g, ln_b, *ws)


def _dsa_prompt_kernel(qT_ref, qiT_ref, wiT_ref, kb_ref, kib_ref, vT_ref, att_ref,
                       sc_ref, qn_ref, acc_ref, j_ref, sa_ref, sb_ref, *, topk, off, n_pos_bits):
    kc = KEY_CHUNK
    i = pl.program_id(1)
    qpos0 = i * Q_BLOCK
    n_chunks = (qpos0 + Q_BLOCK + kc - 1) // kc
    qpos = qpos0 + lax.broadcasted_iota(jnp.int32, (1, Q_BLOCK), 1)

    def key_pos(c):
        return c * kc + lax.broadcasted_iota(jnp.int32, (kc, Q_BLOCK), 0)

    qn_ref[...] = jnp.zeros(qn_ref.shape, BF16)
    for h in range(N_HEADS):
        n = h // GROUP
        qn_ref[n * HEAD_DIM:(n + 1) * HEAD_DIM, h * Q_BLOCK:(h + 1) * Q_BLOCK] = qT_ref[h * HEAD_DIM:(h + 1) * HEAD_DIM, :]

    wi = wiT_ref[...]

    def idx_body(c, carry):
        kic = kib_ref[c]
        acc = jnp.zeros((kc, Q_BLOCK), F32)
        for hp in range(IDX_HEADS // 2):
            r0 = 2 * hp * IDX_DIM
            rhs = jnp.concatenate([qiT_ref[r0:r0 + IDX_DIM, :], qiT_ref[r0 + IDX_DIM:r0 + 2 * IDX_DIM, :]], axis=1)
            s = jnp.maximum(jnp.dot(kic, rhs, preferred_element_type=F32), 0.0)
            acc = acc + s[:, :Q_BLOCK] * wi[2 * hp:2 * hp + 1, :] + s[:, Q_BLOCK:] * wi[2 * hp + 1:2 * hp + 2, :]
        kp = key_pos(c)
        valid = (kp >= off) & (kp <= qpos)
        sc_ref[c] = jnp.where(valid, acc, NEG_INF)
        return carry

    lax.fori_loop(0, n_chunks, idx_body, 0)

    def count(pred):
        def body(c, accs):
            v = jnp.where(pred(c), 1.0, 0.0).reshape(COUNT_LANES, kc // (SUBLANES * COUNT_LANES), SUBLANES, Q_BLOCK)
            return tuple(a + jnp.sum(v[r], axis=0) for r, a in enumerate(accs))
        accs = lax.fori_loop(0, n_chunks, body, (jnp.zeros((SUBLANES, Q_BLOCK), F32),) * COUNT_LANES)
        return jnp.sum(functools.reduce(lambda a, b: a + b, accs), axis=0, keepdims=True)

    def bit_body(t, res):
        trial = res | jnp.left_shift(jnp.int32(1), 31 - t)
        cand = _ukey_to_f32(trial)
        cnt = count(lambda c: sc_ref[c] >= cand)
        return jnp.where(cnt >= topk, trial, res)

    res = lax.fori_loop(0, 32, bit_body, jnp.zeros((1, Q_BLOCK), jnp.int32))
    few = (qpos - off + 1) <= topk
    thr = jnp.where(few, NEG_INF, _ukey_to_f32(res))
    cnt_gt = count(lambda c: sc_ref[c] > thr)
    cnt_ge = count(lambda c: sc_ref[c] >= thr)
    n_ties = topk - cnt_gt
    tie_rows = jnp.where(few, 0.0, jnp.where(cnt_ge > topk, 1.0, 0.0))
    j_ref[...] = jnp.full(j_ref.shape, 2 ** 30, jnp.int32)

    @pl.when(jnp.max(tie_rows) > 0.0)
    def _():
        def jbit_body(t, resj):
            trial = resj | jnp.left_shift(jnp.int32(1), n_pos_bits - 1 - t)
            below = count(lambda c: (sc_ref[c] == thr) & (key_pos(c) < trial))
            return jnp.where(below < n_ties, trial, resj)
        j_ref[...] = lax.fori_loop(0, n_pos_bits, jbit_body, jnp.zeros((1, Q_BLOCK), jnp.int32))

    jmax = j_ref[...]

    def bias_body(c, carry):
        sc = sc_ref[c]
        kp = key_pos(c)
        valid = (kp >= off) & (kp <= qpos)
        sel = (sc > thr) | ((sc == thr) & (kp <= jmax))
        sc_ref[c] = jnp.where(valid & sel, 0.0, NEG_INF)
        return carry

    lax.fori_loop(0, n_chunks, bias_body, 0)

    acc_ref[...] = jnp.zeros(acc_ref.shape, F32)
    gw = GROUP * Q_BLOCK
    last = n_chunks - 1

    def score_stage(c, buf_ref, m_run):
        s = jnp.dot(kb_ref[c], qn_ref[...], preferred_element_type=F32) + jnp.concatenate([sc_ref[c]] * N_HEADS, axis=1)
        buf_ref[...] = s
        return jnp.maximum(m_run, jnp.max(s, axis=0, keepdims=True))

    def prob_stage(c, buf_ref, m_before, m_with, weight):
        m_safe = jnp.maximum(m_with, -1e30)
        alpha = jnp.exp2(m_before - m_safe)
        p = jnp.exp2(buf_ref[...] - m_safe).astype(BF16)
        vt = vT_ref[c]
        for n in range(N_KV_HEADS):
            cols = slice(n * gw, (n + 1) * gw)
            pv = jnp.dot(vt[n * LANES:(n + 1) * LANES, :], p[:, cols], preferred_element_type=F32)
            acc_ref[n] = acc_ref[n] * alpha[:, cols] + weight * pv

    def pair_body(t, carry):
        m_prev, m_cur = carry
        c0 = 2 * t
        c1 = jnp.minimum(c0 + 1, last)
        c2 = jnp.minimum(c0 + 2, last)
        m_1 = score_stage(c1, sb_ref, m_cur)
        prob_stage(c0, sa_ref, m_prev, m_cur, 1.0)
        m_2 = score_stage(c2, sa_ref, m_1)
        prob_stage(c1, sb_ref, m_cur, m_1, jnp.where(c0 + 1 <= last, 1.0, 0.0))
        return m_1, m_2

    m_none = jnp.full((1, N_HEADS * Q_BLOCK), NEG_INF, F32)
    lax.fori_loop(0, (n_chunks + 1) // 2, pair_body, (m_none, score_stage(0, sa_ref, m_none)))

    heads = []
    for n in range(N_KV_HEADS):
        a = acc_ref[n]
        denom = a[HEAD_DIM:HEAD_DIM + 1, :]
        o = a[:HEAD_DIM, :] / jnp.where(denom > 0.0, denom, 1.0)
        for g in range(GROUP):
            heads.append(o[:, g * Q_BLOCK:(g + 1) * Q_BLOCK])
    att_ref[...] = jnp.concatenate(heads, axis=0).T.astype(BF16)


def _dsa_prompt(qT, qiT, wiT, kb3, kib3, vT3, *, batch, t_pad, topk, off):
    kc = KEY_CHUNK
    nqb = t_pad // Q_BLOCK
    ncb = t_pad // kc
    n_tok = batch * t_pad
    qcol = lambda n: pl.BlockSpec((n, Q_BLOCK), lambda b, i: (0, b * nqb + i))
    kern = functools.partial(_dsa_prompt_kernel, topk=topk, off=off,
                             n_pos_bits=max(1, math.ceil(math.log2(t_pad))))
    return pl.pallas_call(
        kern,
        grid=(batch, nqb),
        in_specs=[qcol(ATT_WIDTH), qcol(IDX_HEADS * IDX_DIM), qcol(IDX_HEADS),
                  pl.BlockSpec((ncb, kc, KV_WIDTH), lambda b, i: (b, 0, 0)),
                  pl.BlockSpec((ncb, kc, IDX_DIM), lambda b, i: (b, 0, 0)),
                  pl.BlockSpec((ncb, N_KV_HEADS * LANES, kc), lambda b, i: (b, 0, 0))],
        out_specs=pl.BlockSpec((Q_BLOCK, ATT_WIDTH), lambda b, i: (b * nqb + i, 0)),
        out_shape=jax.ShapeDtypeStruct((n_tok, ATT_WIDTH), BF16),
        scratch_shapes=[pltpu.VMEM((ncb, kc, Q_BLOCK), F32),
                        pltpu.VMEM((KV_WIDTH, N_HEADS * Q_BLOCK), BF16),
                        pltpu.VMEM((N_KV_HEADS, LANES, GROUP * Q_BLOCK), F32),
                        pltpu.VMEM((1, Q_BLOCK), jnp.int32),
                        pltpu.VMEM((kc, N_HEADS * Q_BLOCK), F32),
                        pltpu.VMEM((kc, N_HEADS * Q_BLOCK), F32)],
        compiler_params=_cparams(2),
        name="dsa_prompt",
    )(qT, qiT, wiT, kb3, kib3, vT3)


def _idx_sample_kernel(pt_ref, qi_ref, wcol_ref, kinew_ref, *rest):
    npg = PAGES_PER_STEP
    pages = rest[:npg]
    past_ref, new_ref = rest[npg], rest[npg + 1]
    s_q = new_ref.shape[0]
    qi = qi_ref[...]
    wcol = wcol_ref[...]

    def page_scores(keys_t_bf16):
        s = jnp.dot(qi, keys_t_bf16, preferred_element_type=F32)
        s = jnp.maximum(s, 0.0) * wcol
        return jnp.sum(s.reshape(IDX_HEADS, s_q, LANES), axis=0)

    for p in range(npg):
        past_ref[:, p * PAGE_SIZE:(p + 1) * PAGE_SIZE] = page_scores(pages[p][...].astype(BF16))

    @pl.when(pl.program_id(1) == 0)
    def _():
        s = page_scores(kinew_ref[...])
        qrow = lax.broadcasted_iota(jnp.int32, s.shape, 0)
        kcol = lax.broadcasted_iota(jnp.int32, s.shape, 1)
        new_ref[...] = jnp.where(kcol <= qrow, s, NEG_INF)


def _idx_sample(page_table, qi_stack, wcol, kinew, cache_kidx, *, s_q):
    db, n_pages = page_table.shape
    npg = PAGES_PER_STEP
    n_steps = n_pages // npg
    rows = IDX_HEADS * s_q
    page_spec = lambda r: pl.BlockSpec((None, IDX_DIM, PAGE_SIZE), lambda b, c, pt: (pt[b, c * npg + r], 0, 0))
    grid_spec = pltpu.PrefetchScalarGridSpec(
        num_scalar_prefetch=1,
        grid=(db, n_steps),
        in_specs=[pl.BlockSpec((None, rows, IDX_DIM), lambda b, c, pt: (b, 0, 0)),
                  pl.BlockSpec((None, rows, LANES), lambda b, c, pt: (b, 0, 0)),
                  pl.BlockSpec((None, IDX_DIM, PAGE_SIZE), lambda b, c, pt: (b, 0, 0))]
                 + [page_spec(r) for r in range(npg)],
        out_specs=(pl.BlockSpec((None, None, s_q, npg * PAGE_SIZE), lambda b, c, pt: (c, b, 0, 0)),
                   pl.BlockSpec((None, s_q, PAGE_SIZE), lambda b, c, pt: (b, 0, 0))),
    )
    return pl.pallas_call(
        _idx_sample_kernel,
        grid_spec=grid_spec,
        out_shape=(jax.ShapeDtypeStruct((n_steps, db, s_q, npg * PAGE_SIZE), F32),
                   jax.ShapeDtypeStruct((db, s_q, PAGE_SIZE), F32)),
        compiler_params=_cparams(2),
        name="idx_sample",
    )(page_table, qi_stack, wcol, kinew, *([cache_kidx] * npg))


def _sel_sample_kernel(past_ref, new_ref, thr_ref, j_ref, *, topk, n_pos_bits):
    n_c, g, s_q, w = past_ref.shape
    rows = g * s_q
    l_past = n_c * w
    lane_pos = lax.broadcasted_iota(jnp.int32, (rows, LANES), 1)

    def lane_tile_sum(v):
        return functools.reduce(lambda a, b: a + b, [v[:, t * LANES:(t + 1) * LANES] for t in range(v.shape[1] // LANES)])

    def count(pred):
        def body(cc, acc):
            x = past_ref[cc].reshape(rows, w)
            pos = cc * w + lax.broadcasted_iota(jnp.int32, (rows, w), 1)
            return acc + lane_tile_sum(jnp.where(pred(x, pos), 1.0, 0.0))
        acc = lax.fori_loop(0, n_c, body, jnp.zeros((rows, LANES), F32))
        acc = acc + jnp.where(pred(new_ref[...].reshape(rows, PAGE_SIZE), l_past + lane_pos), 1.0, 0.0)
        return jnp.sum(acc, axis=1, keepdims=True)

    def bit_body(t, res):
        trial = res | jnp.left_shift(jnp.int32(1), 31 - t)
        cand = _ukey_to_f32(trial)
        return jnp.where(count(lambda x, pos: x >= cand) >= topk, trial, res)

    res = lax.fori_loop(0, 32, bit_body, jnp.zeros((rows, 1), jnp.int32))
    thr = _ukey_to_f32(res)
    n_ties = topk - count(lambda x, pos: x > thr)

    def jbit_body(t, resj):
        trial = resj | jnp.left_shift(jnp.int32(1), n_pos_bits - 1 - t)
        below = count(lambda x, pos: (x == thr) & (pos < trial))
        return jnp.where(below < n_ties, trial, resj)

    jmax = lax.fori_loop(0, n_pos_bits, jbit_body, jnp.zeros((rows, 1), jnp.int32))
    thr_ref[...] = jnp.broadcast_to(thr, (rows, LANES))
    j_ref[...] = jnp.broadcast_to(jmax, (rows, LANES))


def _sel_sample(sc_past, sc_new, *, topk):
    n_c, db, s_q, w = sc_past.shape
    g = math.gcd(SEL_SEQS, db)
    kern = functools.partial(_sel_sample_kernel, topk=topk,
                             n_pos_bits=max(1, math.ceil(math.log2(n_c * w + PAGE_SIZE))))
    return pl.pallas_call(
        kern,
        grid=(db // g,),
        in_specs=[pl.BlockSpec((n_c, g, s_q, w), lambda i: (0, i, 0, 0)),
                  pl.BlockSpec((g, s_q, PAGE_SIZE), lambda i: (i, 0, 0))],
        out_specs=(pl.BlockSpec((g * s_q, LANES), lambda i: (i, 0)),
                   pl.BlockSpec((g * s_q, LANES), lambda i: (i, 0))),
        out_shape=(jax.ShapeDtypeStruct((db * s_q, LANES), F32),
                   jax.ShapeDtypeStruct((db * s_q, LANES), jnp.int32)),
        compiler_params=_cparams(1),
        name="sel_sample",
    )(sc_past, sc_new)


def _att_sample_kernel(pt_ref, qbd_ref, sc_past_ref, sc_new_ref, thr_ref, j_ref, knew_ref, vnew_ref, *rest,
                       l_past):
    npg = PAGES_PER_STEP
    kpages = rest[:npg]
    vpages = rest[npg:2 * npg]
    att_ref, acc_ref, m_ref, l_ref = rest[2 * npg:]
    c = pl.program_id(1)
    qbd = qbd_ref[...]
    thr = thr_ref[...]
    jmax = j_ref[...]

    @pl.when(c == 0)
    def _():
        acc_ref[...] = jnp.zeros(acc_ref.shape, F32)
        m_ref[...] = jnp.full(m_ref.shape, NEG_INF, F32)
        l_ref[...] = jnp.zeros(l_ref.shape, F32)

    def masked_scores(keys_t_bf16, sc, pos0):
        pos = pos0 + lax.broadcasted_iota(jnp.int32, sc.shape, 1)
        sel = (sc > thr) | ((sc == thr) & (pos <= jmax))
        bias = jnp.where(sel, 0.0, NEG_INF)
        s = jnp.dot(qbd, keys_t_bf16, preferred_element_type=F32)
        return s + jnp.concatenate([bias] * N_HEADS, axis=0)

    def accumulate(s_list, vt_list):
        m_old = m_ref[...]
        m_new = m_old
        for s in s_list:
            m_new = jnp.maximum(m_new, jnp.max(s, axis=1, keepdims=True))
        m_safe = jnp.maximum(m_new, -1e30)
        alpha = jnp.exp2(m_old - m_safe)
        acc = acc_ref[...] * alpha
        l = l_ref[...] * alpha
        for s, vt in zip(s_list, vt_list):
            p = jnp.exp2(s - m_safe)
            l = l + jnp.sum(p, axis=1, keepdims=True)
            acc = acc + lax.dot_general(p.astype(BF16), vt, NT_DIMS, preferred_element_type=F32)
        acc_ref[...] = acc
        l_ref[...] = l
        m_ref[...] = m_new

    s_list, v_list = [], []
    for p in range(npg):
        sc = sc_past_ref[:, p * PAGE_SIZE:(p + 1) * PAGE_SIZE]
        pos0 = (c * npg + p) * PAGE_SIZE
        s_list.append(masked_scores(kpages[p][...].astype(BF16), sc, pos0))
        v_list.append(vpages[p][...].astype(BF16))
    accumulate(s_list, v_list)

    @pl.when(c == pl.num_programs(1) - 1)
    def _():
        accumulate([masked_scores(knew_ref[...], sc_new_ref[...], l_past)], [vnew_ref[...]])
        att_ref[...] = acc_ref[...] / l_ref[...]


def _att_sample(page_table, qbd, sc_past, sc_new, thr, jmax, knew, vnew, cache_k, cache_v, *, s_q):
    db, n_pages = page_table.shape
    npg = PAGES_PER_STEP
    n_steps = n_pages // npg
    rows = N_HEADS * s_q
    page_spec = lambda r: pl.BlockSpec((None, KV_WIDTH, PAGE_SIZE), lambda b, c, pt: (pt[b, c * npg + r], 0, 0))
    per_seq = lambda r, n: pl.BlockSpec((None, r, n), lambda b, c, pt: (b, 0, 0))
    grid_spec = pltpu.PrefetchScalarGridSpec(
        num_scalar_prefetch=1,
        grid=(db, n_steps),
        in_specs=[per_seq(rows, KV_WIDTH),
                  pl.BlockSpec((None, None, s_q, npg * PAGE_SIZE), lambda b, c, pt: (c, b, 0, 0)),
                  per_seq(s_q, PAGE_SIZE),
                  pl.BlockSpec((s_q, LANES), lambda b, c, pt: (b, 0)),
                  pl.BlockSpec((s_q, LANES), lambda b, c, pt: (b, 0)),
                  per_seq(KV_WIDTH, PAGE_SIZE), per_seq(KV_WIDTH, PAGE_SIZE)]
                 + [page_spec(r) for r in range(npg)] * 2,
        out_specs=per_seq(rows, KV_WIDTH),
        scratch_shapes=[pltpu.VMEM((rows, KV_WIDTH), F32), pltpu.VMEM((rows, 1), F32), pltpu.VMEM((rows, 1), F32)],
    )
    kern = functools.partial(_att_sample_kernel, l_past=n_pages * PAGE_SIZE)
    return pl.pallas_call(
        kern,
        grid_spec=grid_spec,
        out_shape=jax.ShapeDtypeStruct((db, rows, KV_WIDTH), F32),
        compiler_params=_cparams(2),
        name="att_sample",
    )(page_table, qbd, sc_past, sc_new, thr, jmax, knew, vnew, *([cache_k] * npg), *([cache_v] * npg))


def _gla_kernel(*refs, chunk, n_chunks, off, t_end, has_s0):
    if has_s0:
        gq_ref, gk_ref, gv_ref, og_ref, la_ref, gn_ref, s0_ref, o_ref, sfin_ref, st_ref = refs
    else:
        gq_ref, gk_ref, gv_ref, og_ref, la_ref, gn_ref, o_ref, sfin_ref, st_ref = refs
    j = pl.program_id(1)
    tb = chunk * n_chunks

    @pl.when(j == 0)
    def _():
        for hh in range(GLA_HEADS):
            st_ref[hh] = s0_ref[hh].T if has_s0 else jnp.zeros((GLA_DV, GLA_DK), F32)

    r_i = lax.broadcasted_iota(jnp.int32, (chunk, chunk), 0)
    c_i = lax.broadcasted_iota(jnp.int32, (chunk, chunk), 1)
    causal = r_i >= c_i
    tril = jnp.where(causal, 1.0, 0.0)
    gn = gn_ref[...]

    def chunk_body(c, carry):
        r0 = pl.multiple_of(c * chunk, chunk)
        rows = pl.ds(r0, chunk)
        pos = j * tb + r0 + lax.broadcasted_iota(jnp.int32, (chunk, GLA_KW), 0)
        valid = (pos >= off) & (pos < t_end)
        la = jnp.where(valid, la_ref[rows, :], 0.0)
        k = jnp.where(valid, gk_ref[rows, :], 0.0)
        q = gq_ref[rows, :] * GLA_DK ** -0.5
        v = gv_ref[rows, :]
        b = jnp.dot(tril, la, preferred_element_type=F32, precision=lax.Precision.HIGHEST)
        b_last = b[chunk - 1:chunk, :]
        qd = (q * jnp.exp(b)).astype(BF16)
        kd = (k * jnp.exp(-b)).astype(BF16)
        ke = (k * jnp.exp(b_last - b)).astype(BF16)
        decay = jnp.exp(b_last)
        outs = []
        for hh in range(GLA_HEADS):
            ks = slice(hh * GLA_DK, (hh + 1) * GLA_DK)
            vh = v[:, hh * GLA_DV:(hh + 1) * GLA_DV]
            a = lax.dot_general(qd[:, ks], kd[:, ks], NT_DIMS, preferred_element_type=F32)
            a = jnp.where(causal, a, 0.0).astype(BF16)
            st = st_ref[hh]
            o = (jnp.dot(a, vh, preferred_element_type=F32)
                 + lax.dot_general(qd[:, ks], st.astype(BF16), NT_DIMS, preferred_element_type=F32))
            u_t = lax.dot_general(vh, ke[:, ks], TN_DIMS, preferred_element_type=F32)
            st_ref[hh] = decay[:, ks] * st + u_t
            o = o * lax.rsqrt(jnp.mean(o * o, axis=-1, keepdims=True) + LN_EPS) * gn
            outs.append(o)
        og = og_ref[rows, :]
        o_ref[rows, :] = (jnp.concatenate(outs, axis=1) * (og * jax.nn.sigmoid(og))).astype(BF16)
        return carry

    lax.fori_loop(0, n_chunks, chunk_body, 0)

    @pl.when(j == pl.num_programs(1) - 1)
    def _():
        for hh in range(GLA_HEADS):
            sfin_ref[hh] = st_ref[hh].T


def _gla(gq, gk, gv, og, la, gnorm, s0, *, batch, t_pad, tb, chunk, off, t_end):
    n_steps = t_pad // tb
    tok = lambda n: pl.BlockSpec((tb, n), lambda b, j: (b * n_steps + j, 0))
    state = pl.BlockSpec((None, GLA_HEADS, GLA_DK, GLA_DV), lambda b, j: (b, 0, 0, 0))
    has_s0 = s0 is not None
    kern = functools.partial(_gla_kernel, chunk=chunk, n_chunks=tb // chunk, off=off, t_end=t_end, has_s0=has_s0)
    in_specs = [tok(GLA_KW), tok(GLA_KW), tok(GLA_VW), tok(GLA_VW), tok(GLA_KW), _full_spec(gnorm.shape)]
    args = [gq, gk, gv, og, la, gnorm]
    if has_s0:
        in_specs.append(state)
        args.append(s0)
    return pl.pallas_call(
        kern,
        grid=(batch, n_steps),
        in_specs=in_specs,
        out_specs=(tok(GLA_VW), state),
        out_shape=(jax.ShapeDtypeStruct((batch * t_pad, GLA_VW), BF16),
                   jax.ShapeDtypeStruct((batch, GLA_HEADS, GLA_DK, GLA_DV), F32)),
        scratch_shapes=[pltpu.VMEM((GLA_HEADS, GLA_DV, GLA_DK), F32)],
        compiler_params=_cparams(2),
        name="gla",
    )(*args)


def _merge_kernel(x_ref, att_ref, gla_ref, gates_ref, lng_ref, lnb_ref, wa_ref, wg_ref, wo_ref,
                  l1g_ref, l1b_ref, h1_ref, *, alpha):
    h = _layer_norm(x_ref[...], lng_ref[...], lnb_ref[...])
    pa = jnp.dot(att_ref[...], wa_ref[...], preferred_element_type=F32)
    pg = jnp.dot(gla_ref[...], wg_ref[...], preferred_element_type=F32)
    gates = gates_ref[...].astype(F32)
    merged = gates[:, :D_MODEL] * pa + gates[:, D_MODEL:] * pg
    mix = jnp.dot(merged.astype(BF16), wo_ref[...], preferred_element_type=F32)
    h1_ref[...] = _layer_norm(alpha * h + mix, l1g_ref[...], l1b_ref[...])


def _merge(x, att, gla, gates, ln_g, ln_b, wa, wg, wo, l1g, l1b, *, alpha):
    n_tok = x.shape[0]
    tm = _row_tile(n_tok, TM_TOKEN)
    row = lambda n: pl.BlockSpec((tm, n), lambda i: (i, 0))
    consts = (ln_g, ln_b, wa, wg, wo, l1g, l1b)
    return pl.pallas_call(
        functools.partial(_merge_kernel, alpha=alpha),
        grid=(n_tok // tm,),
        in_specs=[row(D_MODEL), row(ATT_WIDTH), row(GLA_VW), row(N_BRANCHES * D_MODEL)]
                 + [_full_spec(a.shape) for a in consts],
        out_specs=row(D_MODEL),
        out_shape=jax.ShapeDtypeStruct((n_tok, D_MODEL), F32),
        compiler_params=_cparams(1),
        name="merge",
    )(x, att, gla, gates, *consts)


def _ffn_kernel(h_ref, wu_ref, wd_ref, g_ref, b_ref, y_ref, *, alpha, n_split):
    h = h_ref[...]
    hb = h.astype(BF16)
    w = D_FF // n_split
    ff = jnp.zeros(h.shape, F32)
    for s in range(n_split):
        u = jnp.dot(hb, wu_ref[:, s * w:(s + 1) * w], preferred_element_type=F32)
        u = jnp.square(jnp.maximum(u, 0.0)).astype(BF16)
        ff = ff + jnp.dot(u, wd_ref[s * w:(s + 1) * w, :], preferred_element_type=F32)
    y_ref[...] = _layer_norm(alpha * h + ff, g_ref[...], b_ref[...])


def _ffn(h1, wu, wd, g, b, *, alpha):
    n_tok = h1.shape[0]
    tm = _row_tile(n_tok, TM_TOKEN)
    row = pl.BlockSpec((tm, D_MODEL), lambda i: (i, 0))
    return pl.pallas_call(
        functools.partial(_ffn_kernel, alpha=alpha, n_split=4),
        grid=(n_tok // tm,),
        in_specs=[row] + [_full_spec(a.shape) for a in (wu, wd, g, b)],
        out_specs=row,
        out_shape=jax.ShapeDtypeStruct((n_tok, D_MODEL), F32),
        compiler_params=_cparams(1),
        name="ffn",
    )(h1, wu, wd, g, b)


def _pack_weights(w_in, w_gla_a2, b_gla_a):
    points = []
    acc = 0
    for s in IN_SIZES[:-1]:
        acc += s
        points.append(acc)
    wq, wk, wv, wqi, wki, wwi, wgq, wgk, wgv, wog, wa1, wgt = jnp.split(w_in, points, axis=-1)
    wq = wq * (math.log2(math.e) * HEAD_DIM ** -0.5)
    wqi = wqi * IDX_DIM ** -0.5
    pad_cols = lambda a, n: jnp.pad(a, ((0, 0), (0, n - a.shape[1])))
    wv_heads = wv.T.reshape(N_KV_HEADS, HEAD_DIM, D_MODEL)
    wv_aug = jnp.pad(wv_heads, ((0, 0), (0, LANES - HEAD_DIM), (0, 0))).reshape(N_KV_HEADS * LANES, D_MODEL)
    w = {
        "qT": wq.T, "qiT": wqi.T, "vT": wv_aug,
        "wiT": jnp.pad(wwi.T, ((0, 2 * SUBLANES - IDX_HEADS), (0, 0))),
        "row_p": pad_cols(jnp.concatenate([wk, wv, wki], axis=1), 2 * KV_WIDTH + LANES),
        "row_s": pad_cols(jnp.concatenate([wq, wqi, wk, wv, wki, wwi], axis=1),
                          ATT_WIDTH + IDX_HEADS * IDX_DIM + 2 * KV_WIDTH + LANES),
        "gla": jnp.concatenate([wgq, wgk, wgv, wog], axis=1),
        "a1": pad_cols(wa1, LANES),
        "a2": jnp.pad(w_gla_a2, ((0, LANES - GLA_GATE_RANK), (0, 0))),
        "gt": wgt,
    }
    w = {name: a.astype(BF16) for name, a in w.items()}
    w["ba"] = b_gla_a.astype(F32)[None]
    return w


def _round_up(x, m):
    return -(-x // m) * m


def _row_tile(n, pref, unit=2 * SUBLANES):
    best = unit
    for t in range(unit, min(n, pref) + 1, unit):
        if n % t == 0:
            best = t
    assert n % best == 0
    return best


def kernel(x_prompt, x_sample, cache_k, cache_v, cache_kidx, state_gla, page_table, meta_tokens, ln_in_g, ln_in_b, w_in, w_gla_a2, b_gla_a, gla_norm_g, w_proj_attn, w_proj_gla, w_out, ln1_g, ln1_b, w_ff_up, w_ff_down, ln2_g, ln2_b):
    depth = w_in.shape[0]
    assert depth == 1, "single-layer step only"
    B, S_p, D = x_prompt.shape
    DB, S_s, _ = x_sample.shape
    n_pages = page_table.shape[1]
    past = n_pages * PAGE_SIZE
    assert D == D_MODEL and S_p % GLA_CHUNK == 0 and S_s <= SUBLANES and n_pages % PAGES_PER_STEP == 0
    topk_prompt = min(TOPK_MAX, S_p // 4)
    topk_sample = min(TOPK_MAX, (past + S_s) // 4)
    alpha = (2 * depth) ** 0.25

    row = lambda a: a.astype(F32).reshape(1, -1)
    ln_g, ln_b = row(ln_in_g), row(ln_in_b)
    w = _pack_weights(w_in[0], w_gla_a2[0], b_gla_a[0])
    wa, wg, wo = (a[0].astype(BF16) for a in (w_proj_attn, w_proj_gla, w_out))
    wu, wd = w_ff_up[0].astype(BF16), w_ff_down[0].astype(BF16)
    gnorm = row(gla_norm_g[0])
    l1g, l1b, l2g, l2b = row(ln1_g[0]), row(ln1_b[0]), row(ln2_g[0]), row(ln2_b[0])

    T = S_p + N_META
    off = (-N_META) % GLA_CHUNK
    t_pad = _round_up(off + T, math.lcm(KEY_CHUNK, GLA_CHUNK))
    gla_tb = _row_tile(t_pad, GLA_TOKENS_PER_STEP, GLA_CHUNK)
    n_tok = B * t_pad
    meta = jnp.broadcast_to(meta_tokens.astype(x_prompt.dtype)[None], (B, N_META, D))
    xp = jnp.concatenate([jnp.zeros((B, off, D), x_prompt.dtype), meta, x_prompt,
                          jnp.zeros((B, t_pad - off - T, D), x_prompt.dtype)], axis=1).reshape(n_tok, D)

    qT, qiT, vT3, wiT, k32, v32, ki32, kb, kib = _proj_attn_prompt(xp, ln_g, ln_b, w, n_tok)
    att_p = _dsa_prompt(qT, qiT, wiT,
                        kb.reshape(n_tok // KEY_CHUNK, KEY_CHUNK, KV_WIDTH),
                        kib.reshape(n_tok // KEY_CHUNK, KEY_CHUNK, IDX_DIM), vT3,
                        batch=B, t_pad=t_pad, topk=topk_prompt, off=off)
    gq, gk, gv, og, la, gates_p = _proj_gla(xp, ln_g, ln_b, w, n_tok)
    gla_p, state_p = _gla(gq, gk, gv, og, la, gnorm, None, batch=B, t_pad=t_pad, tb=gla_tb,
                          chunk=GLA_CHUNK, off=off, t_end=off + T)
    h1_p = _merge(xp, att_p, gla_p, gates_p, ln_g, ln_b, wa, wg, wo, l1g, l1b, alpha=alpha)
    y_p = _ffn(h1_p, wu, wd, l2g, l2b, alpha=alpha)

    seq = lambda a: a.reshape((B, t_pad) + a.shape[1:])
    y_prompt = seq(y_p)[:, off + N_META:off + T]
    k_prompt = seq(k32)[:, off:off + T].reshape(1, B, T, N_KV_HEADS, HEAD_DIM)
    v_prompt = seq(v32)[:, off:off + T].reshape(1, B, T, N_KV_HEADS, HEAD_DIM)
    kidx_prompt = seq(ki32)[:, off:off + T][None]
    gla_state_prompt = state_p[None]

    R = SAMPLE_ROWS
    n_tok_s = DB * R
    xs = jnp.pad(x_sample, ((0, 0), (0, R - S_s), (0, 0))).reshape(n_tok_s, D)
    q_s, qi_s, k_s, v_s, kiwi_s = _proj_attn_sample(xs, ln_g, ln_b, w, n_tok_s)
    sseq = lambda a: a.reshape((DB, R) + a.shape[1:])[:, :S_s]
    k_new, v_new = sseq(k_s), sseq(v_s)
    ki_new = sseq(kiwi_s)[..., :IDX_DIM]
    wi_new = sseq(kiwi_s)[..., IDX_DIM:IDX_DIM + IDX_HEADS]
    qi_stack = sseq(qi_s).reshape(DB, S_s, IDX_HEADS, IDX_DIM).transpose(0, 2, 1, 3).reshape(DB, IDX_HEADS * S_s, IDX_DIM)
    wcol = jnp.broadcast_to(wi_new.transpose(0, 2, 1).reshape(DB, IDX_HEADS * S_s, 1), (DB, IDX_HEADS * S_s, LANES))
    q_heads = sseq(q_s).reshape(DB, S_s, N_HEADS, HEAD_DIM).transpose(0, 2, 1, 3)
    kv_of_head = (jnp.arange(N_HEADS) // GROUP)[:, None] == jnp.arange(N_KV_HEADS)[None, :]
    qbd = jnp.where(kv_of_head[None, :, None, :, None], q_heads[:, :, :, None, :], jnp.zeros((), BF16))
    qbd = qbd.reshape(DB, N_HEADS * S_s, KV_WIDTH)
    pad_page = lambda a: jnp.pad(a, ((0, 0), (0, PAGE_SIZE - S_s), (0, 0))).astype(BF16).transpose(0, 2, 1)
    kinew_pg, knew_pg, vnew_pg = pad_page(ki_new), pad_page(k_new), pad_page(v_new)
    ck = cache_k[0].transpose(0, 2, 3, 1).reshape(-1, KV_WIDTH, PAGE_SIZE)
    cv = cache_v[0].transpose(0, 2, 3, 1).reshape(-1, KV_WIDTH, PAGE_SIZE)
    cki = cache_kidx[0].transpose(0, 2, 1)
    sc_past, sc_new = _idx_sample(page_table, qi_stack, wcol, kinew_pg, cki, s_q=S_s)
    thr, jmax = _sel_sample(sc_past, sc_new, topk=topk_sample)
    o_s = _att_sample(page_table, qbd, sc_past, sc_new, thr, jmax, knew_pg, vnew_pg, ck, cv, s_q=S_s)
    o_s = o_s.reshape(DB, N_KV_HEADS, GROUP, S_s, N_KV_HEADS, HEAD_DIM)
    att_s = jnp.stack([o_s[:, n, :, :, n, :] for n in range(N_KV_HEADS)], axis=1)
    att_s = att_s.transpose(0, 3, 1, 2, 4).reshape(DB, S_s, ATT_WIDTH).astype(BF16)
    att_s = jnp.pad(att_s, ((0, 0), (0, R - S_s), (0, 0))).reshape(n_tok_s, ATT_WIDTH)

    gq, gk, gv, og, la, gates_s = _proj_gla(xs, ln_g, ln_b, w, n_tok_s)
    gla_s, state_s = _gla(gq, gk, gv, og, la, gnorm, state_gla[0], batch=DB, t_pad=R, tb=R,
                          chunk=R, off=0, t_end=S_s)
    h1_s = _merge(xs, att_s, gla_s, gates_s, ln_g, ln_b, wa, wg, wo, l1g, l1b, alpha=alpha)
    y_s = _ffn(h1_s, wu, wd, l2g, l2b, alpha=alpha)

    y_sample = y_s.reshape(DB, R, D)[:, :S_s]
    k_sample = k_new.reshape(1, DB, S_s, N_KV_HEADS, HEAD_DIM)
    v_sample = v_new.reshape(1, DB, S_s, N_KV_HEADS, HEAD_DIM)
    kidx_sample = ki_new[None]
    gla_state_sample = state_s[None]
    return (y_prompt, y_sample, k_prompt, v_prompt, kidx_prompt, gla_state_prompt,
            k_sample, v_sample, kidx_sample, gla_state_sample)
```

```python
import functools
import math

import jax
import jax.numpy as jnp
from jax import lax
from jax.experimental import pallas as pl
from jax.experimental.pallas import tpu as pltpu

D_MODEL = 1024
PAGE_SIZE = 128
N_META = 16
N_HEADS = 16
HEAD_DIM = 64
N_KV_HEADS = 4
GROUP = N_HEADS // N_KV_HEADS
ATT_WIDTH = N_HEADS * HEAD_DIM
KV_WIDTH = N_KV_HEADS * HEAD_DIM
IDX_HEADS = 8
IDX_DIM = 64
TOPK_MAX = 256
GLA_HEADS = 4
GLA_DK = D_MODEL // 2 // GLA_HEADS
GLA_DV = D_MODEL // GLA_HEADS
GLA_KW = GLA_HEADS * GLA_DK
GLA_VW = GLA_HEADS * GLA_DV
GLA_GATE_RANK = 16
GLA_TAU = 16.0
GLA_CHUNK = 64
N_BRANCHES = 2
D_FF = 4 * D_MODEL
LN_EPS = 1e-5
IN_SIZES = (ATT_WIDTH, KV_WIDTH, KV_WIDTH, IDX_HEADS * IDX_DIM, IDX_DIM, IDX_HEADS,
            GLA_KW, GLA_KW, GLA_VW, GLA_VW, GLA_GATE_RANK, N_BRANCHES * D_MODEL)

LANES = 128
SUBLANES = 8
VMEM_LIMIT_BYTES = 56 * 1024 * 1024
COUNT_LANES = 4
Q_BLOCK = LANES
KEY_CHUNK = 3 * LANES
SAMPLE_ROWS = 16
PAGES_PER_STEP = 16
IDX_PAGES_PER_DOT = 4
SEL_SEQS = 16
TM_PROJ_GLA = 256
TM_TOKEN = 512
GLA_TOKENS_PER_STEP = 11 * GLA_CHUNK

F32 = jnp.float32
BF16 = jnp.bfloat16
NEG_INF = float("-inf")
INT_MIN = -2 ** 31
NT_DIMS = (((1,), (1,)), ((), ()))
TN_DIMS = (((0,), (0,)), ((), ()))


def _cparams(n_grid):
    return pltpu.CompilerParams(dimension_semantics=("arbitrary",) * n_grid,
                                vmem_limit_bytes=VMEM_LIMIT_BYTES)


def _full_spec(shape):
    nd = len(shape)
    return pl.BlockSpec(shape, lambda *_: (0,) * nd)


def _layer_norm(x, g, b):
    mu = jnp.mean(x, axis=-1, keepdims=True)
    xc = x - mu
    var = jnp.mean(xc * xc, axis=-1, keepdims=True)
    return xc * lax.rsqrt(var + LN_EPS) * g + b


def _ukey_to_f32(u):
    bits = jnp.where(u < 0, u & jnp.int32(0x7FFFFFFF), ~u)
    return lax.bitcast_convert_type(bits, F32)


def _proj_attn_prompt_kernel(x_ref, g_ref, b_ref, wq_ref, wqi_ref, wv_ref, wwi_ref, wrow_ref,
                             qT_ref, qiT_ref, vT_ref, wiT_ref, k_ref, v_ref, ki_ref, kb_ref, kib_ref):
    hb = _layer_norm(x_ref[...], g_ref[...], b_ref[...]).astype(BF16)
    qT_ref[...] = lax.dot_general(wq_ref[...], hb, NT_DIMS, preferred_element_type=F32).astype(BF16)
    qiT_ref[...] = lax.dot_general(wqi_ref[...], hb, NT_DIMS, preferred_element_type=F32).astype(BF16)
    vt = lax.dot_general(wv_ref[...], hb, NT_DIMS, preferred_element_type=F32)
    row = lax.broadcasted_iota(jnp.int32, vt.shape, 0)
    vt = jnp.where(row % LANES == HEAD_DIM, 1.0, vt)
    vT_ref[0] = vt.astype(BF16)
    wi = lax.dot_general(wwi_ref[...], hb, NT_DIMS, preferred_element_type=F32)
    wiT_ref[...] = wi[:IDX_HEADS] * IDX_HEADS ** -0.5
    y = jnp.dot(hb, wrow_ref[...], preferred_element_type=F32)
    k = y[:, :KV_WIDTH]
    ki = y[:, 2 * KV_WIDTH:2 * KV_WIDTH + IDX_DIM]
    k_ref[...] = k
    v_ref[...] = y[:, KV_WIDTH:2 * KV_WIDTH]
    ki_ref[...] = ki
    kb_ref[...] = k.astype(BF16)
    kib_ref[...] = ki.astype(BF16)


def _proj_attn_prompt(x, ln_g, ln_b, w, n_tok):
    tm = KEY_CHUNK
    n_steps = n_tok // tm
    row = lambda n: pl.BlockSpec((tm, n), lambda i: (i, 0))
    col = lambda n: pl.BlockSpec((n, tm), lambda i: (0, i))
    out_shape = (
        jax.ShapeDtypeStruct((ATT_WIDTH, n_tok), BF16),
        jax.ShapeDtypeStruct((IDX_HEADS * IDX_DIM, n_tok), BF16),
        jax.ShapeDtypeStruct((n_steps, N_KV_HEADS * LANES, tm), BF16),
        jax.ShapeDtypeStruct((IDX_HEADS, n_tok), F32),
        jax.ShapeDtypeStruct((n_tok, KV_WIDTH), F32),
        jax.ShapeDtypeStruct((n_tok, KV_WIDTH), F32),
        jax.ShapeDtypeStruct((n_tok, IDX_DIM), F32),
        jax.ShapeDtypeStruct((n_tok, KV_WIDTH), BF16),
        jax.ShapeDtypeStruct((n_tok, IDX_DIM), BF16),
    )
    out_specs = (col(ATT_WIDTH), col(IDX_HEADS * IDX_DIM),
                 pl.BlockSpec((1, N_KV_HEADS * LANES, tm), lambda i: (i, 0, 0)),
                 col(IDX_HEADS), row(KV_WIDTH), row(KV_WIDTH), row(IDX_DIM), row(KV_WIDTH), row(IDX_DIM))
    ws = (w["qT"], w["qiT"], w["vT"], w["wiT"], w["row_p"])
    return pl.pallas_call(
        _proj_attn_prompt_kernel,
        grid=(n_steps,),
        in_specs=[row(D_MODEL), _full_spec(ln_g.shape), _full_spec(ln_b.shape)] + [_full_spec(a.shape) for a in ws],
        out_specs=out_specs,
        out_shape=out_shape,
        compiler_params=_cparams(1),
        name="proj_attn_prompt",
    )(x, ln_g, ln_b, *ws)


def _proj_attn_sample_kernel(x_ref, g_ref, b_ref, w_ref, scale_ref, q_ref, qi_ref, k_ref, v_ref, kiwi_ref):
    hb = _layer_norm(x_ref[...], g_ref[...], b_ref[...]).astype(BF16)
    y = jnp.dot(hb, w_ref[...], preferred_element_type=F32)
    o = 0
    q_ref[...] = y[:, o:o + ATT_WIDTH].astype(BF16)
    o += ATT_WIDTH
    qi_ref[...] = y[:, o:o + IDX_HEADS * IDX_DIM].astype(BF16)
    o += IDX_HEADS * IDX_DIM
    k_ref[...] = y[:, o:o + KV_WIDTH]
    o += KV_WIDTH
    v_ref[...] = y[:, o:o + KV_WIDTH]
    o += KV_WIDTH
    kiwi_ref[...] = y[:, o:o + LANES] * scale_ref[...]


def _proj_attn_sample(x, ln_g, ln_b, w, n_tok):
    tm = _row_tile(n_tok, TM_TOKEN)
    row = lambda n: pl.BlockSpec((tm, n), lambda i: (i, 0))
    lane = lax.iota(jnp.int32, LANES)
    scale = jnp.where((lane >= IDX_DIM) & (lane < IDX_DIM + IDX_HEADS), IDX_HEADS ** -0.5, 1.0).astype(F32)[None]
    out_shape = (
        jax.ShapeDtypeStruct((n_tok, ATT_WIDTH), BF16),
        jax.ShapeDtypeStruct((n_tok, IDX_HEADS * IDX_DIM), BF16),
        jax.ShapeDtypeStruct((n_tok, KV_WIDTH), F32),
        jax.ShapeDtypeStruct((n_tok, KV_WIDTH), F32),
        jax.ShapeDtypeStruct((n_tok, LANES), F32),
    )
    return pl.pallas_call(
        _proj_attn_sample_kernel,
        grid=(n_tok // tm,),
        in_specs=[row(D_MODEL), _full_spec(ln_g.shape), _full_spec(ln_b.shape),
                  _full_spec(w["row_s"].shape), _full_spec(scale.shape)],
        out_specs=(row(ATT_WIDTH), row(IDX_HEADS * IDX_DIM), row(KV_WIDTH), row(KV_WIDTH), row(LANES)),
        out_shape=out_shape,
        compiler_params=_cparams(1),
        name="proj_attn_sample",
    )(x, ln_g, ln_b, w["row_s"], scale)


def _log_sigmoid(x):
    return jnp.minimum(x, 0.0) - jnp.log1p(jnp.exp(-jnp.abs(x)))


def _proj_gla_kernel(x_ref, g_ref, b_ref, wg_ref, wa1_ref, wa2_ref, ba_ref, wgt_ref,
                     gq_ref, gk_ref, gv_ref, og_ref, la_ref, gates_ref):
    hb = _layer_norm(x_ref[...], g_ref[...], b_ref[...]).astype(BF16)
    y = jnp.dot(hb, wg_ref[...], preferred_element_type=F32)
    gq_ref[...] = y[:, :GLA_KW]
    gk_ref[...] = y[:, GLA_KW:2 * GLA_KW]
    gv_ref[...] = y[:, 2 * GLA_KW:2 * GLA_KW + GLA_VW].astype(BF16)
    og_ref[...] = y[:, 2 * GLA_KW + GLA_VW:]
    a1 = jnp.dot(hb, wa1_ref[...], preferred_element_type=F32).astype(BF16)
    z = jnp.dot(a1, wa2_ref[...], preferred_element_type=F32) + ba_ref[...]
    la_ref[...] = _log_sigmoid(z) / GLA_TAU
    gt = jnp.dot(hb, wgt_ref[...], preferred_element_type=F32)
    gates_ref[...] = jax.nn.sigmoid(gt).astype(BF16)


def _proj_gla(x, ln_g, ln_b, w, n_tok):
    tm = _row_tile(n_tok, TM_PROJ_GLA)
    row = lambda n: pl.BlockSpec((tm, n), lambda i: (i, 0))
    ws = (w["gla"], w["a1"], w["a2"], w["ba"], w["gt"])
    out_shape = (
        jax.ShapeDtypeStruct((n_tok, GLA_KW), F32),
        jax.ShapeDtypeStruct((n_tok, GLA_KW), F32),
        jax.ShapeDtypeStruct((n_tok, GLA_VW), BF16),
        jax.ShapeDtypeStruct((n_tok, GLA_VW), F32),
        jax.ShapeDtypeStruct((n_tok, GLA_KW), F32),
        jax.ShapeDtypeStruct((n_tok, N_BRANCHES * D_MODEL), BF16),
    )
    return pl.pallas_call(
        _proj_gla_kernel,
        grid=(n_tok // tm,),
        in_specs=[row(D_MODEL), _full_spec(ln_g.shape), _full_spec(ln_b.shape)] + [_full_spec(a.shape) for a in ws],
        out_specs=(row(GLA_KW), row(GLA_KW), row(GLA_VW), row(GLA_VW), row(GLA_KW), row(N_BRANCHES * D_MODEL)),
        out_shape=out_shape,
        compiler_params=_cparams(1),
        name="proj_gla",
    )(x, ln_g, ln_b, *ws)


def _dsa_prompt_kernel(qT_ref, qiT_ref, wiT_ref, kb_ref, kib_ref, vT_ref, att_ref,
                       sc_ref, qn_ref, acc_ref, j_ref, sa_ref, sb_ref, *, topk, off, n_pos_bits):
    kc = KEY_CHUNK
    i = pl.program_id(1)
    qpos0 = i * Q_BLOCK
    n_chunks = (qpos0 + Q_BLOCK + kc - 1) // kc
    qpos = qpos0 + lax.broadcasted_iota(jnp.int32, (1, Q_BLOCK), 1)

    def key_pos(c):
        return c * kc + lax.broadcasted_iota(jnp.int32, (kc, Q_BLOCK), 0)

    qn_ref[...] = jnp.zeros(qn_ref.shape, BF16)
    for h in range(N_HEADS):
        n = h // GROUP
        qn_ref[n * HEAD_DIM:(n + 1) * HEAD_DIM, h * Q_BLOCK:(h + 1) * Q_BLOCK] = qT_ref[h * HEAD_DIM:(h + 1) * HEAD_DIM, :]

    wi = wiT_ref[...]

    def idx_body(c, carry):
        kic = kib_ref[c]
        acc = jnp.zeros((kc, Q_BLOCK), F32)
        for hp in range(IDX_HEADS // 2):
            r0 = 2 * hp * IDX_DIM
            rhs = jnp.concatenate([qiT_ref[r0:r0 + IDX_DIM, :], qiT_ref[r0 + IDX_DIM:r0 + 2 * IDX_DIM, :]], axis=1)
            s = jnp.maximum(jnp.dot(kic, rhs, preferred_element_type=F32), 0.0)
            acc = acc + s[:, :Q_BLOCK] * wi[2 * hp:2 * hp + 1, :] + s[:, Q_BLOCK:] * wi[2 * hp + 1:2 * hp + 2, :]
        kp = key_pos(c)
        valid = (kp >= off) & (kp <= qpos)
        sc_ref[c] = jnp.where(valid, acc, NEG_INF)
        return carry

    lax.fori_loop(0, n_chunks, idx_body, 0)

    def count(pred):
        def body(c, accs):
            v = jnp.where(pred(c), 1.0, 0.0).reshape(COUNT_LANES, kc // (SUBLANES * COUNT_LANES), SUBLANES, Q_BLOCK)
            return tuple(a + jnp.sum(v[r], axis=0) for r, a in enumerate(accs))
        accs = lax.fori_loop(0, n_chunks, body, (jnp.zeros((SUBLANES, Q_BLOCK), F32),) * COUNT_LANES)
        return jnp.sum(functools.reduce(lambda a, b: a + b, accs), axis=0, keepdims=True)

    def bit_body(t, res):
        trial = res | jnp.left_shift(jnp.int32(1), 31 - t)
        cand = _ukey_to_f32(trial)
        cnt = count(lambda c: sc_ref[c] >= cand)
        return jnp.where(cnt >= topk, trial, res)

    res = lax.fori_loop(0, 32, bit_body, jnp.zeros((1, Q_BLOCK), jnp.int32))
    few = (qpos - off + 1) <= topk
    thr = jnp.where(few, NEG_INF, _ukey_to_f32(res))
    cnt_gt = count(lambda c: sc_ref[c] > thr)
    cnt_ge = count(lambda c: sc_ref[c] >= thr)
    n_ties = topk - cnt_gt
    tie_rows = jnp.where(few, 0.0, jnp.where(cnt_ge > topk, 1.0, 0.0))
    j_ref[...] = jnp.full(j_ref.shape, 2 ** 30, jnp.int32)

    @pl.when(jnp.max(tie_rows) > 0.0)
    def _():
        def jbit_body(t, resj):
            trial = resj | jnp.left_shift(jnp.int32(1), n_pos_bits - 1 - t)
            below = count(lambda c: (sc_ref[c] == thr) & (key_pos(c) < trial))
            return jnp.where(below < n_ties, trial, resj)
        j_ref[...] = lax.fori_loop(0, n_pos_bits, jbit_body, jnp.zeros((1, Q_BLOCK), jnp.int32))

    jmax = j_ref[...]

    def bias_body(c, carry):
        sc = sc_ref[c]
        kp = key_pos(c)
        valid = (kp >= off) & (kp <= qpos)
        sel = (sc > thr) | ((sc == thr) & (kp <= jmax))
        sc_ref[c] = jnp.where(valid & sel, 0.0, NEG_INF)
        return carry

    lax.fori_loop(0, n_chunks, bias_body, 0)

    acc_ref[...] = jnp.zeros(acc_ref.shape, F32)
    gw = GROUP * Q_BLOCK
    last = n_chunks - 1

    def score_stage(c, buf_ref, m_run):
        s = jnp.dot(kb_ref[c], qn_ref[...], preferred_element_type=F32) + jnp.concatenate([sc_ref[c]] * N_HEADS, axis=1)
        buf_ref[...] = s
        return jnp.maximum(m_run, jnp.max(s, axis=0, keepdims=True))

    def prob_stage(c, buf_ref, m_before, m_with, weight):
        m_safe = jnp.maximum(m_with, -1e30)
        alpha = jnp.exp2(m_before - m_safe)
        p = jnp.exp2(buf_ref[...] - m_safe).astype(BF16)
        vt = vT_ref[c]
        for n in range(N_KV_HEADS):
            cols = slice(n * gw, (n + 1) * gw)
            pv = jnp.dot(vt[n * LANES:(n + 1) * LANES, :], p[:, cols], preferred_element_type=F32)
            acc_ref[n] = acc_ref[n] * alpha[:, cols] + weight * pv

    def pair_body(t, carry):
        m_prev, m_cur = carry
        c0 = 2 * t
        c1 = jnp.minimum(c0 + 1, last)
        c2 = jnp.minimum(c0 + 2, last)
        m_1 = score_stage(c1, sb_ref, m_cur)
        prob_stage(c0, sa_ref, m_prev, m_cur, 1.0)
        m_2 = score_stage(c2, sa_ref, m_1)
        prob_stage(c1, sb_ref, m_cur, m_1, jnp.where(c0 + 1 <= last, 1.0, 0.0))
        return m_1, m_2

    m_none = jnp.full((1, N_HEADS * Q_BLOCK), NEG_INF, F32)
    lax.fori_loop(0, (n_chunks + 1) // 2, pair_body, (m_none, score_stage(0, sa_ref, m_none)))

    heads = []
    for n in range(N_KV_HEADS):
        a = acc_ref[n]
        denom = a[HEAD_DIM:HEAD_DIM + 1, :]
        o = a[:HEAD_DIM, :] / jnp.where(denom > 0.0, denom, 1.0)
        for g in range(GROUP):
            heads.append(o[:, g * Q_BLOCK:(g + 1) * Q_BLOCK])
    att_ref[...] = jnp.concatenate(heads, axis=0).T.astype(BF16)


def _dsa_prompt(qT, qiT, wiT, kb3, kib3, vT3, *, batch, t_pad, topk, off):
    kc = KEY_CHUNK
    nqb = t_pad // Q_BLOCK
    ncb = t_pad // kc
    n_tok = batch * t_pad
    qcol = lambda n: pl.BlockSpec((n, Q_BLOCK), lambda b, i: (0, b * nqb + i))
    kern = functools.partial(_dsa_prompt_kernel, topk=topk, off=off,
                             n_pos_bits=max(1, math.ceil(math.log2(t_pad))))
    return pl.pallas_call(
        kern,
        grid=(batch, nqb),
        in_specs=[qcol(ATT_WIDTH), qcol(IDX_HEADS * IDX_DIM), qcol(IDX_HEADS),
                  pl.BlockSpec((ncb, kc, KV_WIDTH), lambda b, i: (b, 0, 0)),
                  pl.BlockSpec((ncb, kc, IDX_DIM), lambda b, i: (b, 0, 0)),
                  pl.BlockSpec((ncb, N_KV_HEADS * LANES, kc), lambda b, i: (b, 0, 0))],
        out_specs=pl.BlockSpec((Q_BLOCK, ATT_WIDTH), lambda b, i: (b * nqb + i, 0)),
        out_shape=jax.ShapeDtypeStruct((n_tok, ATT_WIDTH), BF16),
        scratch_shapes=[pltpu.VMEM((ncb, kc, Q_BLOCK), F32),
                        pltpu.VMEM((KV_WIDTH, N_HEADS * Q_BLOCK), BF16),
                        pltpu.VMEM((N_KV_HEADS, LANES, GROUP * Q_BLOCK), F32),
                        pltpu.VMEM((1, Q_BLOCK), jnp.int32),
                        pltpu.VMEM((kc, N_HEADS * Q_BLOCK), F32),
                        pltpu.VMEM((kc, N_HEADS * Q_BLOCK), F32)],
        compiler_params=_cparams(2),
        name="dsa_prompt",
    )(qT, qiT, wiT, kb3, kib3, vT3)


def _idx_sample_kernel(pt_ref, qi_ref, wcol_ref, kinew_ref, cki_hbm, past_ref, new_ref, kibuf, sem):
    n_pages = kibuf.shape[1]
    s_q = new_ref.shape[0]
    b = pl.program_id(0)
    slot = b % 2

    def page_copies(seq, dst_slot):
        return [pltpu.make_async_copy(cki_hbm.at[pt_ref[seq, r]], kibuf.at[dst_slot, r], sem.at[dst_slot])
                for r in range(n_pages)]

    @pl.when(b == 0)
    def _():
        for cp in page_copies(0, 0):
            cp.start()

    @pl.when(b + 1 < pl.num_programs(0))
    def _():
        for cp in page_copies(b + 1, 1 - slot):
            cp.start()

    qi = qi_ref[...]

    def scores(keys_t_bf16):
        s = jnp.dot(qi, keys_t_bf16, preferred_element_type=F32)
        wcol = jnp.concatenate([wcol_ref[...]] * (s.shape[1] // LANES), axis=1)
        s = jnp.maximum(s, 0.0) * wcol
        return jnp.sum(s.reshape(IDX_HEADS, s_q, s.shape[1]), axis=0)

    s_new = scores(kinew_ref[...])
    qrow = lax.broadcasted_iota(jnp.int32, s_new.shape, 0)
    kcol = lax.broadcasted_iota(jnp.int32, s_new.shape, 1)
    new_ref[...] = jnp.where(kcol <= qrow, s_new, NEG_INF)

    for cp in page_copies(b, slot):
        cp.wait()
    npg = PAGES_PER_STEP
    for r0 in range(0, n_pages, IDX_PAGES_PER_DOT):
        keys = jnp.concatenate([kibuf[slot, r0 + j].astype(BF16) for j in range(IDX_PAGES_PER_DOT)], axis=1)
        c, p = divmod(r0, npg)
        past_ref[c, :, p * PAGE_SIZE:(p + IDX_PAGES_PER_DOT) * PAGE_SIZE] = scores(keys)


def _idx_sample(page_table, qi_stack, wcol, kinew, cache_kidx, *, s_q):
    db, n_pages = page_table.shape
    npg = PAGES_PER_STEP
    n_steps = n_pages // npg
    rows = IDX_HEADS * s_q
    per_seq = lambda r, n: pl.BlockSpec((None, r, n), lambda b, pt: (b, 0, 0))
    grid_spec = pltpu.PrefetchScalarGridSpec(
        num_scalar_prefetch=1,
        grid=(db,),
        in_specs=[per_seq(rows, IDX_DIM), per_seq(rows, LANES), per_seq(IDX_DIM, PAGE_SIZE),
                  pl.BlockSpec(memory_space=pl.ANY)],
        out_specs=(pl.BlockSpec((n_steps, None, s_q, npg * PAGE_SIZE), lambda b, pt: (0, b, 0, 0)),
                   per_seq(s_q, PAGE_SIZE)),
        scratch_shapes=[pltpu.VMEM((2, n_pages, IDX_DIM, PAGE_SIZE), F32), pltpu.SemaphoreType.DMA((2,))],
    )
    return pl.pallas_call(
        _idx_sample_kernel,
        grid_spec=grid_spec,
        out_shape=(jax.ShapeDtypeStruct((n_steps, db, s_q, npg * PAGE_SIZE), F32),
                   jax.ShapeDtypeStruct((db, s_q, PAGE_SIZE), F32)),
        compiler_params=_cparams(1),
        name="idx_sample",
    )(page_table, qi_stack, wcol, kinew, cache_kidx)


def _sel_sample_kernel(past_ref, new_ref, thr_ref, j_ref, *, topk, n_pos_bits):
    n_c, g, s_q, w = past_ref.shape
    rows = g * s_q
    l_past = n_c * w
    lane_pos = lax.broadcasted_iota(jnp.int32, (rows, LANES), 1)

    def lane_tile_sum(v):
        return functools.reduce(lambda a, b: a + b, [v[:, t * LANES:(t + 1) * LANES] for t in range(v.shape[1] // LANES)])

    def count(pred):
        def body(cc, acc):
            x = past_ref[cc].reshape(rows, w)
            pos = cc * w + lax.broadcasted_iota(jnp.int32, (rows, w), 1)
            return acc + lane_tile_sum(jnp.where(pred(x, pos), 1.0, 0.0))
        acc = lax.fori_loop(0, n_c, body, jnp.zeros((rows, LANES), F32))
        acc = acc + jnp.where(pred(new_ref[...].reshape(rows, PAGE_SIZE), l_past + lane_pos), 1.0, 0.0)
        return jnp.sum(acc, axis=1, keepdims=True)

    def bit_body(t, res):
        trial = res | jnp.left_shift(jnp.int32(1), 31 - t)
        cand = _ukey_to_f32(trial)
        return jnp.where(count(lambda x, pos: x >= cand) >= topk, trial, res)

    res = lax.fori_loop(0, 32, bit_body, jnp.zeros((rows, 1), jnp.int32))
    thr = _ukey_to_f32(res)
    n_ties = topk - count(lambda x, pos: x > thr)
    cnt_ge = count(lambda x, pos: x >= thr)
    thr_ref[...] = jnp.broadcast_to(thr, (rows, LANES))
    j_ref[...] = jnp.full((rows, LANES), 2 ** 30, jnp.int32)

    @pl.when(jnp.max(jnp.where(cnt_ge > topk, 1.0, 0.0)) > 0.0)
    def _():
        def jbit_body(t, resj):
            trial = resj | jnp.left_shift(jnp.int32(1), n_pos_bits - 1 - t)
            below = count(lambda x, pos: (x == thr) & (pos < trial))
            return jnp.where(below < n_ties, trial, resj)
        jmax = lax.fori_loop(0, n_pos_bits, jbit_body, jnp.zeros((rows, 1), jnp.int32))
        j_ref[...] = jnp.broadcast_to(jmax, (rows, LANES))


def _sel_sample(sc_past, sc_new, *, topk):
    n_c, db, s_q, w = sc_past.shape
    g = math.gcd(SEL_SEQS, db)
    kern = functools.partial(_sel_sample_kernel, topk=topk,
                             n_pos_bits=max(1, math.ceil(math.log2(n_c * w + PAGE_SIZE))))
    return pl.pallas_call(
        kern,
        grid=(db // g,),
        in_specs=[pl.BlockSpec((n_c, g, s_q, w), lambda i: (0, i, 0, 0)),
                  pl.BlockSpec((g, s_q, PAGE_SIZE), lambda i: (i, 0, 0))],
        out_specs=(pl.BlockSpec((g * s_q, LANES), lambda i: (i, 0)),
                   pl.BlockSpec((g * s_q, LANES), lambda i: (i, 0))),
        out_shape=(jax.ShapeDtypeStruct((db * s_q, LANES), F32),
                   jax.ShapeDtypeStruct((db * s_q, LANES), jnp.int32)),
        compiler_params=_cparams(1),
        name="sel_sample",
    )(sc_past, sc_new)


def _att_sample_kernel(pt_ref, qbd_ref, sc_past_ref, sc_new_ref, thr_ref, j_ref, knew_ref, vnew_ref, ck_hbm, cv_hbm,
                       att_ref, kbuf, vbuf, ksem, vsem, acc_ref, m_ref, l_ref):
    npg = kbuf.shape[1]
    n_steps = sc_past_ref.shape[0]
    b = pl.program_id(0)
    qbd = qbd_ref[...]
    thr = thr_ref[...]
    jmax = j_ref[...]

    def page_copies(seq, step, slot):
        cps = []
        for r in range(npg):
            page = pt_ref[seq, step * npg + r]
            cps.append(pltpu.make_async_copy(ck_hbm.at[page], kbuf.at[slot, r], ksem.at[slot]))
            cps.append(pltpu.make_async_copy(cv_hbm.at[page], vbuf.at[slot, r], vsem.at[slot]))
        return cps

    @pl.when(b == 0)
    def _():
        for cp in page_copies(0, 0, 0):
            cp.start()

    acc_ref[...] = jnp.zeros(acc_ref.shape, F32)
    m_ref[...] = jnp.full(m_ref.shape, NEG_INF, F32)
    l_ref[...] = jnp.zeros(l_ref.shape, F32)

    def masked_scores(keys_t_bf16, sc, pos0):
        pos = pos0 + lax.broadcasted_iota(jnp.int32, sc.shape, 1)
        sel = (sc > thr) | ((sc == thr) & (pos <= jmax))
        bias = jnp.where(sel, 0.0, NEG_INF)
        s = jnp.dot(qbd, keys_t_bf16, preferred_element_type=F32)
        return s + jnp.concatenate([bias] * N_HEADS, axis=0)

    def accumulate(s_list, vt_list):
        m_old = m_ref[...]
        m_new = jnp.maximum(m_old, jnp.max(functools.reduce(jnp.maximum, s_list), axis=1, keepdims=True))
        m_safe = jnp.maximum(m_new, -1e30)
        alpha = jnp.exp2(m_old - m_safe)
        acc = acc_ref[...] * alpha
        p_list = [jnp.exp2(s - m_safe) for s in s_list]
        for p, vt in zip(p_list, vt_list):
            acc = acc + lax.dot_general(p.astype(BF16), vt, NT_DIMS, preferred_element_type=F32)
        acc_ref[...] = acc
        l_ref[...] = l_ref[...] * alpha + jnp.sum(functools.reduce(lambda a, b: a + b, p_list), axis=1, keepdims=True)
        m_ref[...] = m_new

    for c in range(n_steps):
        slot = c % 2
        if c + 1 < n_steps:
            for cp in page_copies(b, c + 1, 1 - slot):
                cp.start()
        else:
            @pl.when(b + 1 < pl.num_programs(0))
            def _():
                for cp in page_copies(b + 1, 0, 1 - slot):
                    cp.start()
        for cp in page_copies(b, c, slot):
            cp.wait()
        s_list, v_list = [], []
        for p in range(npg):
            sc = sc_past_ref[c, :, p * PAGE_SIZE:(p + 1) * PAGE_SIZE]
            s_list.append(masked_scores(kbuf[slot, p].astype(BF16), sc, (c * npg + p) * PAGE_SIZE))
            v_list.append(vbuf[slot, p].astype(BF16))
        accumulate(s_list, v_list)

    accumulate([masked_scores(knew_ref[...], sc_new_ref[...], n_steps * npg * PAGE_SIZE)], [vnew_ref[...]])
    att_ref[...] = acc_ref[...] / l_ref[...]


def _att_sample(page_table, qbd, sc_past, sc_new, thr, jmax, knew, vnew, cache_k, cache_v, *, s_q):
    db, n_pages = page_table.shape
    npg = PAGES_PER_STEP
    n_steps = n_pages // npg
    rows = N_HEADS * s_q
    assert n_steps % 2 == 0
    per_seq = lambda r, n: pl.BlockSpec((None, r, n), lambda b, pt: (b, 0, 0))
    page_buf = pltpu.VMEM((2, npg, KV_WIDTH, PAGE_SIZE), F32)
    grid_spec = pltpu.PrefetchScalarGridSpec(
        num_scalar_prefetch=1,
        grid=(db,),
        in_specs=[per_seq(rows, KV_WIDTH),
                  pl.BlockSpec((n_steps, None, s_q, npg * PAGE_SIZE), lambda b, pt: (0, b, 0, 0)),
                  per_seq(s_q, PAGE_SIZE),
                  pl.BlockSpec((s_q, LANES), lambda b, pt: (b, 0)),
                  pl.BlockSpec((s_q, LANES), lambda b, pt: (b, 0)),
                  per_seq(KV_WIDTH, PAGE_SIZE), per_seq(KV_WIDTH, PAGE_SIZE),
                  pl.BlockSpec(memory_space=pl.ANY), pl.BlockSpec(memory_space=pl.ANY)],
        out_specs=per_seq(rows, KV_WIDTH),
        scratch_shapes=[page_buf, page_buf, pltpu.SemaphoreType.DMA((2,)), pltpu.SemaphoreType.DMA((2,)),
                        pltpu.VMEM((rows, KV_WIDTH), F32), pltpu.VMEM((rows, 1), F32), pltpu.VMEM((rows, 1), F32)],
    )
    return pl.pallas_call(
        _att_sample_kernel,
        grid_spec=grid_spec,
        out_shape=jax.ShapeDtypeStruct((db, rows, KV_WIDTH), F32),
        compiler_params=_cparams(1),
        name="att_sample",
    )(page_table, qbd, sc_past, sc_new, thr, jmax, knew, vnew, cache_k, cache_v)


def _gla_kernel(*refs, chunk, n_chunks, off, t_end, has_s0):
    if has_s0:
        gq_ref, gk_ref, gv_ref, og_ref, la_ref, gn_ref, s0_ref, o_ref, sfin_ref, st_ref = refs
    else:
        gq_ref, gk_ref, gv_ref, og_ref, la_ref, gn_ref, o_ref, sfin_ref, st_ref = refs
    j = pl.program_id(1)
    tb = chunk * n_chunks

    @pl.when(j == 0)
    def _():
        for hh in range(GLA_HEADS):
            st_ref[hh] = s0_ref[hh].T if has_s0 else jnp.zeros((GLA_DV, GLA_DK), F32)

    r_i = lax.broadcasted_iota(jnp.int32, (chunk, chunk), 0)
    c_i = lax.broadcasted_iota(jnp.int32, (chunk, chunk), 1)
    causal = r_i >= c_i
    tril = jnp.where(causal, 1.0, 0.0)
    gn = gn_ref[...]

    def chunk_body(c, carry):
        r0 = pl.multiple_of(c * chunk, chunk)
        rows = pl.ds(r0, chunk)
        pos = j * tb + r0 + lax.broadcasted_iota(jnp.int32, (chunk, GLA_KW), 0)
        valid = (pos >= off) & (pos < t_end)
        la = jnp.where(valid, la_ref[rows, :], 0.0)
        k = jnp.where(valid, gk_ref[rows, :], 0.0)
        q = gq_ref[rows, :] * GLA_DK ** -0.5
        v = gv_ref[rows, :]
        b = jnp.dot(tril, la, preferred_element_type=F32, precision=lax.Precision.HIGHEST)
        b_last = b[chunk - 1:chunk, :]
        qd = (q * jnp.exp(b)).astype(BF16)
        kd = (k * jnp.exp(-b)).astype(BF16)
        ke = (k * jnp.exp(b_last - b)).astype(BF16)
        decay = jnp.exp(b_last)
        outs = []
        for hh in range(GLA_HEADS):
            ks = slice(hh * GLA_DK, (hh + 1) * GLA_DK)
            vh = v[:, hh * GLA_DV:(hh + 1) * GLA_DV]
            a = lax.dot_general(qd[:, ks], kd[:, ks], NT_DIMS, preferred_element_type=F32)
            a = jnp.where(causal, a, 0.0).astype(BF16)
            st = st_ref[hh]
            o = (jnp.dot(a, vh, preferred_element_type=F32)
                 + lax.dot_general(qd[:, ks], st.astype(BF16), NT_DIMS, preferred_element_type=F32))
            u_t = lax.dot_general(vh, ke[:, ks], TN_DIMS, preferred_element_type=F32)
            st_ref[hh] = decay[:, ks] * st + u_t
            o = o * lax.rsqrt(jnp.mean(o * o, axis=-1, keepdims=True) + LN_EPS) * gn
            outs.append(o)
        og = og_ref[rows, :]
        o_ref[rows, :] = (jnp.concatenate(outs, axis=1) * (og * jax.nn.sigmoid(og))).astype(BF16)
        return carry

    lax.fori_loop(0, n_chunks, chunk_body, 0)

    @pl.when(j == pl.num_programs(1) - 1)
    def _():
        for hh in range(GLA_HEADS):
            sfin_ref[hh] = st_ref[hh].T


def _gla(gq, gk, gv, og, la, gnorm, s0, *, batch, t_pad, tb, chunk, off, t_end):
    n_steps = t_pad // tb
    tok = lambda n: pl.BlockSpec((tb, n), lambda b, j: (b * n_steps + j, 0))
    state = pl.BlockSpec((None, GLA_HEADS, GLA_DK, GLA_DV), lambda b, j: (b, 0, 0, 0))
    has_s0 = s0 is not None
    kern = functools.partial(_gla_kernel, chunk=chunk, n_chunks=tb // chunk, off=off, t_end=t_end, has_s0=has_s0)
    in_specs = [tok(GLA_KW), tok(GLA_KW), tok(GLA_VW), tok(GLA_VW), tok(GLA_KW), _full_spec(gnorm.shape)]
    args = [gq, gk, gv, og, la, gnorm]
    if has_s0:
        in_specs.append(state)
        args.append(s0)
    return pl.pallas_call(
        kern,
        grid=(batch, n_steps),
        in_specs=in_specs,
        out_specs=(tok(GLA_VW), state),
        out_shape=(jax.ShapeDtypeStruct((batch * t_pad, GLA_VW), BF16),
                   jax.ShapeDtypeStruct((batch, GLA_HEADS, GLA_DK, GLA_DV), F32)),
        scratch_shapes=[pltpu.VMEM((GLA_HEADS, GLA_DV, GLA_DK), F32)],
        compiler_params=_cparams(2),
        name="gla",
    )(*args)


def _merge_kernel(x_ref, att_ref, gla_ref, gates_ref, lng_ref, lnb_ref, wa_ref, wg_ref, wo_ref,
                  l1g_ref, l1b_ref, h1_ref, *, alpha):
    h = _layer_norm(x_ref[...], lng_ref[...], lnb_ref[...])
    pa = jnp.dot(att_ref[...], wa_ref[...], preferred_element_type=F32)
    pg = jnp.dot(gla_ref[...], wg_ref[...], preferred_element_type=F32)
    gates = gates_ref[...].astype(F32)
    merged = gates[:, :D_MODEL] * pa + gates[:, D_MODEL:] * pg
    mix = jnp.dot(merged.astype(BF16), wo_ref[...], preferred_element_type=F32)
    h1_ref[...] = _layer_norm(alpha * h + mix, l1g_ref[...], l1b_ref[...])


def _merge(x, att, gla, gates, ln_g, ln_b, wa, wg, wo, l1g, l1b, *, alpha):
    n_tok = x.shape[0]
    tm = _row_tile(n_tok, TM_TOKEN)
    row = lambda n: pl.BlockSpec((tm, n), lambda i: (i, 0))
    consts = (ln_g, ln_b, wa, wg, wo, l1g, l1b)
    return pl.pallas_call(
        functools.partial(_merge_kernel, alpha=alpha),
        grid=(n_tok // tm,),
        in_specs=[row(D_MODEL), row(ATT_WIDTH), row(GLA_VW), row(N_BRANCHES * D_MODEL)]
                 + [_full_spec(a.shape) for a in consts],
        out_specs=row(D_MODEL),
        out_shape=jax.ShapeDtypeStruct((n_tok, D_MODEL), F32),
        compiler_params=_cparams(1),
        name="merge",
    )(x, att, gla, gates, *consts)


def _ffn_kernel(h_ref, wu_ref, wd_ref, g_ref, b_ref, y_ref, *, alpha, n_split):
    h = h_ref[...]
    hb = h.astype(BF16)
    w = D_FF // n_split
    ff = jnp.zeros(h.shape, F32)
    for s in range(n_split):
        u = jnp.dot(hb, wu_ref[:, s * w:(s + 1) * w], preferred_element_type=F32)
        u = jnp.square(jnp.maximum(u, 0.0)).astype(BF16)
        ff = ff + jnp.dot(u, wd_ref[s * w:(s + 1) * w, :], preferred_element_type=F32)
    y_ref[...] = _layer_norm(alpha * h + ff, g_ref[...], b_ref[...])


def _ffn(h1, wu, wd, g, b, *, alpha):
    n_tok = h1.shape[0]
    tm = _row_tile(n_tok, TM_TOKEN)
    row = pl.BlockSpec((tm, D_MODEL), lambda i: (i, 0))
    return pl.pallas_call(
        functools.partial(_ffn_kernel, alpha=alpha, n_split=4),
        grid=(n_tok // tm,),
        in_specs=[row] + [_full_spec(a.shape) for a in (wu, wd, g, b)],
        out_specs=row,
        out_shape=jax.ShapeDtypeStruct((n_tok, D_MODEL), F32),
        compiler_params=_cparams(1),
        name="ffn",
    )(h1, wu, wd, g, b)


def _pack_weights(w_in, w_gla_a2, b_gla_a):
    points = []
    acc = 0
    for s in IN_SIZES[:-1]:
        acc += s
        points.append(acc)
    wq, wk, wv, wqi, wki, wwi, wgq, wgk, wgv, wog, wa1, wgt = jnp.split(w_in, points, axis=-1)
    wq = wq * (math.log2(math.e) * HEAD_DIM ** -0.5)
    wqi = wqi * IDX_DIM ** -0.5
    pad_cols = lambda a, n: jnp.pad(a, ((0, 0), (0, n - a.shape[1])))
    wv_heads = wv.T.reshape(N_KV_HEADS, HEAD_DIM, D_MODEL)
    wv_aug = jnp.pad(wv_heads, ((0, 0), (0, LANES - HEAD_DIM), (0, 0))).reshape(N_KV_HEADS * LANES, D_MODEL)
    w = {
        "qT": wq.T, "qiT": wqi.T, "vT": wv_aug,
        "wiT": jnp.pad(wwi.T, ((0, 2 * SUBLANES - IDX_HEADS), (0, 0))),
        "row_p": pad_cols(jnp.concatenate([wk, wv, wki], axis=1), 2 * KV_WIDTH + LANES),
        "row_s": pad_cols(jnp.concatenate([wq, wqi, wk, wv, wki, wwi], axis=1),
                          ATT_WIDTH + IDX_HEADS * IDX_DIM + 2 * KV_WIDTH + LANES),
        "gla": jnp.concatenate([wgq, wgk, wgv, wog], axis=1),
        "a1": pad_cols(wa1, LANES),
        "a2": jnp.pad(w_gla_a2, ((0, LANES - GLA_GATE_RANK), (0, 0))),
        "gt": wgt,
    }
    w = {name: a.astype(BF16) for name, a in w.items()}
    w["ba"] = b_gla_a.astype(F32)[None]
    return w


def _round_up(x, m):
    return -(-x // m) * m


def _row_tile(n, pref, unit=2 * SUBLANES):
    best = unit
    for t in range(unit, min(n, pref) + 1, unit):
        if n % t == 0:
            best = t
    assert n % best == 0
    return best


def kernel(x_prompt, x_sample, cache_k, cache_v, cache_kidx, state_gla, page_table, meta_tokens, ln_in_g, ln_in_b, w_in, w_gla_a2, b_gla_a, gla_norm_g, w_proj_attn, w_proj_gla, w_out, ln1_g, ln1_b, w_ff_up, w_ff_down, ln2_g, ln2_b):
    depth = w_in.shape[0]
    assert depth == 1, "single-layer step only"
    B, S_p, D = x_prompt.shape
    DB, S_s, _ = x_sample.shape
    n_pages = page_table.shape[1]
    past = n_pages * PAGE_SIZE
    assert D == D_MODEL and S_p % GLA_CHUNK == 0 and S_s <= SUBLANES and n_pages % (2 * PAGES_PER_STEP) == 0
    topk_prompt = min(TOPK_MAX, S_p // 4)
    topk_sample = min(TOPK_MAX, (past + S_s) // 4)
    alpha = (2 * depth) ** 0.25

    row = lambda a: a.astype(F32).reshape(1, -1)
    ln_g, ln_b = row(ln_in_g), row(ln_in_b)
    w = _pack_weights(w_in[0], w_gla_a2[0], b_gla_a[0])
    wa, wg, wo = (a[0].astype(BF16) for a in (w_proj_attn, w_proj_gla, w_out))
    wu, wd = w_ff_up[0].astype(BF16), w_ff_down[0].astype(BF16)
    gnorm = row(gla_norm_g[0])
    l1g, l1b, l2g, l2b = row(ln1_g[0]), row(ln1_b[0]), row(ln2_g[0]), row(ln2_b[0])

    T = S_p + N_META
    off = (-N_META) % GLA_CHUNK
    t_pad = _round_up(off + T, math.lcm(KEY_CHUNK, GLA_CHUNK))
    gla_tb = _row_tile(t_pad, GLA_TOKENS_PER_STEP, GLA_CHUNK)
    n_tok = B * t_pad
    meta = jnp.broadcast_to(meta_tokens.astype(x_prompt.dtype)[None], (B, N_META, D))
    xp = jnp.concatenate([jnp.zeros((B, off, D), x_prompt.dtype), meta, x_prompt,
                          jnp.zeros((B, t_pad - off - T, D), x_prompt.dtype)], axis=1).reshape(n_tok, D)

    qT, qiT, vT3, wiT, k32, v32, ki32, kb, kib = _proj_attn_prompt(xp, ln_g, ln_b, w, n_tok)
    att_p = _dsa_prompt(qT, qiT, wiT,
                        kb.reshape(n_tok // KEY_CHUNK, KEY_CHUNK, KV_WIDTH),
                        kib.reshape(n_tok // KEY_CHUNK, KEY_CHUNK, IDX_DIM), vT3,
                        batch=B, t_pad=t_pad, topk=topk_prompt, off=off)
    gq, gk, gv, og, la, gates_p = _proj_gla(xp, ln_g, ln_b, w, n_tok)
    gla_p, state_p = _gla(gq, gk, gv, og, la, gnorm, None, batch=B, t_pad=t_pad, tb=gla_tb,
                          chunk=GLA_CHUNK, off=off, t_end=off + T)
    h1_p = _merge(xp, att_p, gla_p, gates_p, ln_g, ln_b, wa, wg, wo, l1g, l1b, alpha=alpha)
    y_p = _ffn(h1_p, wu, wd, l2g, l2b, alpha=alpha)

    seq = lambda a: a.reshape((B, t_pad) + a.shape[1:])
    y_prompt = seq(y_p)[:, off + N_META:off + T]
    k_prompt = seq(k32)[:, off:off + T].reshape(1, B, T, N_KV_HEADS, HEAD_DIM)
    v_prompt = seq(v32)[:, off:off + T].reshape(1, B, T, N_KV_HEADS, HEAD_DIM)
    kidx_prompt = seq(ki32)[:, off:off + T][None]
    gla_state_prompt = state_p[None]

    R = SAMPLE_ROWS
    n_tok_s = DB * R
    xs = jnp.pad(x_sample, ((0, 0), (0, R - S_s), (0, 0))).reshape(n_tok_s, D)
    q_s, qi_s, k_s, v_s, kiwi_s = _proj_attn_sample(xs, ln_g, ln_b, w, n_tok_s)
    sseq = lambda a: a.reshape((DB, R) + a.shape[1:])[:, :S_s]
    k_new, v_new = sseq(k_s), sseq(v_s)
    ki_new = sseq(kiwi_s)[..., :IDX_DIM]
    wi_new = sseq(kiwi_s)[..., IDX_DIM:IDX_DIM + IDX_HEADS]
    qi_stack = sseq(qi_s).reshape(DB, S_s, IDX_HEADS, IDX_DIM).transpose(0, 2, 1, 3).reshape(DB, IDX_HEADS * S_s, IDX_DIM)
    wcol = jnp.broadcast_to(wi_new.transpose(0, 2, 1).reshape(DB, IDX_HEADS * S_s, 1), (DB, IDX_HEADS * S_s, LANES))
    q_heads = sseq(q_s).reshape(DB, S_s, N_HEADS, HEAD_DIM).transpose(0, 2, 1, 3)
    kv_of_head = (jnp.arange(N_HEADS) // GROUP)[:, None] == jnp.arange(N_KV_HEADS)[None, :]
    qbd = jnp.where(kv_of_head[None, :, None, :, None], q_heads[:, :, :, None, :], jnp.zeros((), BF16))
    qbd = qbd.reshape(DB, N_HEADS * S_s, KV_WIDTH)
    pad_page = lambda a: jnp.pad(a, ((0, 0), (0, PAGE_SIZE - S_s), (0, 0))).astype(BF16).transpose(0, 2, 1)
    kinew_pg, knew_pg, vnew_pg = pad_page(ki_new), pad_page(k_new), pad_page(v_new)
    ck = cache_k[0].transpose(0, 2, 3, 1).reshape(-1, KV_WIDTH, PAGE_SIZE)
    cv = cache_v[0].transpose(0, 2, 3, 1).reshape(-1, KV_WIDTH, PAGE_SIZE)
    cki = cache_kidx[0].transpose(0, 2, 1)
    sc_past, sc_new = _idx_sample(page_table, qi_stack, wcol, kinew_pg, cki, s_q=S_s)
    thr, jmax = _sel_sample(sc_past, sc_new, topk=topk_sample)
    o_s = _att_sample(page_table, qbd, sc_past, sc_new, thr, jmax, knew_pg, vnew_pg, ck, cv, s_q=S_s)
    o_s = o_s.reshape(DB, N_KV_HEADS, GROUP, S_s, N_KV_HEADS, HEAD_DIM)
    att_s = jnp.stack([o_s[:, n, :, :, n, :] for n in range(N_KV_HEADS)], axis=1)
    att_s = att_s.transpose(0, 3, 1, 2, 4).reshape(DB, S_s, ATT_WIDTH).astype(BF16)
    att_s = jnp.pad(att_s, ((0, 0), (0, R - S_s), (0, 0))).reshape(n_tok_s, ATT_WIDTH)

    gq, gk, gv, og, la, gates_s = _proj_gla(xs, ln_g, ln_b, w, n_tok_s)
    gla_s, state_s = _gla(gq, gk, gv, og, la, gnorm, state_gla[0], batch=DB, t_pad=R, tb=R,
                          chunk=R, off=0, t_end=S_s)
    h1_s = _merge(xs, att_s, gla_s, gates_s, ln_g, ln_b, wa, wg, wo, l1g, l1b, alpha=alpha)
    y_s = _ffn(h1_s, wu, wd, l2g, l2b, alpha=alpha)

    y_sample = y_s.reshape(DB, R, D)[:, :S_s]
    k_sample = k_new.reshape(1, DB, S_s, N_KV_HEADS, HEAD_DIM)
    v_sample = v_new.reshape(1, DB, S_s, N_KV_HEADS, HEAD_DIM)
    kidx_sample = ki_new[None]
    gla_state_sample = state_s[None]
    return (y_prompt, y_sample, k_prompt, v_prompt, kidx_prompt, gla_state_prompt,
            k_sample, v_sample, kidx_sample, gla_state_sample)
```

```python
import functools
import math

import jax
import jax.numpy as jnp
from jax import lax
from jax.experimental import pallas as pl
from jax.experimental.pallas import tpu as pltpu

D_MODEL = 1024
PAGE_SIZE = 128
N_META = 16
N_HEADS = 16
HEAD_DIM = 64
N_KV_HEADS = 4
GROUP = N_HEADS // N_KV_HEADS
ATT_WIDTH = N_HEADS * HEAD_DIM
KV_WIDTH = N_KV_HEADS * HEAD_DIM
IDX_HEADS = 8
IDX_DIM = 64
TOPK_MAX = 256
GLA_HEADS = 4
GLA_DK = D_MODEL // 2 // GLA_HEADS
GLA_DV = D_MODEL // GLA_HEADS
GLA_KW = GLA_HEADS * GLA_DK
GLA_VW = GLA_HEADS * GLA_DV
GLA_GATE_RANK = 16
GLA_TAU = 16.0
GLA_CHUNK = 64
N_BRANCHES = 2
D_FF = 4 * D_MODEL
LN_EPS = 1e-5
IN_SIZES = (ATT_WIDTH, KV_WIDTH, KV_WIDTH, IDX_HEADS * IDX_DIM, IDX_DIM, IDX_HEADS,
            GLA_KW, GLA_KW, GLA_VW, GLA_VW, GLA_GATE_RANK, N_BRANCHES * D_MODEL)

LANES = 128
SUBLANES = 8
VMEM_LIMIT_BYTES = 56 * 1024 * 1024
COUNT_LANES = 4
Q_BLOCK = LANES
KEY_CHUNK = 3 * LANES
SAMPLE_ROWS = 16
PAGES_PER_STEP = 16
IDX_PAGES_PER_DOT = 4
SEL_SEQS = 16
TM_PROJ_GLA = 256
TM_TOKEN = 512
GLA_TOKENS_PER_STEP = 11 * GLA_CHUNK

F32 = jnp.float32
BF16 = jnp.bfloat16
NEG_INF = float("-inf")
INT_MIN = -2 ** 31
NT_DIMS = (((1,), (1,)), ((), ()))
TN_DIMS = (((0,), (0,)), ((), ()))


def _cparams(n_grid):
    return pltpu.CompilerParams(dimension_semantics=("arbitrary",) * n_grid,
                                vmem_limit_bytes=VMEM_LIMIT_BYTES)


def _full_spec(shape):
    nd = len(shape)
    return pl.BlockSpec(shape, lambda *_: (0,) * nd)


def _layer_norm(x, g, b):
    mu = jnp.mean(x, axis=-1, keepdims=True)
    xc = x - mu
    var = jnp.mean(xc * xc, axis=-1, keepdims=True)
    return xc * lax.rsqrt(var + LN_EPS) * g + b


def _top_half(x):
    bits = lax.bitcast_convert_type(x, jnp.int32) & jnp.int32(-65536)
    return lax.bitcast_convert_type(bits, F32)


def _ukey_to_f32(u):
    bits = jnp.where(u < 0, u & jnp.int32(0x7FFFFFFF), ~u)
    return lax.bitcast_convert_type(bits, F32)


def _proj_attn_prompt_kernel(x_ref, g_ref, b_ref, wq_ref, wqi_ref, wv_ref, wwi_ref, wrow_ref,
                             qT_ref, qiT_ref, vT_ref, wiT_ref, k_ref, v_ref, ki_ref, kb_ref, kib_ref):
    hb = _layer_norm(x_ref[...], g_ref[...], b_ref[...]).astype(BF16)
    qT_ref[...] = lax.dot_general(wq_ref[...], hb, NT_DIMS, preferred_element_type=F32).astype(BF16)
    qiT_ref[...] = lax.dot_general(wqi_ref[...], hb, NT_DIMS, preferred_element_type=F32).astype(BF16)
    vt = lax.dot_general(wv_ref[...], hb, NT_DIMS, preferred_element_type=F32)
    row = lax.broadcasted_iota(jnp.int32, vt.shape, 0)
    vt = jnp.where(row % LANES == HEAD_DIM, 1.0, vt)
    vT_ref[0] = vt.astype(BF16)
    wi = lax.dot_general(wwi_ref[...], hb, NT_DIMS, preferred_element_type=F32)
    wiT_ref[...] = wi[:IDX_HEADS] * IDX_HEADS ** -0.5
    y = jnp.dot(hb, wrow_ref[...], preferred_element_type=F32)
    k = y[:, :KV_WIDTH]
    ki = y[:, 2 * KV_WIDTH:2 * KV_WIDTH + IDX_DIM]
    k_ref[...] = k
    v_ref[...] = y[:, KV_WIDTH:2 * KV_WIDTH]
    ki_ref[...] = ki
    kb_ref[...] = k.astype(BF16)
    kib_ref[...] = ki.astype(BF16)


def _proj_attn_prompt(x, ln_g, ln_b, w, n_tok):
    tm = KEY_CHUNK
    n_steps = n_tok // tm
    row = lambda n: pl.BlockSpec((tm, n), lambda i: (i, 0))
    col = lambda n: pl.BlockSpec((n, tm), lambda i: (0, i))
    out_shape = (
        jax.ShapeDtypeStruct((ATT_WIDTH, n_tok), BF16),
        jax.ShapeDtypeStruct((IDX_HEADS * IDX_DIM, n_tok), BF16),
        jax.ShapeDtypeStruct((n_steps, N_KV_HEADS * LANES, tm), BF16),
        jax.ShapeDtypeStruct((IDX_HEADS, n_tok), F32),
        jax.ShapeDtypeStruct((n_tok, KV_WIDTH), F32),
        jax.ShapeDtypeStruct((n_tok, KV_WIDTH), F32),
        jax.ShapeDtypeStruct((n_tok, IDX_DIM), F32),
        jax.ShapeDtypeStruct((n_tok, KV_WIDTH), BF16),
        jax.ShapeDtypeStruct((n_tok, IDX_DIM), BF16),
    )
    out_specs = (col(ATT_WIDTH), col(IDX_HEADS * IDX_DIM),
                 pl.BlockSpec((1, N_KV_HEADS * LANES, tm), lambda i: (i, 0, 0)),
                 col(IDX_HEADS), row(KV_WIDTH), row(KV_WIDTH), row(IDX_DIM), row(KV_WIDTH), row(IDX_DIM))
    ws = (w["qT"], w["qiT"], w["vT"], w["wiT"], w["row_p"])
    return pl.pallas_call(
        _proj_attn_prompt_kernel,
        grid=(n_steps,),
        in_specs=[row(D_MODEL), _full_spec(ln_g.shape), _full_spec(ln_b.shape)] + [_full_spec(a.shape) for a in ws],
        out_specs=out_specs,
        out_shape=out_shape,
        compiler_params=_cparams(1),
        name="proj_attn_prompt",
    )(x, ln_g, ln_b, *ws)


def _proj_attn_sample_kernel(x_ref, g_ref, b_ref, w_ref, scale_ref, q_ref, qi_ref, k_ref, v_ref, kiwi_ref):
    hb = _layer_norm(x_ref[...], g_ref[...], b_ref[...]).astype(BF16)
    y = jnp.dot(hb, w_ref[...], preferred_element_type=F32)
    o = 0
    q_ref[...] = y[:, o:o + ATT_WIDTH].astype(BF16)
    o += ATT_WIDTH
    qi_ref[...] = y[:, o:o + IDX_HEADS * IDX_DIM].astype(BF16)
    o += IDX_HEADS * IDX_DIM
    k_ref[...] = y[:, o:o + KV_WIDTH]
    o += KV_WIDTH
    v_ref[...] = y[:, o:o + KV_WIDTH]
    o += KV_WIDTH
    kiwi_ref[...] = y[:, o:o + LANES] * scale_ref[...]


def _proj_attn_sample(x, ln_g, ln_b, w, n_tok):
    tm = _row_tile(n_tok, TM_TOKEN)
    row = lambda n: pl.BlockSpec((tm, n), lambda i: (i, 0))
    lane = lax.iota(jnp.int32, LANES)
    scale = jnp.where((lane >= IDX_DIM) & (lane < IDX_DIM + IDX_HEADS), IDX_HEADS ** -0.5, 1.0).astype(F32)[None]
    out_shape = (
        jax.ShapeDtypeStruct((n_tok, ATT_WIDTH), BF16),
        jax.ShapeDtypeStruct((n_tok, IDX_HEADS * IDX_DIM), BF16),
        jax.ShapeDtypeStruct((n_tok, KV_WIDTH), F32),
        jax.ShapeDtypeStruct((n_tok, KV_WIDTH), F32),
        jax.ShapeDtypeStruct((n_tok, LANES), F32),
    )
    return pl.pallas_call(
        _proj_attn_sample_kernel,
        grid=(n_tok // tm,),
        in_specs=[row(D_MODEL), _full_spec(ln_g.shape), _full_spec(ln_b.shape),
                  _full_spec(w["row_s"].shape), _full_spec(scale.shape)],
        out_specs=(row(ATT_WIDTH), row(IDX_HEADS * IDX_DIM), row(KV_WIDTH), row(KV_WIDTH), row(LANES)),
        out_shape=out_shape,
        compiler_params=_cparams(1),
        name="proj_attn_sample",
    )(x, ln_g, ln_b, w["row_s"], scale)


def _log_sigmoid(x):
    return jnp.minimum(x, 0.0) - jnp.log1p(jnp.exp(-jnp.abs(x)))


def _proj_gla_kernel(x_ref, g_ref, b_ref, wg_ref, wa1_ref, wa2_ref, ba_ref, wgt_ref,
                     gq_ref, gk_ref, gv_ref, og_ref, la_ref, gates_ref):
    hb = _layer_norm(x_ref[...], g_ref[...], b_ref[...]).astype(BF16)
    y = jnp.dot(hb, wg_ref[...], preferred_element_type=F32)
    gq_ref[...] = y[:, :GLA_KW]
    gk_ref[...] = y[:, GLA_KW:2 * GLA_KW]
    gv_ref[...] = y[:, 2 * GLA_KW:2 * GLA_KW + GLA_VW].astype(BF16)
    og_ref[...] = y[:, 2 * GLA_KW + GLA_VW:]
    a1 = jnp.dot(hb, wa1_ref[...], preferred_element_type=F32).astype(BF16)
    z = jnp.dot(a1, wa2_ref[...], preferred_element_type=F32) + ba_ref[...]
    la_ref[...] = _log_sigmoid(z) / GLA_TAU
    gt = jnp.dot(hb, wgt_ref[...], preferred_element_type=F32)
    gates_ref[...] = jax.nn.sigmoid(gt).astype(BF16)


def _proj_gla(x, ln_g, ln_b, w, n_tok):
    tm = _row_tile(n_tok, TM_PROJ_GLA)
    row = lambda n: pl.BlockSpec((tm, n), lambda i: (i, 0))
    ws = (w["gla"], w["a1"], w["a2"], w["ba"], w["gt"])
    out_shape = (
        jax.ShapeDtypeStruct((n_tok, GLA_KW), F32),
        jax.ShapeDtypeStruct((n_tok, GLA_KW), F32),
        jax.ShapeDtypeStruct((n_tok, GLA_VW), BF16),
        jax.ShapeDtypeStruct((n_tok, GLA_VW), F32),
        jax.ShapeDtypeStruct((n_tok, GLA_KW), F32),
        jax.ShapeDtypeStruct((n_tok, N_BRANCHES * D_MODEL), BF16),
    )
    return pl.pallas_call(
        _proj_gla_kernel,
        grid=(n_tok // tm,),
        in_specs=[row(D_MODEL), _full_spec(ln_g.shape), _full_spec(ln_b.shape)] + [_full_spec(a.shape) for a in ws],
        out_specs=(row(GLA_KW), row(GLA_KW), row(GLA_VW), row(GLA_VW), row(GLA_KW), row(N_BRANCHES * D_MODEL)),
        out_shape=out_shape,
        compiler_params=_cparams(1),
        name="proj_gla",
    )(x, ln_g, ln_b, *ws)


def _dsa_prompt_kernel(qT_ref, qiT_ref, wiT_ref, kb_ref, kib_ref, vT_ref, att_ref,
                       sc_ref, sch_ref, qn_ref, acc_ref, j_ref, sa_ref, sb_ref, *, topk, off, n_pos_bits):
    kc = KEY_CHUNK
    i = pl.program_id(1)
    n_chunks = (i * Q_BLOCK + Q_BLOCK + kc - 1) // kc
    qpos = i * Q_BLOCK + lax.broadcasted_iota(jnp.int32, (1, Q_BLOCK), 1)

    def key_pos(c):
        return c * kc + lax.broadcasted_iota(jnp.int32, (kc, Q_BLOCK), 0)

    def run_indexer():
        wi = wiT_ref[...]

        def idx_body(c, carry):
            kic = kib_ref[c]
            acc = jnp.zeros((kc, Q_BLOCK), F32)
            for hp in range(IDX_HEADS // 2):
                r0 = 2 * hp * IDX_DIM
                rhs = jnp.concatenate([qiT_ref[r0:r0 + IDX_DIM, :], qiT_ref[r0 + IDX_DIM:r0 + 2 * IDX_DIM, :]], axis=1)
                s = jnp.maximum(jnp.dot(kic, rhs, preferred_element_type=F32), 0.0)
                acc = acc + s[:, :Q_BLOCK] * wi[2 * hp:2 * hp + 1, :] + s[:, Q_BLOCK:] * wi[2 * hp + 1:2 * hp + 2, :]
            kp = key_pos(c)
            valid = (kp >= off) & (kp <= qpos)
            sc = jnp.where(valid, acc, NEG_INF)
            sc_ref[c] = sc
            sch_ref[c] = _top_half(sc).astype(BF16)
            return carry

        lax.fori_loop(0, n_chunks, idx_body, 0)

    def chunk_counts(pred_c, accs):
        v = jnp.where(pred_c, 1.0, 0.0).reshape(COUNT_LANES, kc // (SUBLANES * COUNT_LANES), SUBLANES, Q_BLOCK)
        return tuple(a + jnp.sum(v[r], axis=0) for r, a in enumerate(accs))

    zero_accs = (jnp.zeros((SUBLANES, Q_BLOCK), F32),) * COUNT_LANES

    def total(accs):
        return jnp.sum(functools.reduce(lambda a, b: a + b, accs), axis=0, keepdims=True)

    def count(pred):
        return total(lax.fori_loop(0, n_chunks, lambda c, accs: chunk_counts(pred(c), accs), zero_accs))

    pk = 2 * SUBLANES
    one_b, zero_b = jnp.ones((), BF16), jnp.zeros((), BF16)

    def count_top(cand_top):
        def body(c, accs):
            ge = sch_ref[c].reshape(kc // pk, pk, Q_BLOCK) >= cand_top
            v = jnp.where(ge, one_b, zero_b).reshape(COUNT_LANES, kc // (pk * COUNT_LANES), pk, Q_BLOCK)
            return tuple(functools.reduce(lambda x, y: x + y, [a] + [v[r, k] for k in range(v.shape[1])])
                         for r, a in enumerate(accs))
        accs = lax.fori_loop(0, n_chunks, body, (jnp.zeros((pk, Q_BLOCK), BF16),) * COUNT_LANES)
        return jnp.sum(functools.reduce(lambda x, y: x + y, [a.astype(F32) for a in accs]), axis=0, keepdims=True)

    def search():
        def top_body(t, res):
            trial = res | jnp.left_shift(jnp.int32(1), 31 - t)
            cand = jnp.broadcast_to(_top_half(_ukey_to_f32(trial)), (pk, Q_BLOCK)).astype(BF16)
            return jnp.where(count_top(cand[None]) >= topk, trial, res)

        def bit_body(t, res):
            trial = res | jnp.left_shift(jnp.int32(1), 31 - t)
            cand = _ukey_to_f32(trial)
            cnt = count(lambda c: sc_ref[c] >= cand)
            return jnp.where(cnt >= topk, trial, res)

        res = lax.fori_loop(0, 16, top_body, jnp.zeros((1, Q_BLOCK), jnp.int32))
        return lax.fori_loop(16, 32, bit_body, res)

    def finish_selection(res):
        few = (qpos - off + 1) <= topk
        thr = jnp.where(few, NEG_INF, _ukey_to_f32(res))
        cnt_gt = count(lambda c: sc_ref[c] > thr)
        cnt_ge = count(lambda c: sc_ref[c] >= thr)
        n_ties = topk - cnt_gt
        tie_rows = jnp.where(few, 0.0, jnp.where(cnt_ge > topk, 1.0, 0.0))
        j_ref[...] = jnp.full(j_ref.shape, 2 ** 30, jnp.int32)

        @pl.when(jnp.max(tie_rows) > 0.0)
        def _():
            def jbit_body(t, resj):
                trial = resj | jnp.left_shift(jnp.int32(1), n_pos_bits - 1 - t)
                below = count(lambda c: (sc_ref[c] == thr) & (key_pos(c) < trial))
                return jnp.where(below < n_ties, trial, resj)
            j_ref[...] = lax.fori_loop(0, n_pos_bits, jbit_body, jnp.zeros((1, Q_BLOCK), jnp.int32))

        jmax = j_ref[...]

        def bias_body(c, carry):
            sc = sc_ref[c]
            kp = key_pos(c)
            valid = (kp >= off) & (kp <= qpos)
            sel = (sc > thr) | ((sc == thr) & (kp <= jmax))
            sc_ref[c] = jnp.where(valid & sel, 0.0, NEG_INF)
            return carry

        lax.fori_loop(0, n_chunks, bias_body, 0)

    gw = GROUP * Q_BLOCK
    last = n_chunks - 1
    m_none = jnp.full((1, N_HEADS * Q_BLOCK), NEG_INF, F32)

    def score_stage(c, buf_ref, m_run):
        bias = jnp.concatenate([sc_ref[c]] * N_HEADS, axis=1)
        s = jnp.dot(kb_ref[c], qn_ref[...], preferred_element_type=F32) + bias
        buf_ref[...] = s
        return jnp.maximum(m_run, jnp.max(s, axis=0, keepdims=True))

    def prob_stage(c, buf_ref, m_before, m_with, weight):
        m_safe = jnp.maximum(m_with, -1e30)
        alpha = jnp.exp2(m_before - m_safe)
        p = jnp.exp2(buf_ref[...] - m_safe).astype(BF16)
        vt = vT_ref[c]
        for n in range(N_KV_HEADS):
            cols = slice(n * gw, (n + 1) * gw)
            pv = jnp.dot(vt[n * LANES:(n + 1) * LANES, :], p[:, cols], preferred_element_type=F32)
            acc_ref[n] = acc_ref[n] * alpha[:, cols] + weight * pv

    def attention_start():
        qn_ref[...] = jnp.zeros(qn_ref.shape, BF16)
        for h in range(N_HEADS):
            n = h // GROUP
            qn_ref[n * HEAD_DIM:(n + 1) * HEAD_DIM, h * Q_BLOCK:(h + 1) * Q_BLOCK] = qT_ref[h * HEAD_DIM:(h + 1) * HEAD_DIM, :]
        acc_ref[...] = jnp.zeros(acc_ref.shape, F32)
        return m_none, score_stage(0, sa_ref, m_none)

    def pair_step(t, m_prev, m_cur):
        c0 = 2 * t
        on = lambda c: jnp.where(c <= last, 1.0, 0.0)
        clamp = lambda c: jnp.minimum(c, last)
        m_1 = score_stage(clamp(c0 + 1), sb_ref, m_cur)
        prob_stage(clamp(c0), sa_ref, m_prev, m_cur, on(c0))
        m_2 = score_stage(clamp(c0 + 2), sa_ref, m_1)
        prob_stage(clamp(c0 + 1), sb_ref, m_cur, m_1, on(c0 + 1))
        return m_1, m_2

    def attention_finish():
        heads = []
        for n in range(N_KV_HEADS):
            a = acc_ref[n]
            denom = a[HEAD_DIM:HEAD_DIM + 1, :]
            o = a[:HEAD_DIM, :] / jnp.where(denom > 0.0, denom, 1.0)
            for g in range(GROUP):
                heads.append(o[:, g * Q_BLOCK:(g + 1) * Q_BLOCK])
        att_ref[...] = jnp.concatenate(heads, axis=0).T.astype(BF16)

    run_indexer()
    finish_selection(search())
    lax.fori_loop(0, (n_chunks + 1) // 2, lambda t, ms: pair_step(t, *ms), attention_start())
    attention_finish()


def _dsa_prompt(qT, qiT, wiT, kb3, kib3, vT3, *, batch, t_pad, topk, off):
    kc = KEY_CHUNK
    nqb = t_pad // Q_BLOCK
    ncb = t_pad // kc
    n_tok = batch * t_pad
    qcol = lambda n: pl.BlockSpec((n, Q_BLOCK), lambda b, i: (0, b * nqb + i))
    kern = functools.partial(_dsa_prompt_kernel, topk=topk, off=off,
                             n_pos_bits=max(1, math.ceil(math.log2(t_pad))))
    return pl.pallas_call(
        kern,
        grid=(batch, nqb),
        in_specs=[qcol(ATT_WIDTH), qcol(IDX_HEADS * IDX_DIM), qcol(IDX_HEADS),
                  pl.BlockSpec((ncb, kc, KV_WIDTH), lambda b, i: (b, 0, 0)),
                  pl.BlockSpec((ncb, kc, IDX_DIM), lambda b, i: (b, 0, 0)),
                  pl.BlockSpec((ncb, N_KV_HEADS * LANES, kc), lambda b, i: (b, 0, 0))],
        out_specs=pl.BlockSpec((Q_BLOCK, ATT_WIDTH), lambda b, i: (b * nqb + i, 0)),
        out_shape=jax.ShapeDtypeStruct((n_tok, ATT_WIDTH), BF16),
        scratch_shapes=[pltpu.VMEM((ncb, kc, Q_BLOCK), F32),
                        pltpu.VMEM((ncb, kc, Q_BLOCK), BF16),
                        pltpu.VMEM((KV_WIDTH, N_HEADS * Q_BLOCK), BF16),
                        pltpu.VMEM((N_KV_HEADS, LANES, GROUP * Q_BLOCK), F32),
                        pltpu.VMEM((1, Q_BLOCK), jnp.int32),
                        pltpu.VMEM((kc, N_HEADS * Q_BLOCK), F32),
                        pltpu.VMEM((kc, N_HEADS * Q_BLOCK), F32)],
        compiler_params=_cparams(2),
        name="dsa_prompt",
    )(qT, qiT, wiT, kb3, kib3, vT3)


def _idx_sample_kernel(pt_ref, qi_ref, wcol_ref, kinew_ref, cki_hbm, past_ref, new_ref, kibuf, sem):
    n_pages = kibuf.shape[1]
    s_q = new_ref.shape[0]
    b = pl.program_id(0)
    slot = b % 2

    def page_copies(seq, dst_slot):
        return [pltpu.make_async_copy(cki_hbm.at[pt_ref[seq, r]], kibuf.at[dst_slot, r], sem.at[dst_slot])
                for r in range(n_pages)]

    @pl.when(b == 0)
    def _():
        for cp in page_copies(0, 0):
            cp.start()

    @pl.when(b + 1 < pl.num_programs(0))
    def _():
        for cp in page_copies(b + 1, 1 - slot):
            cp.start()

    qi = qi_ref[...]

    def scores(keys_t_bf16):
        s = jnp.dot(qi, keys_t_bf16, preferred_element_type=F32)
        wcol = jnp.concatenate([wcol_ref[...]] * (s.shape[1] // LANES), axis=1)
        s = jnp.maximum(s, 0.0) * wcol
        return jnp.sum(s.reshape(IDX_HEADS, s_q, s.shape[1]), axis=0)

    s_new = scores(kinew_ref[...])
    qrow = lax.broadcasted_iota(jnp.int32, s_new.shape, 0)
    kcol = lax.broadcasted_iota(jnp.int32, s_new.shape, 1)
    new_ref[...] = jnp.where(kcol <= qrow, s_new, NEG_INF)

    for cp in page_copies(b, slot):
        cp.wait()
    npg = PAGES_PER_STEP
    for r0 in range(0, n_pages, IDX_PAGES_PER_DOT):
        keys = jnp.concatenate([kibuf[slot, r0 + j].astype(BF16) for j in range(IDX_PAGES_PER_DOT)], axis=1)
        c, p = divmod(r0, npg)
        past_ref[c, :, p * PAGE_SIZE:(p + IDX_PAGES_PER_DOT) * PAGE_SIZE] = scores(keys)


def _idx_sample(page_table, qi_stack, wcol, kinew, cache_kidx, *, s_q):
    db, n_pages = page_table.shape
    npg = PAGES_PER_STEP
    n_steps = n_pages // npg
    rows = IDX_HEADS * s_q
    per_seq = lambda r, n: pl.BlockSpec((None, r, n), lambda b, pt: (b, 0, 0))
    grid_spec = pltpu.PrefetchScalarGridSpec(
        num_scalar_prefetch=1,
        grid=(db,),
        in_specs=[per_seq(rows, IDX_DIM), per_seq(rows, LANES), per_seq(IDX_DIM, PAGE_SIZE),
                  pl.BlockSpec(memory_space=pl.ANY)],
        out_specs=(pl.BlockSpec((n_steps, None, s_q, npg * PAGE_SIZE), lambda b, pt: (0, b, 0, 0)),
                   per_seq(s_q, PAGE_SIZE)),
        scratch_shapes=[pltpu.VMEM((2, n_pages, IDX_DIM, PAGE_SIZE), F32), pltpu.SemaphoreType.DMA((2,))],
    )
    return pl.pallas_call(
        _idx_sample_kernel,
        grid_spec=grid_spec,
        out_shape=(jax.ShapeDtypeStruct((n_steps, db, s_q, npg * PAGE_SIZE), F32),
                   jax.ShapeDtypeStruct((db, s_q, PAGE_SIZE), F32)),
        compiler_params=_cparams(1),
        name="idx_sample",
    )(page_table, qi_stack, wcol, kinew, cache_kidx)


def _sel_sample_kernel(past_ref, new_ref, thr_ref, j_ref, *, topk, n_pos_bits):
    n_c, g, s_q, w = past_ref.shape
    rows = g * s_q
    l_past = n_c * w
    lane_pos = lax.broadcasted_iota(jnp.int32, (rows, LANES), 1)

    def lane_tile_sum(v):
        return functools.reduce(lambda a, b: a + b, [v[:, t * LANES:(t + 1) * LANES] for t in range(v.shape[1] // LANES)])

    def count(pred):
        def body(cc, acc):
            x = past_ref[cc].reshape(rows, w)
            pos = cc * w + lax.broadcasted_iota(jnp.int32, (rows, w), 1)
            return acc + lane_tile_sum(jnp.where(pred(x, pos), 1.0, 0.0))
        acc = lax.fori_loop(0, n_c, body, jnp.zeros((rows, LANES), F32))
        acc = acc + jnp.where(pred(new_ref[...].reshape(rows, PAGE_SIZE), l_past + lane_pos), 1.0, 0.0)
        return jnp.sum(acc, axis=1, keepdims=True)

    def bit_body(t, res):
        trial = res | jnp.left_shift(jnp.int32(1), 31 - t)
        cand = _ukey_to_f32(trial)
        return jnp.where(count(lambda x, pos: x >= cand) >= topk, trial, res)

    res = lax.fori_loop(0, 32, bit_body, jnp.zeros((rows, 1), jnp.int32))
    thr = _ukey_to_f32(res)
    n_ties = topk - count(lambda x, pos: x > thr)
    cnt_ge = count(lambda x, pos: x >= thr)
    thr_ref[...] = jnp.broadcast_to(thr, (rows, LANES))
    j_ref[...] = jnp.full((rows, LANES), 2 ** 30, jnp.int32)

    @pl.when(jnp.max(jnp.where(cnt_ge > topk, 1.0, 0.0)) > 0.0)
    def _():
        def jbit_body(t, resj):
            trial = resj | jnp.left_shift(jnp.int32(1), n_pos_bits - 1 - t)
            below = count(lambda x, pos: (x == thr) & (pos < trial))
            return jnp.where(below < n_ties, trial, resj)
        jmax = lax.fori_loop(0, n_pos_bits, jbit_body, jnp.zeros((rows, 1), jnp.int32))
        j_ref[...] = jnp.broadcast_to(jmax, (rows, LANES))


def _sel_sample(sc_past, sc_new, *, topk):
    n_c, db, s_q, w = sc_past.shape
    g = math.gcd(SEL_SEQS, db)
    kern = functools.partial(_sel_sample_kernel, topk=topk,
                             n_pos_bits=max(1, math.ceil(math.log2(n_c * w + PAGE_SIZE))))
    return pl.pallas_call(
        kern,
        grid=(db // g,),
        in_specs=[pl.BlockSpec((n_c, g, s_q, w), lambda i: (0, i, 0, 0)),
                  pl.BlockSpec((g, s_q, PAGE_SIZE), lambda i: (i, 0, 0))],
        out_specs=(pl.BlockSpec((g * s_q, LANES), lambda i: (i, 0)),
                   pl.BlockSpec((g * s_q, LANES), lambda i: (i, 0))),
        out_shape=(jax.ShapeDtypeStruct((db * s_q, LANES), F32),
                   jax.ShapeDtypeStruct((db * s_q, LANES), jnp.int32)),
        compiler_params=_cparams(1),
        name="sel_sample",
    )(sc_past, sc_new)


def _att_sample_kernel(pt_ref, qbd_ref, sc_past_ref, sc_new_ref, thr_ref, j_ref, knew_ref, vnew_ref, ck_hbm, cv_hbm,
                       att_ref, kbuf, vbuf, ksem, vsem, acc_ref, m_ref, l_ref):
    npg = kbuf.shape[1]
    n_steps = sc_past_ref.shape[0]
    b = pl.program_id(0)
    qbd = qbd_ref[...]
    thr = thr_ref[...]
    jmax = j_ref[...]

    def page_copies(seq, step, slot):
        cps = []
        for r in range(npg):
            page = pt_ref[seq, step * npg + r]
            cps.append(pltpu.make_async_copy(ck_hbm.at[page], kbuf.at[slot, r], ksem.at[slot]))
            cps.append(pltpu.make_async_copy(cv_hbm.at[page], vbuf.at[slot, r], vsem.at[slot]))
        return cps

    @pl.when(b == 0)
    def _():
        for cp in page_copies(0, 0, 0):
            cp.start()

    acc_ref[...] = jnp.zeros(acc_ref.shape, F32)
    m_ref[...] = jnp.full(m_ref.shape, NEG_INF, F32)
    l_ref[...] = jnp.zeros(l_ref.shape, F32)

    def masked_scores(keys_t_bf16, sc, pos0):
        pos = pos0 + lax.broadcasted_iota(jnp.int32, sc.shape, 1)
        tile = lambda a: jnp.concatenate([a] * (sc.shape[1] // LANES), axis=1)
        sel = (sc > tile(thr)) | ((sc == tile(thr)) & (pos <= tile(jmax)))
        bias = jnp.where(sel, 0.0, NEG_INF)
        s = jnp.dot(qbd, keys_t_bf16, preferred_element_type=F32)
        return s + jnp.concatenate([bias] * N_HEADS, axis=0)

    def accumulate(s_list, vt_list):
        m_old = m_ref[...]
        m_new = jnp.maximum(m_old, jnp.max(functools.reduce(jnp.maximum, s_list), axis=1, keepdims=True))
        m_safe = jnp.maximum(m_new, -1e30)
        alpha = jnp.exp2(m_old - m_safe)
        acc = acc_ref[...] * alpha
        p_list = [jnp.exp2(s - m_safe) for s in s_list]
        for p, vt in zip(p_list, vt_list):
            acc = acc + lax.dot_general(p.astype(BF16), vt, NT_DIMS, preferred_element_type=F32)
        acc_ref[...] = acc
        l_ref[...] = l_ref[...] * alpha + jnp.sum(functools.reduce(lambda a, b: a + b, p_list), axis=1, keepdims=True)
        m_ref[...] = m_new

    for c in range(n_steps):
        slot = c % 2
        if c + 1 < n_steps:
            for cp in page_copies(b, c + 1, 1 - slot):
                cp.start()
        else:
            @pl.when(b + 1 < pl.num_programs(0))
            def _():
                for cp in page_copies(b + 1, 0, 1 - slot):
                    cp.start()
        for cp in page_copies(b, c, slot):
            cp.wait()
        s_list, v_list = [], []
        for p in range(0, npg, 2):
            sc = sc_past_ref[c, :, p * PAGE_SIZE:(p + 2) * PAGE_SIZE]
            keys = jnp.concatenate([kbuf[slot, p].astype(BF16), kbuf[slot, p + 1].astype(BF16)], axis=1)
            s_list.append(masked_scores(keys, sc, (c * npg + p) * PAGE_SIZE))
            v_list.append(jnp.concatenate([vbuf[slot, p].astype(BF16), vbuf[slot, p + 1].astype(BF16)], axis=1))
        accumulate(s_list, v_list)

    accumulate([masked_scores(knew_ref[...], sc_new_ref[...], n_steps * npg * PAGE_SIZE)], [vnew_ref[...]])
    att_ref[...] = acc_ref[...] / l_ref[...]


def _att_sample(page_table, qbd, sc_past, sc_new, thr, jmax, knew, vnew, cache_k, cache_v, *, s_q):
    db, n_pages = page_table.shape
    npg = PAGES_PER_STEP
    n_steps = n_pages // npg
    rows = N_HEADS * s_q
    assert n_steps % 2 == 0
    per_seq = lambda r, n: pl.BlockSpec((None, r, n), lambda b, pt: (b, 0, 0))
    page_buf = pltpu.VMEM((2, npg, KV_WIDTH, PAGE_SIZE), F32)
    grid_spec = pltpu.PrefetchScalarGridSpec(
        num_scalar_prefetch=1,
        grid=(db,),
        in_specs=[per_seq(rows, KV_WIDTH),
                  pl.BlockSpec((n_steps, None, s_q, npg * PAGE_SIZE), lambda b, pt: (0, b, 0, 0)),
                  per_seq(s_q, PAGE_SIZE),
                  pl.BlockSpec((s_q, LANES), lambda b, pt: (b, 0)),
                  pl.BlockSpec((s_q, LANES), lambda b, pt: (b, 0)),
                  per_seq(KV_WIDTH, PAGE_SIZE), per_seq(KV_WIDTH, PAGE_SIZE),
                  pl.BlockSpec(memory_space=pl.ANY), pl.BlockSpec(memory_space=pl.ANY)],
        out_specs=per_seq(rows, KV_WIDTH),
        scratch_shapes=[page_buf, page_buf, pltpu.SemaphoreType.DMA((2,)), pltpu.SemaphoreType.DMA((2,)),
                        pltpu.VMEM((rows, KV_WIDTH), F32), pltpu.VMEM((rows, 1), F32), pltpu.VMEM((rows, 1), F32)],
    )
    return pl.pallas_call(
        _att_sample_kernel,
        grid_spec=grid_spec,
        out_shape=jax.ShapeDtypeStruct((db, rows, KV_WIDTH), F32),
        compiler_params=_cparams(1),
        name="att_sample",
    )(page_table, qbd, sc_past, sc_new, thr, jmax, knew, vnew, cache_k, cache_v)


def _gla_kernel(*refs, chunk, n_chunks, off, t_end, has_s0):
    if has_s0:
        gq_ref, gk_ref, gv_ref, og_ref, la_ref, gn_ref, s0_ref, o_ref, sfin_ref, st_ref = refs
    else:
        gq_ref, gk_ref, gv_ref, og_ref, la_ref, gn_ref, o_ref, sfin_ref, st_ref = refs
    j = pl.program_id(1)
    tb = chunk * n_chunks

    @pl.when(j == 0)
    def _():
        for hh in range(GLA_HEADS):
            st_ref[hh] = s0_ref[hh].T if has_s0 else jnp.zeros((GLA_DV, GLA_DK), F32)

    r_i = lax.broadcasted_iota(jnp.int32, (chunk, chunk), 0)
    c_i = lax.broadcasted_iota(jnp.int32, (chunk, chunk), 1)
    causal = r_i >= c_i
    tril = jnp.where(causal, 1.0, 0.0)
    gn = gn_ref[...]

    def chunk_body(c, carry):
        r0 = pl.multiple_of(c * chunk, chunk)
        rows = pl.ds(r0, chunk)
        pos = j * tb + r0 + lax.broadcasted_iota(jnp.int32, (chunk, GLA_KW), 0)
        valid = (pos >= off) & (pos < t_end)
        la = jnp.where(valid, la_ref[rows, :], 0.0)
        k = jnp.where(valid, gk_ref[rows, :], 0.0)
        q = gq_ref[rows, :] * GLA_DK ** -0.5
        v = gv_ref[rows, :]
        b = jnp.dot(tril, la, preferred_element_type=F32, precision=lax.Precision.HIGHEST)
        b_last = b[chunk - 1:chunk, :]
        qd = (q * jnp.exp(b)).astype(BF16)
        kd = (k * jnp.exp(-b)).astype(BF16)
        ke = (k * jnp.exp(b_last - b)).astype(BF16)
        decay = jnp.exp(b_last)
        outs = []
        for hh in range(GLA_HEADS):
            ks = slice(hh * GLA_DK, (hh + 1) * GLA_DK)
            vh = v[:, hh * GLA_DV:(hh + 1) * GLA_DV]
            a = lax.dot_general(qd[:, ks], kd[:, ks], NT_DIMS, preferred_element_type=F32)
            a = jnp.where(causal, a, 0.0).astype(BF16)
            st = st_ref[hh]
            o = (jnp.dot(a, vh, preferred_element_type=F32)
                 + lax.dot_general(qd[:, ks], st.astype(BF16), NT_DIMS, preferred_element_type=F32))
            u_t = lax.dot_general(vh, ke[:, ks], TN_DIMS, preferred_element_type=F32)
            st_ref[hh] = decay[:, ks] * st + u_t
            o = o * lax.rsqrt(jnp.mean(o * o, axis=-1, keepdims=True) + LN_EPS) * gn
            outs.append(o)
        og = og_ref[rows, :]
        o_ref[rows, :] = (jnp.concatenate(outs, axis=1) * (og * jax.nn.sigmoid(og))).astype(BF16)
        return carry

    lax.fori_loop(0, n_chunks, chunk_body, 0)

    @pl.when(j == pl.num_programs(1) - 1)
    def _():
        for hh in range(GLA_HEADS):
            sfin_ref[hh] = st_ref[hh].T


def _gla(gq, gk, gv, og, la, gnorm, s0, *, batch, t_pad, tb, chunk, off, t_end):
    n_steps = t_pad // tb
    tok = lambda n: pl.BlockSpec((tb, n), lambda b, j: (b * n_steps + j, 0))
    state = pl.BlockSpec((None, GLA_HEADS, GLA_DK, GLA_DV), lambda b, j: (b, 0, 0, 0))
    has_s0 = s0 is not None
    kern = functools.partial(_gla_kernel, chunk=chunk, n_chunks=tb // chunk, off=off, t_end=t_end, has_s0=has_s0)
    in_specs = [tok(GLA_KW), tok(GLA_KW), tok(GLA_VW), tok(GLA_VW), tok(GLA_KW), _full_spec(gnorm.shape)]
    args = [gq, gk, gv, og, la, gnorm]
    if has_s0:
        in_specs.append(state)
        args.append(s0)
    return pl.pallas_call(
        kern,
        grid=(batch, n_steps),
        in_specs=in_specs,
        out_specs=(tok(GLA_VW), state),
        out_shape=(jax.ShapeDtypeStruct((batch * t_pad, GLA_VW), BF16),
                   jax.ShapeDtypeStruct((batch, GLA_HEADS, GLA_DK, GLA_DV), F32)),
        scratch_shapes=[pltpu.VMEM((GLA_HEADS, GLA_DV, GLA_DK), F32)],
        compiler_params=_cparams(2),
        name="gla",
    )(*args)


def _merge_kernel(x_ref, att_ref, gla_ref, gates_ref, lng_ref, lnb_ref, wa_ref, wg_ref, wo_ref,
                  l1g_ref, l1b_ref, h1_ref, *, alpha):
    h = _layer_norm(x_ref[...], lng_ref[...], lnb_ref[...])
    pa = jnp.dot(att_ref[...], wa_ref[...], preferred_element_type=F32)
    pg = jnp.dot(gla_ref[...], wg_ref[...], preferred_element_type=F32)
    gates = gates_ref[...].astype(F32)
    merged = gates[:, :D_MODEL] * pa + gates[:, D_MODEL:] * pg
    mix = jnp.dot(merged.astype(BF16), wo_ref[...], preferred_element_type=F32)
    h1_ref[...] = _layer_norm(alpha * h + mix, l1g_ref[...], l1b_ref[...])


def _merge(x, att, gla, gates, ln_g, ln_b, wa, wg, wo, l1g, l1b, *, alpha):
    n_tok = x.shape[0]
    tm = _row_tile(n_tok, TM_TOKEN)
    row = lambda n: pl.BlockSpec((tm, n), lambda i: (i, 0))
    consts = (ln_g, ln_b, wa, wg, wo, l1g, l1b)
    return pl.pallas_call(
        functools.partial(_merge_kernel, alpha=alpha),
        grid=(n_tok // tm,),
        in_specs=[row(D_MODEL), row(ATT_WIDTH), row(GLA_VW), row(N_BRANCHES * D_MODEL)]
                 + [_full_spec(a.shape) for a in consts],
        out_specs=row(D_MODEL),
        out_shape=jax.ShapeDtypeStruct((n_tok, D_MODEL), F32),
        compiler_params=_cparams(1),
        name="merge",
    )(x, att, gla, gates, *consts)


def _ffn_kernel(h_ref, wu_ref, wd_ref, g_ref, b_ref, y_ref, *, alpha, n_split):
    h = h_ref[...]
    hb = h.astype(BF16)
    w = D_FF // n_split
    ff = jnp.zeros(h.shape, F32)
    for s in range(n_split):
        u = jnp.dot(hb, wu_ref[:, s * w:(s + 1) * w], preferred_element_type=F32)
        u = jnp.square(jnp.maximum(u, 0.0)).astype(BF16)
        ff = ff + jnp.dot(u, wd_ref[s * w:(s + 1) * w, :], preferred_element_type=F32)
    y_ref[...] = _layer_norm(alpha * h + ff, g_ref[...], b_ref[...])


def _ffn(h1, wu, wd, g, b, *, alpha):
    n_tok = h1.shape[0]
    tm = _row_tile(n_tok, TM_TOKEN)
    row = pl.BlockSpec((tm, D_MODEL), lambda i: (i, 0))
    return pl.pallas_call(
        functools.partial(_ffn_kernel, alpha=alpha, n_split=4),
        grid=(n_tok // tm,),
        in_specs=[row] + [_full_spec(a.shape) for a in (wu, wd, g, b)],
        out_specs=row,
        out_shape=jax.ShapeDtypeStruct((n_tok, D_MODEL), F32),
        compiler_params=_cparams(1),
        name="ffn",
    )(h1, wu, wd, g, b)


def _ffn_window(h1, wu, wd, g, b, *, alpha, start, length):
    batch, rows, _ = h1.shape
    tm = _row_tile(length, TM_TOKEN)
    consts = (wu, wd, g, b)
    assert rows % SUBLANES == 0 and start % SUBLANES == 0
    first_row = lambda bi, j: pl.multiple_of(bi * rows + start + j * tm, SUBLANES)
    return pl.pallas_call(
        functools.partial(_ffn_kernel, alpha=alpha, n_split=4),
        grid=(batch, length // tm),
        in_specs=[pl.BlockSpec((pl.Element(tm), pl.Element(D_MODEL)), lambda bi, j: (first_row(bi, j), 0))]
                 + [pl.BlockSpec(a.shape, lambda bi, j, nd=a.ndim: (0,) * nd) for a in consts],
        out_specs=pl.BlockSpec((None, tm, D_MODEL), lambda bi, j: (bi, j, 0)),
        out_shape=jax.ShapeDtypeStruct((batch, length, D_MODEL), F32),
        compiler_params=_cparams(2),
        name="ffn_window",
    )(h1.reshape(batch * rows, D_MODEL), *consts)


def _pack_weights(w_in, w_gla_a2, b_gla_a):
    points = []
    acc = 0
    for s in IN_SIZES[:-1]:
        acc += s
        points.append(acc)
    wq, wk, wv, wqi, wki, wwi, wgq, wgk, wgv, wog, wa1, wgt = jnp.split(w_in, points, axis=-1)
    wq = wq * (math.log2(math.e) * HEAD_DIM ** -0.5)
    wqi = wqi * IDX_DIM ** -0.5
    pad_cols = lambda a, n: jnp.pad(a, ((0, 0), (0, n - a.shape[1])))
    wv_heads = wv.T.reshape(N_KV_HEADS, HEAD_DIM, D_MODEL)
    wv_aug = jnp.pad(wv_heads, ((0, 0), (0, LANES - HEAD_DIM), (0, 0))).reshape(N_KV_HEADS * LANES, D_MODEL)
    w = {
        "qT": wq.T, "qiT": wqi.T, "vT": wv_aug,
        "wiT": jnp.pad(wwi.T, ((0, 2 * SUBLANES - IDX_HEADS), (0, 0))),
        "row_p": pad_cols(jnp.concatenate([wk, wv, wki], axis=1), 2 * KV_WIDTH + LANES),
        "row_s": pad_cols(jnp.concatenate([wq, wqi, wk, wv, wki, wwi], axis=1),
                          ATT_WIDTH + IDX_HEADS * IDX_DIM + 2 * KV_WIDTH + LANES),
        "gla": jnp.concatenate([wgq, wgk, wgv, wog], axis=1),
        "a1": pad_cols(wa1, LANES),
        "a2": jnp.pad(w_gla_a2, ((0, LANES - GLA_GATE_RANK), (0, 0))),
        "gt": wgt,
    }
    w = {name: a.astype(BF16) for name, a in w.items()}
    w["ba"] = b_gla_a.astype(F32)[None]
    return w


def _round_up(x, m):
    return -(-x // m) * m


def _row_tile(n, pref, unit=2 * SUBLANES):
    best = unit
    for t in range(unit, min(n, pref) + 1, unit):
        if n % t == 0:
            best = t
    assert n % best == 0
    return best


def kernel(x_prompt, x_sample, cache_k, cache_v, cache_kidx, state_gla, page_table, meta_tokens, ln_in_g, ln_in_b, w_in, w_gla_a2, b_gla_a, gla_norm_g, w_proj_attn, w_proj_gla, w_out, ln1_g, ln1_b, w_ff_up, w_ff_down, ln2_g, ln2_b):
    depth = w_in.shape[0]
    assert depth == 1, "single-layer step only"
    B, S_p, D = x_prompt.shape
    DB, S_s, _ = x_sample.shape
    n_pages = page_table.shape[1]
    past = n_pages * PAGE_SIZE
    assert D == D_MODEL and S_p % GLA_CHUNK == 0 and S_s <= SUBLANES and n_pages % (2 * PAGES_PER_STEP) == 0
    topk_prompt = min(TOPK_MAX, S_p // 4)
    topk_sample = min(TOPK_MAX, (past + S_s) // 4)
    alpha = (2 * depth) ** 0.25

    row = lambda a: a.astype(F32).reshape(1, -1)
    ln_g, ln_b = row(ln_in_g), row(ln_in_b)
    w = _pack_weights(w_in[0], w_gla_a2[0], b_gla_a[0])
    wa, wg, wo = (a[0].astype(BF16) for a in (w_proj_attn, w_proj_gla, w_out))
    wu, wd = w_ff_up[0].astype(BF16), w_ff_down[0].astype(BF16)
    gnorm = row(gla_norm_g[0])
    l1g, l1b, l2g, l2b = row(ln1_g[0]), row(ln1_b[0]), row(ln2_g[0]), row(ln2_b[0])

    T = S_p + N_META
    off = (-N_META) % GLA_CHUNK
    t_pad = _round_up(off + T, math.lcm(KEY_CHUNK, GLA_CHUNK))
    gla_tb = _row_tile(t_pad, GLA_TOKENS_PER_STEP, GLA_CHUNK)
    n_tok = B * t_pad
    meta = jnp.broadcast_to(meta_tokens.astype(x_prompt.dtype)[None], (B, N_META, D))
    xp = jnp.concatenate([jnp.zeros((B, off, D), x_prompt.dtype), meta, x_prompt,
                          jnp.zeros((B, t_pad - off - T, D), x_prompt.dtype)], axis=1).reshape(n_tok, D)

    qT, qiT, vT3, wiT, k32, v32, ki32, kb, kib = _proj_attn_prompt(xp, ln_g, ln_b, w, n_tok)
    att_p = _dsa_prompt(qT, qiT, wiT,
                        kb.reshape(n_tok // KEY_CHUNK, KEY_CHUNK, KV_WIDTH),
                        kib.reshape(n_tok // KEY_CHUNK, KEY_CHUNK, IDX_DIM), vT3,
                        batch=B, t_pad=t_pad, topk=topk_prompt, off=off)
    gq, gk, gv, og, la, gates_p = _proj_gla(xp, ln_g, ln_b, w, n_tok)
    gla_p, state_p = _gla(gq, gk, gv, og, la, gnorm, None, batch=B, t_pad=t_pad, tb=gla_tb,
                          chunk=GLA_CHUNK, off=off, t_end=off + T)
    h1_p = _merge(xp, att_p, gla_p, gates_p, ln_g, ln_b, wa, wg, wo, l1g, l1b, alpha=alpha)
    seq = lambda a: a.reshape((B, t_pad) + a.shape[1:])
    y_prompt = _ffn_window(seq(h1_p), wu, wd, l2g, l2b, alpha=alpha, start=off + N_META, length=S_p)

    k_prompt = seq(k32)[:, off:off + T].reshape(1, B, T, N_KV_HEADS, HEAD_DIM)
    v_prompt = seq(v32)[:, off:off + T].reshape(1, B, T, N_KV_HEADS, HEAD_DIM)
    kidx_prompt = seq(ki32)[:, off:off + T][None]
    gla_state_prompt = state_p[None]

    R = SAMPLE_ROWS
    n_tok_s = DB * R
    xs = jnp.pad(x_sample, ((0, 0), (0, R - S_s), (0, 0))).reshape(n_tok_s, D)
    q_s, qi_s, k_s, v_s, kiwi_s = _proj_attn_sample(xs, ln_g, ln_b, w, n_tok_s)
    sseq = lambda a: a.reshape((DB, R) + a.shape[1:])[:, :S_s]
    k_new, v_new = sseq(k_s), sseq(v_s)
    ki_new = sseq(kiwi_s)[..., :IDX_DIM]
    wi_new = sseq(kiwi_s)[..., IDX_DIM:IDX_DIM + IDX_HEADS]
    qi_stack = sseq(qi_s).reshape(DB, S_s, IDX_HEADS, IDX_DIM).transpose(0, 2, 1, 3).reshape(DB, IDX_HEADS * S_s, IDX_DIM)
    wcol = jnp.broadcast_to(wi_new.transpose(0, 2, 1).reshape(DB, IDX_HEADS * S_s, 1), (DB, IDX_HEADS * S_s, LANES))
    q_heads = sseq(q_s).reshape(DB, S_s, N_HEADS, HEAD_DIM).transpose(0, 2, 1, 3)
    kv_of_head = (jnp.arange(N_HEADS) // GROUP)[:, None] == jnp.arange(N_KV_HEADS)[None, :]
    qbd = jnp.where(kv_of_head[None, :, None, :, None], q_heads[:, :, :, None, :], jnp.zeros((), BF16))
    qbd = qbd.reshape(DB, N_HEADS * S_s, KV_WIDTH)
    pad_page = lambda a: jnp.pad(a, ((0, 0), (0, PAGE_SIZE - S_s), (0, 0))).astype(BF16).transpose(0, 2, 1)
    kinew_pg, knew_pg, vnew_pg = pad_page(ki_new), pad_page(k_new), pad_page(v_new)
    ck = cache_k[0].transpose(0, 2, 3, 1).reshape(-1, KV_WIDTH, PAGE_SIZE)
    cv = cache_v[0].transpose(0, 2, 3, 1).reshape(-1, KV_WIDTH, PAGE_SIZE)
    cki = cache_kidx[0].transpose(0, 2, 1)
    sc_past, sc_new = _idx_sample(page_table, qi_stack, wcol, kinew_pg, cki, s_q=S_s)
    thr, jmax = _sel_sample(sc_past, sc_new, topk=topk_sample)
    o_s = _att_sample(page_table, qbd, sc_past, sc_new, thr, jmax, knew_pg, vnew_pg, ck, cv, s_q=S_s)
    o_s = o_s.reshape(DB, N_KV_HEADS, GROUP, S_s, N_KV_HEADS, HEAD_DIM)
    att_s = jnp.stack([o_s[:, n, :, :, n, :] for n in range(N_KV_HEADS)], axis=1)
    att_s = att_s.transpose(0, 3, 1, 2, 4).reshape(DB, S_s, ATT_WIDTH).astype(BF16)
    att_s = jnp.pad(att_s, ((0, 0), (0, R - S_s), (0, 0))).reshape(n_tok_s, ATT_WIDTH)

    gq, gk, gv, og, la, gates_s = _proj_gla(xs, ln_g, ln_b, w, n_tok_s)
    gla_s, state_s = _gla(gq, gk, gv, og, la, gnorm, state_gla[0], batch=DB, t_pad=R, tb=R,
                          chunk=R, off=0, t_end=S_s)
    h1_s = _merge(xs, att_s, gla_s, gates_s, ln_g, ln_b, wa, wg, wo, l1g, l1b, alpha=alpha)
    y_s = _ffn(h1_s, wu, wd, l2g, l2b, alpha=alpha)

    y_sample = y_s.reshape(DB, R, D)[:, :S_s]
    k_sample = k_new.reshape(1, DB, S_s, N_KV_HEADS, HEAD_DIM)
    v_sample = v_new.reshape(1, DB, S_s, N_KV_HEADS, HEAD_DIM)
    kidx_sample = ki_new[None]
    gla_state_sample = state_s[None]
    return (y_prompt, y_sample, k_prompt, v_prompt, kidx_prompt, gla_state_prompt,
            k_sample, v_sample, kidx_sample, gla_state_sample)
```

```python
import functools
import math

import jax
import jax.numpy as jnp
from jax import lax
from jax.experimental import pallas as pl
from jax.experimental.pallas import tpu as pltpu

D_MODEL = 1024
PAGE_SIZE = 128
N_META = 16
N_HEADS = 16
HEAD_DIM = 64
N_KV_HEADS = 4
GROUP = N_HEADS // N_KV_HEADS
ATT_WIDTH = N_HEADS * HEAD_DIM
KV_WIDTH = N_KV_HEADS * HEAD_DIM
IDX_HEADS = 8
IDX_DIM = 64
TOPK_MAX = 256
GLA_HEADS = 4
GLA_DK = D_MODEL // 2 // GLA_HEADS
GLA_DV = D_MODEL // GLA_HEADS
GLA_KW = GLA_HEADS * GLA_DK
GLA_VW = GLA_HEADS * GLA_DV
GLA_GATE_RANK = 16
GLA_TAU = 16.0
GLA_CHUNK = 64
N_BRANCHES = 2
D_FF = 4 * D_MODEL
LN_EPS = 1e-5
IN_SIZES = (ATT_WIDTH, KV_WIDTH, KV_WIDTH, IDX_HEADS * IDX_DIM, IDX_DIM, IDX_HEADS,
            GLA_KW, GLA_KW, GLA_VW, GLA_VW, GLA_GATE_RANK, N_BRANCHES * D_MODEL)

LANES = 128
SUBLANES = 8
VMEM_LIMIT_BYTES = 56 * 1024 * 1024
COUNT_LANES = 4
Q_BLOCK = LANES
KEY_CHUNK = 3 * LANES
SAMPLE_ROWS = 16
PAGES_PER_STEP = 16
IDX_PAGES_PER_DOT = 4
SEL_SEQS = 16
TM_PROJ_GLA = 256
TM_TOKEN = 512
GLA_TOKENS_PER_STEP = 6 * GLA_CHUNK

F32 = jnp.float32
BF16 = jnp.bfloat16
NEG_INF = float("-inf")
INT_MIN = -2 ** 31
NT_DIMS = (((1,), (1,)), ((), ()))
TN_DIMS = (((0,), (0,)), ((), ()))


def _cparams(n_grid):
    return pltpu.CompilerParams(dimension_semantics=("arbitrary",) * n_grid,
                                vmem_limit_bytes=VMEM_LIMIT_BYTES)


def _full_spec(shape):
    nd = len(shape)
    return pl.BlockSpec(shape, lambda *_: (0,) * nd)


def _layer_norm(x, g, b):
    mu = jnp.mean(x, axis=-1, keepdims=True)
    xc = x - mu
    var = jnp.mean(xc * xc, axis=-1, keepdims=True)
    return xc * lax.rsqrt(var + LN_EPS) * g + b


def _ukey_to_f32(u):
    bits = jnp.where(u < 0, u & jnp.int32(0x7FFFFFFF), ~u)
    return lax.bitcast_convert_type(bits, F32)


def _proj_attn_prompt_kernel(x_ref, g_ref, b_ref, wq_ref, wqi_ref, wv_ref, wwi_ref, wrow_ref,
                             qT_ref, qiT_ref, vT_ref, wiT_ref, k_ref, v_ref, ki_ref, kb_ref, kib_ref):
    hb = _layer_norm(x_ref[...], g_ref[...], b_ref[...]).astype(BF16)
    qT_ref[...] = lax.dot_general(wq_ref[...], hb, NT_DIMS, preferred_element_type=F32).astype(BF16)
    qiT_ref[...] = lax.dot_general(wqi_ref[...], hb, NT_DIMS, preferred_element_type=F32).astype(BF16)
    vt = lax.dot_general(wv_ref[...], hb, NT_DIMS, preferred_element_type=F32)
    row = lax.broadcasted_iota(jnp.int32, vt.shape, 0)
    vt = jnp.where(row % LANES == HEAD_DIM, 1.0, vt)
    vT_ref[0] = vt.astype(BF16)
    wi = lax.dot_general(wwi_ref[...], hb, NT_DIMS, preferred_element_type=F32)
    wiT_ref[...] = wi[:IDX_HEADS] * IDX_HEADS ** -0.5
    y = jnp.dot(hb, wrow_ref[...], preferred_element_type=F32)
    k = y[:, :KV_WIDTH]
    ki = y[:, 2 * KV_WIDTH:2 * KV_WIDTH + IDX_DIM]
    k_ref[...] = k
    v_ref[...] = y[:, KV_WIDTH:2 * KV_WIDTH]
    ki_ref[...] = ki
    kb_ref[...] = k.astype(BF16)
    kib_ref[...] = ki.astype(BF16)


def _proj_attn_prompt(x, ln_g, ln_b, w, n_tok):
    tm = KEY_CHUNK
    n_steps = n_tok // tm
    row = lambda n: pl.BlockSpec((tm, n), lambda i: (i, 0))
    col = lambda n: pl.BlockSpec((n, tm), lambda i: (0, i))
    out_shape = (
        jax.ShapeDtypeStruct((ATT_WIDTH, n_tok), BF16),
        jax.ShapeDtypeStruct((IDX_HEADS * IDX_DIM, n_tok), BF16),
        jax.ShapeDtypeStruct((n_steps, N_KV_HEADS * LANES, tm), BF16),
        jax.ShapeDtypeStruct((IDX_HEADS, n_tok), F32),
        jax.ShapeDtypeStruct((n_tok, KV_WIDTH), F32),
        jax.ShapeDtypeStruct((n_tok, KV_WIDTH), F32),
        jax.ShapeDtypeStruct((n_tok, IDX_DIM), F32),
        jax.ShapeDtypeStruct((n_tok, KV_WIDTH), BF16),
        jax.ShapeDtypeStruct((n_tok, IDX_DIM), BF16),
    )
    out_specs = (col(ATT_WIDTH), col(IDX_HEADS * IDX_DIM),
                 pl.BlockSpec((1, N_KV_HEADS * LANES, tm), lambda i: (i, 0, 0)),
                 col(IDX_HEADS), row(KV_WIDTH), row(KV_WIDTH), row(IDX_DIM), row(KV_WIDTH), row(IDX_DIM))
    ws = (w["qT"], w["qiT"], w["vT"], w["wiT"], w["row_p"])
    return pl.pallas_call(
        _proj_attn_prompt_kernel,
        grid=(n_steps,),
        in_specs=[row(D_MODEL), _full_spec(ln_g.shape), _full_spec(ln_b.shape)] + [_full_spec(a.shape) for a in ws],
        out_specs=out_specs,
        out_shape=out_shape,
        compiler_params=_cparams(1),
        name="proj_attn_prompt",
    )(x, ln_g, ln_b, *ws)


def _proj_attn_sample_kernel(x_ref, g_ref, b_ref, w_ref, scale_ref, q_ref, qi_ref, k_ref, v_ref, kiwi_ref):
    hb = _layer_norm(x_ref[...], g_ref[...], b_ref[...]).astype(BF16)
    y = jnp.dot(hb, w_ref[...], preferred_element_type=F32)
    o = 0
    q_ref[...] = y[:, o:o + ATT_WIDTH].astype(BF16)
    o += ATT_WIDTH
    qi_ref[...] = y[:, o:o + IDX_HEADS * IDX_DIM].astype(BF16)
    o += IDX_HEADS * IDX_DIM
    k_ref[...] = y[:, o:o + KV_WIDTH]
    o += KV_WIDTH
    v_ref[...] = y[:, o:o + KV_WIDTH]
    o += KV_WIDTH
    kiwi_ref[...] = y[:, o:o + LANES] * scale_ref[...]


def _proj_attn_sample(x, ln_g, ln_b, w, n_tok):
    tm = _row_tile(n_tok, TM_TOKEN)
    row = lambda n: pl.BlockSpec((tm, n), lambda i: (i, 0))
    lane = lax.iota(jnp.int32, LANES)
    scale = jnp.where((lane >= IDX_DIM) & (lane < IDX_DIM + IDX_HEADS), IDX_HEADS ** -0.5, 1.0).astype(F32)[None]
    out_shape = (
        jax.ShapeDtypeStruct((n_tok, ATT_WIDTH), BF16),
        jax.ShapeDtypeStruct((n_tok, IDX_HEADS * IDX_DIM), BF16),
        jax.ShapeDtypeStruct((n_tok, KV_WIDTH), F32),
        jax.ShapeDtypeStruct((n_tok, KV_WIDTH), F32),
        jax.ShapeDtypeStruct((n_tok, LANES), F32),
    )
    return pl.pallas_call(
        _proj_attn_sample_kernel,
        grid=(n_tok // tm,),
        in_specs=[row(D_MODEL), _full_spec(ln_g.shape), _full_spec(ln_b.shape),
                  _full_spec(w["row_s"].shape), _full_spec(scale.shape)],
        out_specs=(row(ATT_WIDTH), row(IDX_HEADS * IDX_DIM), row(KV_WIDTH), row(KV_WIDTH), row(LANES)),
        out_shape=out_shape,
        compiler_params=_cparams(1),
        name="proj_attn_sample",
    )(x, ln_g, ln_b, w["row_s"], scale)


def _log_sigmoid(x):
    return jnp.minimum(x, 0.0) - jnp.log1p(jnp.exp(-jnp.abs(x)))


def _proj_gla_kernel(x_ref, g_ref, b_ref, wg_ref, wa1_ref, wa2_ref, ba_ref, wgt_ref,
                     gq_ref, gk_ref, gv_ref, og_ref, la_ref, gates_ref):
    hb = _layer_norm(x_ref[...], g_ref[...], b_ref[...]).astype(BF16)
    y = jnp.dot(hb, wg_ref[...], preferred_element_type=F32)
    gq_ref[...] = y[:, :GLA_KW]
    gk_ref[...] = y[:, GLA_KW:2 * GLA_KW]
    gv_ref[...] = y[:, 2 * GLA_KW:2 * GLA_KW + GLA_VW].astype(BF16)
    og_ref[...] = y[:, 2 * GLA_KW + GLA_VW:]
    a1 = jnp.dot(hb, wa1_ref[...], preferred_element_type=F32).astype(BF16)
    z = jnp.dot(a1, wa2_ref[...], preferred_element_type=F32) + ba_ref[...]
    la_ref[...] = _log_sigmoid(z) / GLA_TAU
    gt = jnp.dot(hb, wgt_ref[...], preferred_element_type=F32)
    gates_ref[...] = jax.nn.sigmoid(gt).astype(BF16)


def _proj_gla(x, ln_g, ln_b, w, n_tok):
    tm = _row_tile(n_tok, TM_PROJ_GLA)
    row = lambda n: pl.BlockSpec((tm, n), lambda i: (i, 0))
    ws = (w["gla"], w["a1"], w["a2"], w["ba"], w["gt"])
    out_shape = (
        jax.ShapeDtypeStruct((n_tok, GLA_KW), F32),
        jax.ShapeDtypeStruct((n_tok, GLA_KW), F32),
        jax.ShapeDtypeStruct((n_tok, GLA_VW), BF16),
        jax.ShapeDtypeStruct((n_tok, GLA_VW), F32),
        jax.ShapeDtypeStruct((n_tok, GLA_KW), F32),
        jax.ShapeDtypeStruct((n_tok, N_BRANCHES * D_MODEL), BF16),
    )
    return pl.pallas_call(
        _proj_gla_kernel,
        grid=(n_tok // tm,),
        in_specs=[row(D_MODEL), _full_spec(ln_g.shape), _full_spec(ln_b.shape)] + [_full_spec(a.shape) for a in ws],
        out_specs=(row(GLA_KW), row(GLA_KW), row(GLA_VW), row(GLA_VW), row(GLA_KW), row(N_BRANCHES * D_MODEL)),
        out_shape=out_shape,
        compiler_params=_cparams(1),
        name="proj_gla",
    )(x, ln_g, ln_b, *ws)


def _dsa_prompt_kernel(qT_ref, qiT_ref, wiT_ref, kb_ref, kib_ref, vT_ref, att_ref,
                       sc_ref, qn_ref, acc_ref, j_ref, sa_ref, sb_ref, *, topk, off, n_pos_bits):
    kc = KEY_CHUNK
    i = pl.program_id(1)
    n_chunks = (i * Q_BLOCK + Q_BLOCK + kc - 1) // kc
    qpos = i * Q_BLOCK + lax.broadcasted_iota(jnp.int32, (1, Q_BLOCK), 1)

    def key_pos(c):
        return c * kc + lax.broadcasted_iota(jnp.int32, (kc, Q_BLOCK), 0)

    def run_indexer():
        wi = wiT_ref[...]

        def chunk_scores(c):
            kic = kib_ref[c]
            acc = jnp.zeros((kc, Q_BLOCK), F32)
            for hp in range(IDX_HEADS // 2):
                r0 = 2 * hp * IDX_DIM
                rhs = jnp.concatenate([qiT_ref[r0:r0 + IDX_DIM, :], qiT_ref[r0 + IDX_DIM:r0 + 2 * IDX_DIM, :]], axis=1)
                s = jnp.maximum(jnp.dot(kic, rhs, preferred_element_type=F32), 0.0)
                acc = acc + s[:, :Q_BLOCK] * wi[2 * hp:2 * hp + 1, :] + s[:, Q_BLOCK:] * wi[2 * hp + 1:2 * hp + 2, :]
            kp = key_pos(c)
            valid = (kp >= off) & (kp <= qpos)
            sc_ref[c] = jnp.where(valid, acc, NEG_INF)

        def idx_body(t, carry):
            chunk_scores(2 * t)
            chunk_scores(jnp.minimum(2 * t + 1, n_chunks - 1))
            return carry

        lax.fori_loop(0, (n_chunks + 1) // 2, idx_body, 0)

    def chunk_counts(pred_c, accs):
        v = jnp.where(pred_c, 1.0, 0.0).reshape(COUNT_LANES, kc // (SUBLANES * COUNT_LANES), SUBLANES, Q_BLOCK)
        return tuple(a + jnp.sum(v[r], axis=0) for r, a in enumerate(accs))

    zero_accs = (jnp.zeros((SUBLANES, Q_BLOCK), F32),) * COUNT_LANES

    def total(accs):
        return jnp.sum(functools.reduce(lambda a, b: a + b, accs), axis=0, keepdims=True)

    def count(pred):
        return total(lax.fori_loop(0, n_chunks, lambda c, accs: chunk_counts(pred(c), accs), zero_accs))

    def search():
        def bit_body(t, res):
            trial = res | jnp.left_shift(jnp.int32(1), 31 - t)
            cand = _ukey_to_f32(trial)
            cnt = count(lambda c: sc_ref[c] >= cand)
            return jnp.where(cnt >= topk, trial, res)
        return lax.fori_loop(0, 32, bit_body, jnp.zeros((1, Q_BLOCK), jnp.int32))

    def finish_selection(res):
        few = (qpos - off + 1) <= topk
        thr = jnp.where(few, NEG_INF, _ukey_to_f32(res))
        cnt_gt = count(lambda c: sc_ref[c] > thr)
        cnt_ge = count(lambda c: sc_ref[c] >= thr)
        n_ties = topk - cnt_gt
        tie_rows = jnp.where(few, 0.0, jnp.where(cnt_ge > topk, 1.0, 0.0))
        j_ref[...] = jnp.full(j_ref.shape, 2 ** 30, jnp.int32)

        @pl.when(jnp.max(tie_rows) > 0.0)
        def _():
            def jbit_body(t, resj):
                trial = resj | jnp.left_shift(jnp.int32(1), n_pos_bits - 1 - t)
                below = count(lambda c: (sc_ref[c] == thr) & (key_pos(c) < trial))
                return jnp.where(below < n_ties, trial, resj)
            j_ref[...] = lax.fori_loop(0, n_pos_bits, jbit_body, jnp.zeros((1, Q_BLOCK), jnp.int32))

        jmax = j_ref[...]

        def bias_body(c, carry):
            sc = sc_ref[c]
            kp = key_pos(c)
            valid = (kp >= off) & (kp <= qpos)
            sel = (sc > thr) | ((sc == thr) & (kp <= jmax))
            sc_ref[c] = jnp.where(valid & sel, 0.0, NEG_INF)
            return carry

        lax.fori_loop(0, n_chunks, bias_body, 0)

    gw = GROUP * Q_BLOCK
    last = n_chunks - 1
    m_none = jnp.full((1, N_HEADS * Q_BLOCK), NEG_INF, F32)

    def score_stage(c, buf_ref, m_run):
        bias = jnp.concatenate([sc_ref[c]] * N_HEADS, axis=1)
        s = jnp.dot(kb_ref[c], qn_ref[...], preferred_element_type=F32) + bias
        buf_ref[...] = s
        return jnp.maximum(m_run, jnp.max(s, axis=0, keepdims=True))

    def prob_stage(c, buf_ref, m_before, m_with, weight):
        m_safe = jnp.maximum(m_with, -1e30)
        alpha = jnp.exp2(m_before - m_safe)
        p = jnp.exp2(buf_ref[...] - m_safe).astype(BF16)
        vt = vT_ref[c]
        for n in range(N_KV_HEADS):
            cols = slice(n * gw, (n + 1) * gw)
            pv = jnp.dot(vt[n * LANES:(n + 1) * LANES, :], p[:, cols], preferred_element_type=F32)
            acc_ref[n] = acc_ref[n] * alpha[:, cols] + weight * pv

    def attention_start():
        qn_ref[...] = jnp.zeros(qn_ref.shape, BF16)
        for h in range(N_HEADS):
            n = h // GROUP
            qn_ref[n * HEAD_DIM:(n + 1) * HEAD_DIM, h * Q_BLOCK:(h + 1) * Q_BLOCK] = qT_ref[h * HEAD_DIM:(h + 1) * HEAD_DIM, :]
        acc_ref[...] = jnp.zeros(acc_ref.shape, F32)
        return m_none, score_stage(0, sa_ref, m_none)

    def pair_step(t, m_prev, m_cur):
        c0 = 2 * t
        on = lambda c: jnp.where(c <= last, 1.0, 0.0)
        clamp = lambda c: jnp.minimum(c, last)
        m_1 = score_stage(clamp(c0 + 1), sb_ref, m_cur)
        prob_stage(clamp(c0), sa_ref, m_prev, m_cur, on(c0))
        m_2 = score_stage(clamp(c0 + 2), sa_ref, m_1)
        prob_stage(clamp(c0 + 1), sb_ref, m_cur, m_1, on(c0 + 1))
        return m_1, m_2

    def attention_finish():
        heads = []
        for n in range(N_KV_HEADS):
            a = acc_ref[n]
            denom = a[HEAD_DIM:HEAD_DIM + 1, :]
            o = a[:HEAD_DIM, :] / jnp.where(denom > 0.0, denom, 1.0)
            for g in range(GROUP):
                heads.append(o[:, g * Q_BLOCK:(g + 1) * Q_BLOCK])
        att_ref[...] = jnp.concatenate(heads, axis=0).T.astype(BF16)

    run_indexer()
    finish_selection(search())
    lax.fori_loop(0, (n_chunks + 1) // 2, lambda t, ms: pair_step(t, *ms), attention_start())
    attention_finish()


def _dsa_prompt(qT, qiT, wiT, kb3, kib3, vT3, *, batch, t_pad, topk, off):
    kc = KEY_CHUNK
    nqb = t_pad // Q_BLOCK
    ncb = t_pad // kc
    n_tok = batch * t_pad
    qcol = lambda n: pl.BlockSpec((n, Q_BLOCK), lambda b, i: (0, b * nqb + i))
    kern = functools.partial(_dsa_prompt_kernel, topk=topk, off=off,
                             n_pos_bits=max(1, math.ceil(math.log2(t_pad))))
    return pl.pallas_call(
        kern,
        grid=(batch, nqb),
        in_specs=[qcol(ATT_WIDTH), qcol(IDX_HEADS * IDX_DIM), qcol(IDX_HEADS),
                  pl.BlockSpec((ncb, kc, KV_WIDTH), lambda b, i: (b, 0, 0)),
                  pl.BlockSpec((ncb, kc, IDX_DIM), lambda b, i: (b, 0, 0)),
                  pl.BlockSpec((ncb, N_KV_HEADS * LANES, kc), lambda b, i: (b, 0, 0))],
        out_specs=pl.BlockSpec((Q_BLOCK, ATT_WIDTH), lambda b, i: (b * nqb + i, 0)),
        out_shape=jax.ShapeDtypeStruct((n_tok, ATT_WIDTH), BF16),
        scratch_shapes=[pltpu.VMEM((ncb, kc, Q_BLOCK), F32),
                        pltpu.VMEM((KV_WIDTH, N_HEADS * Q_BLOCK), BF16),
                        pltpu.VMEM((N_KV_HEADS, LANES, GROUP * Q_BLOCK), F32),
                        pltpu.VMEM((1, Q_BLOCK), jnp.int32),
                        pltpu.VMEM((kc, N_HEADS * Q_BLOCK), F32),
                        pltpu.VMEM((kc, N_HEADS * Q_BLOCK), F32)],
        compiler_params=_cparams(2),
        name="dsa_prompt",
    )(qT, qiT, wiT, kb3, kib3, vT3)


def _idx_sample_kernel(pt_ref, qi_ref, wcol_ref, kinew_ref, cki_hbm, past_ref, new_ref, kibuf, sem):
    n_pages = kibuf.shape[1]
    s_q = new_ref.shape[0]
    b = pl.program_id(0)
    slot = b % 2

    def page_copies(seq, dst_slot):
        return [pltpu.make_async_copy(cki_hbm.at[pt_ref[seq, r]], kibuf.at[dst_slot, r], sem.at[dst_slot])
                for r in range(n_pages)]

    @pl.when(b == 0)
    def _():
        for cp in page_copies(0, 0):
            cp.start()

    @pl.when(b + 1 < pl.num_programs(0))
    def _():
        for cp in page_copies(b + 1, 1 - slot):
            cp.start()

    qi = qi_ref[...]

    def scores(keys_t_bf16):
        s = jnp.dot(qi, keys_t_bf16, preferred_element_type=F32)
        wcol = jnp.concatenate([wcol_ref[...]] * (s.shape[1] // LANES), axis=1)
        s = jnp.maximum(s, 0.0) * wcol
        return jnp.sum(s.reshape(IDX_HEADS, s_q, s.shape[1]), axis=0)

    s_new = scores(kinew_ref[...])
    qrow = lax.broadcasted_iota(jnp.int32, s_new.shape, 0)
    kcol = lax.broadcasted_iota(jnp.int32, s_new.shape, 1)
    new_ref[...] = jnp.where(kcol <= qrow, s_new, NEG_INF)

    for cp in page_copies(b, slot):
        cp.wait()
    npg = PAGES_PER_STEP
    for r0 in range(0, n_pages, IDX_PAGES_PER_DOT):
        keys = jnp.concatenate([kibuf[slot, r0 + j].astype(BF16) for j in range(IDX_PAGES_PER_DOT)], axis=1)
        c, p = divmod(r0, npg)
        past_ref[c, :, p * PAGE_SIZE:(p + IDX_PAGES_PER_DOT) * PAGE_SIZE] = scores(keys)


def _idx_sample(page_table, qi_stack, wcol, kinew, cache_kidx, *, s_q):
    db, n_pages = page_table.shape
    npg = PAGES_PER_STEP
    n_steps = n_pages // npg
    rows = IDX_HEADS * s_q
    per_seq = lambda r, n: pl.BlockSpec((None, r, n), lambda b, pt: (b, 0, 0))
    grid_spec = pltpu.PrefetchScalarGridSpec(
        num_scalar_prefetch=1,
        grid=(db,),
        in_specs=[per_seq(rows, IDX_DIM), per_seq(rows, LANES), per_seq(IDX_DIM, PAGE_SIZE),
                  pl.BlockSpec(memory_space=pl.ANY)],
        out_specs=(pl.BlockSpec((n_steps, None, s_q, npg * PAGE_SIZE), lambda b, pt: (0, b, 0, 0)),
                   per_seq(s_q, PAGE_SIZE)),
        scratch_shapes=[pltpu.VMEM((2, n_pages, IDX_DIM, PAGE_SIZE), F32), pltpu.SemaphoreType.DMA((2,))],
    )
    return pl.pallas_call(
        _idx_sample_kernel,
        grid_spec=grid_spec,
        out_shape=(jax.ShapeDtypeStruct((n_steps, db, s_q, npg * PAGE_SIZE), F32),
                   jax.ShapeDtypeStruct((db, s_q, PAGE_SIZE), F32)),
        compiler_params=_cparams(1),
        name="idx_sample",
    )(page_table, qi_stack, wcol, kinew, cache_kidx)


def _sel_sample_kernel(past_ref, new_ref, thr_ref, j_ref, *, topk, n_pos_bits):
    n_c, g, s_q, w = past_ref.shape
    rows = g * s_q
    l_past = n_c * w
    lane_pos = lax.broadcasted_iota(jnp.int32, (rows, LANES), 1)

    def lane_tile_sum(v):
        return functools.reduce(lambda a, b: a + b, [v[:, t * LANES:(t + 1) * LANES] for t in range(v.shape[1] // LANES)])

    def count(pred):
        def body(cc, acc):
            x = past_ref[cc].reshape(rows, w)
            pos = cc * w + lax.broadcasted_iota(jnp.int32, (rows, w), 1)
            return acc + lane_tile_sum(jnp.where(pred(x, pos), 1.0, 0.0))
        acc = lax.fori_loop(0, n_c, body, jnp.zeros((rows, LANES), F32))
        acc = acc + jnp.where(pred(new_ref[...].reshape(rows, PAGE_SIZE), l_past + lane_pos), 1.0, 0.0)
        return jnp.sum(acc, axis=1, keepdims=True)

    def bit_body(t, res):
        trial = res | jnp.left_shift(jnp.int32(1), 31 - t)
        cand = _ukey_to_f32(trial)
        return jnp.where(count(lambda x, pos: x >= cand) >= topk, trial, res)

    res = lax.fori_loop(0, 32, bit_body, jnp.zeros((rows, 1), jnp.int32))
    thr = _ukey_to_f32(res)
    n_ties = topk - count(lambda x, pos: x > thr)
    cnt_ge = count(lambda x, pos: x >= thr)
    thr_ref[...] = jnp.broadcast_to(thr, (rows, LANES))
    j_ref[...] = jnp.full((rows, LANES), 2 ** 30, jnp.int32)

    @pl.when(jnp.max(jnp.where(cnt_ge > topk, 1.0, 0.0)) > 0.0)
    def _():
        def jbit_body(t, resj):
            trial = resj | jnp.left_shift(jnp.int32(1), n_pos_bits - 1 - t)
            below = count(lambda x, pos: (x == thr) & (pos < trial))
            return jnp.where(below < n_ties, trial, resj)
        jmax = lax.fori_loop(0, n_pos_bits, jbit_body, jnp.zeros((rows, 1), jnp.int32))
        j_ref[...] = jnp.broadcast_to(jmax, (rows, LANES))


def _sel_sample(sc_past, sc_new, *, topk):
    n_c, db, s_q, w = sc_past.shape
    g = math.gcd(SEL_SEQS, db)
    kern = functools.partial(_sel_sample_kernel, topk=topk,
                             n_pos_bits=max(1, math.ceil(math.log2(n_c * w + PAGE_SIZE))))
    return pl.pallas_call(
        kern,
        grid=(db // g,),
        in_specs=[pl.BlockSpec((n_c, g, s_q, w), lambda i: (0, i, 0, 0)),
                  pl.BlockSpec((g, s_q, PAGE_SIZE), lambda i: (i, 0, 0))],
        out_specs=(pl.BlockSpec((g * s_q, LANES), lambda i: (i, 0)),
                   pl.BlockSpec((g * s_q, LANES), lambda i: (i, 0))),
        out_shape=(jax.ShapeDtypeStruct((db * s_q, LANES), F32),
                   jax.ShapeDtypeStruct((db * s_q, LANES), jnp.int32)),
        compiler_params=_cparams(1),
        name="sel_sample",
    )(sc_past, sc_new)


def _att_sample_kernel(pt_ref, qbd_ref, sc_past_ref, sc_new_ref, thr_ref, j_ref, knew_ref, vnew_ref, ck_hbm, cv_hbm,
                       att_ref, kbuf, vbuf, ksem, vsem, acc_ref, m_ref, l_ref):
    npg = kbuf.shape[1]
    n_steps = sc_past_ref.shape[0]
    b = pl.program_id(0)
    qbd = qbd_ref[...]
    thr = thr_ref[...]
    jmax = j_ref[...]

    def page_copies(seq, step, slot):
        cps = []
        for r in range(npg):
            page = pt_ref[seq, step * npg + r]
            cps.append(pltpu.make_async_copy(ck_hbm.at[page], kbuf.at[slot, r], ksem.at[slot]))
            cps.append(pltpu.make_async_copy(cv_hbm.at[page], vbuf.at[slot, r], vsem.at[slot]))
        return cps

    @pl.when(b == 0)
    def _():
        for cp in page_copies(0, 0, 0):
            cp.start()

    acc_ref[...] = jnp.zeros(acc_ref.shape, F32)
    m_ref[...] = jnp.full(m_ref.shape, NEG_INF, F32)
    l_ref[...] = jnp.zeros(l_ref.shape, F32)

    def masked_scores(keys_t_bf16, sc, pos0):
        pos = pos0 + lax.broadcasted_iota(jnp.int32, sc.shape, 1)
        tile = lambda a: jnp.concatenate([a] * (sc.shape[1] // LANES), axis=1)
        sel = (sc > tile(thr)) | ((sc == tile(thr)) & (pos <= tile(jmax)))
        bias = jnp.where(sel, 0.0, NEG_INF)
        s = jnp.dot(qbd, keys_t_bf16, preferred_element_type=F32)
        return s + jnp.concatenate([bias] * N_HEADS, axis=0)

    def accumulate(s_list, vt_list):
        m_old = m_ref[...]
        m_new = jnp.maximum(m_old, jnp.max(functools.reduce(jnp.maximum, s_list), axis=1, keepdims=True))
        m_safe = jnp.maximum(m_new, -1e30)
        alpha = jnp.exp2(m_old - m_safe)
        acc = acc_ref[...] * alpha
        p_list = [jnp.exp2(s - m_safe) for s in s_list]
        for p, vt in zip(p_list, vt_list):
            acc = acc + lax.dot_general(p.astype(BF16), vt, NT_DIMS, preferred_element_type=F32)
        acc_ref[...] = acc
        l_ref[...] = l_ref[...] * alpha + jnp.sum(functools.reduce(lambda a, b: a + b, p_list), axis=1, keepdims=True)
        m_ref[...] = m_new

    for c in range(n_steps):
        slot = c % 2
        if c + 1 < n_steps:
            for cp in page_copies(b, c + 1, 1 - slot):
                cp.start()
        else:
            @pl.when(b + 1 < pl.num_programs(0))
            def _():
                for cp in page_copies(b + 1, 0, 1 - slot):
                    cp.start()
        for cp in page_copies(b, c, slot):
            cp.wait()
        s_list, v_list = [], []
        for p in range(0, npg, 2):
            sc = sc_past_ref[c, :, p * PAGE_SIZE:(p + 2) * PAGE_SIZE]
            keys = jnp.concatenate([kbuf[slot, p].astype(BF16), kbuf[slot, p + 1].astype(BF16)], axis=1)
            s_list.append(masked_scores(keys, sc, (c * npg + p) * PAGE_SIZE))
            v_list.append(jnp.concatenate([vbuf[slot, p].astype(BF16), vbuf[slot, p + 1].astype(BF16)], axis=1))
        accumulate(s_list, v_list)

    accumulate([masked_scores(knew_ref[...], sc_new_ref[...], n_steps * npg * PAGE_SIZE)], [vnew_ref[...]])
    att_ref[...] = acc_ref[...] / l_ref[...]


def _att_sample(page_table, qbd, sc_past, sc_new, thr, jmax, knew, vnew, cache_k, cache_v, *, s_q):
    db, n_pages = page_table.shape
    npg = PAGES_PER_STEP
    n_steps = n_pages // npg
    rows = N_HEADS * s_q
    assert n_steps % 2 == 0
    per_seq = lambda r, n: pl.BlockSpec((None, r, n), lambda b, pt: (b, 0, 0))
    page_buf = pltpu.VMEM((2, npg, KV_WIDTH, PAGE_SIZE), F32)
    grid_spec = pltpu.PrefetchScalarGridSpec(
        num_scalar_prefetch=1,
        grid=(db,),
        in_specs=[per_seq(rows, KV_WIDTH),
                  pl.BlockSpec((n_steps, None, s_q, npg * PAGE_SIZE), lambda b, pt: (0, b, 0, 0)),
                  per_seq(s_q, PAGE_SIZE),
                  pl.BlockSpec((s_q, LANES), lambda b, pt: (b, 0)),
                  pl.BlockSpec((s_q, LANES), lambda b, pt: (b, 0)),
                  per_seq(KV_WIDTH, PAGE_SIZE), per_seq(KV_WIDTH, PAGE_SIZE),
                  pl.BlockSpec(memory_space=pl.ANY), pl.BlockSpec(memory_space=pl.ANY)],
        out_specs=per_seq(rows, KV_WIDTH),
        scratch_shapes=[page_buf, page_buf, pltpu.SemaphoreType.DMA((2,)), pltpu.SemaphoreType.DMA((2,)),
                        pltpu.VMEM((rows, KV_WIDTH), F32), pltpu.VMEM((rows, 1), F32), pltpu.VMEM((rows, 1), F32)],
    )
    return pl.pallas_call(
        _att_sample_kernel,
        grid_spec=grid_spec,
        out_shape=jax.ShapeDtypeStruct((db, rows, KV_WIDTH), F32),
        compiler_params=_cparams(1),
        name="att_sample",
    )(page_table, qbd, sc_past, sc_new, thr, jmax, knew, vnew, cache_k, cache_v)


def _gla_kernel(*refs, chunk, n_chunks, off, t_end, has_s0):
    if has_s0:
        gq_ref, gk_ref, gv_ref, og_ref, la_ref, gn_ref, s0_ref, o_ref, sfin_ref, st_ref = refs
    else:
        gq_ref, gk_ref, gv_ref, og_ref, la_ref, gn_ref, o_ref, sfin_ref, st_ref = refs
    j = pl.program_id(1)
    tb = chunk * n_chunks

    @pl.when(j == 0)
    def _():
        for hh in range(GLA_HEADS):
            st_ref[hh] = s0_ref[hh].T if has_s0 else jnp.zeros((GLA_DV, GLA_DK), F32)

    r_i = lax.broadcasted_iota(jnp.int32, (chunk, chunk), 0)
    c_i = lax.broadcasted_iota(jnp.int32, (chunk, chunk), 1)
    causal = r_i >= c_i
    tril = jnp.where(causal, 1.0, 0.0)
    gn = gn_ref[...]

    def chunk_body(c, carry):
        r0 = pl.multiple_of(c * chunk, chunk)
        rows = pl.ds(r0, chunk)
        pos = j * tb + r0 + lax.broadcasted_iota(jnp.int32, (chunk, GLA_KW), 0)
        valid = (pos >= off) & (pos < t_end)
        la = jnp.where(valid, la_ref[rows, :], 0.0)
        k = jnp.where(valid, gk_ref[rows, :], 0.0)
        q = gq_ref[rows, :] * GLA_DK ** -0.5
        v = gv_ref[rows, :]
        b = jnp.dot(tril, la, preferred_element_type=F32, precision=lax.Precision.HIGHEST)
        b_last = b[chunk - 1:chunk, :]
        qd = (q * jnp.exp(b)).astype(BF16)
        kd = (k * jnp.exp(-b)).astype(BF16)
        ke = (k * jnp.exp(b_last - b)).astype(BF16)
        decay = jnp.exp(b_last)
        outs = []
        for hh in range(GLA_HEADS):
            ks = slice(hh * GLA_DK, (hh + 1) * GLA_DK)
            vh = v[:, hh * GLA_DV:(hh + 1) * GLA_DV]
            a = lax.dot_general(qd[:, ks], kd[:, ks], NT_DIMS, preferred_element_type=F32)
            a = jnp.where(causal, a, 0.0).astype(BF16)
            st = st_ref[hh]
            o = (jnp.dot(a, vh, preferred_element_type=F32)
                 + lax.dot_general(qd[:, ks], st.astype(BF16), NT_DIMS, preferred_element_type=F32))
            u_t = lax.dot_general(vh, ke[:, ks], TN_DIMS, preferred_element_type=F32)
            st_ref[hh] = decay[:, ks] * st + u_t
            o = o * lax.rsqrt(jnp.mean(o * o, axis=-1, keepdims=True) + LN_EPS) * gn
            outs.append(o)
        og = og_ref[rows, :]
        o_ref[rows, :] = (jnp.concatenate(outs, axis=1) * (og * jax.nn.sigmoid(og))).astype(BF16)
        return carry

    per_step = next(k for k in (6, 3, 2, 1) if n_chunks % k == 0)

    def step_body(t, carry):
        for k in range(per_step):
            chunk_body(per_step * t + k, carry)
        return carry

    lax.fori_loop(0, n_chunks // per_step, step_body, 0)

    @pl.when(j == pl.num_programs(1) - 1)
    def _():
        for hh in range(GLA_HEADS):
            sfin_ref[hh] = st_ref[hh].T


def _gla(gq, gk, gv, og, la, gnorm, s0, *, batch, t_pad, tb, chunk, off, t_end):
    n_steps = t_pad // tb
    tok = lambda n: pl.BlockSpec((tb, n), lambda b, j: (b * n_steps + j, 0))
    state = pl.BlockSpec((None, GLA_HEADS, GLA_DK, GLA_DV), lambda b, j: (b, 0, 0, 0))
    has_s0 = s0 is not None
    kern = functools.partial(_gla_kernel, chunk=chunk, n_chunks=tb // chunk, off=off, t_end=t_end, has_s0=has_s0)
    in_specs = [tok(GLA_KW), tok(GLA_KW), tok(GLA_VW), tok(GLA_VW), tok(GLA_KW), _full_spec(gnorm.shape)]
    args = [gq, gk, gv, og, la, gnorm]
    if has_s0:
        in_specs.append(state)
        args.append(s0)
    return pl.pallas_call(
        kern,
        grid=(batch, n_steps),
        in_specs=in_specs,
        out_specs=(tok(GLA_VW), state),
        out_shape=(jax.ShapeDtypeStruct((batch * t_pad, GLA_VW), BF16),
                   jax.ShapeDtypeStruct((batch, GLA_HEADS, GLA_DK, GLA_DV), F32)),
        scratch_shapes=[pltpu.VMEM((GLA_HEADS, GLA_DV, GLA_DK), F32)],
        compiler_params=_cparams(2),
        name="gla",
    )(*args)


def _merge_kernel(x_ref, att_ref, gla_ref, gates_ref, lng_ref, lnb_ref, wa_ref, wg_ref, wo_ref,
                  l1g_ref, l1b_ref, h1_ref, *, alpha):
    h = _layer_norm(x_ref[...], lng_ref[...], lnb_ref[...])
    pa = jnp.dot(att_ref[...], wa_ref[...], preferred_element_type=F32)
    pg = jnp.dot(gla_ref[...], wg_ref[...], preferred_element_type=F32)
    gates = gates_ref[...].astype(F32)
    merged = gates[:, :D_MODEL] * pa + gates[:, D_MODEL:] * pg
    mix = jnp.dot(merged.astype(BF16), wo_ref[...], preferred_element_type=F32)
    h1_ref[...] = _layer_norm(alpha * h + mix, l1g_ref[...], l1b_ref[...])


def _merge(x, att, gla, gates, ln_g, ln_b, wa, wg, wo, l1g, l1b, *, alpha):
    n_tok = x.shape[0]
    tm = _row_tile(n_tok, TM_TOKEN)
    row = lambda n: pl.BlockSpec((tm, n), lambda i: (i, 0))
    consts = (ln_g, ln_b, wa, wg, wo, l1g, l1b)
    return pl.pallas_call(
        functools.partial(_merge_kernel, alpha=alpha),
        grid=(n_tok // tm,),
        in_specs=[row(D_MODEL), row(ATT_WIDTH), row(GLA_VW), row(N_BRANCHES * D_MODEL)]
                 + [_full_spec(a.shape) for a in consts],
        out_specs=row(D_MODEL),
        out_shape=jax.ShapeDtypeStruct((n_tok, D_MODEL), F32),
        compiler_params=_cparams(1),
        name="merge",
    )(x, att, gla, gates, *consts)


def _ffn_kernel(h_ref, wu_ref, wd_ref, g_ref, b_ref, y_ref, *, alpha, n_split):
    h = h_ref[...]
    hb = h.astype(BF16)
    w = D_FF // n_split
    ff = jnp.zeros(h.shape, F32)
    for s in range(n_split):
        u = jnp.dot(hb, wu_ref[:, s * w:(s + 1) * w], preferred_element_type=F32)
        u = jnp.square(jnp.maximum(u, 0.0)).astype(BF16)
        ff = ff + jnp.dot(u, wd_ref[s * w:(s + 1) * w, :], preferred_element_type=F32)
    y_ref[...] = _layer_norm(alpha * h + ff, g_ref[...], b_ref[...])


def _ffn(h1, wu, wd, g, b, *, alpha):
    n_tok = h1.shape[0]
    tm = _row_tile(n_tok, TM_TOKEN)
    row = pl.BlockSpec((tm, D_MODEL), lambda i: (i, 0))
    return pl.pallas_call(
        functools.partial(_ffn_kernel, alpha=alpha, n_split=4),
        grid=(n_tok // tm,),
        in_specs=[row] + [_full_spec(a.shape) for a in (wu, wd, g, b)],
        out_specs=row,
        out_shape=jax.ShapeDtypeStruct((n_tok, D_MODEL), F32),
        compiler_params=_cparams(1),
        name="ffn",
    )(h1, wu, wd, g, b)


def _ffn_window(h1, wu, wd, g, b, *, alpha, start, length):
    batch, rows, _ = h1.shape
    tm = _row_tile(length, TM_TOKEN)
    consts = (wu, wd, g, b)
    assert rows % SUBLANES == 0 and start % SUBLANES == 0
    first_row = lambda bi, j: pl.multiple_of(bi * rows + start + j * tm, SUBLANES)
    return pl.pallas_call(
        functools.partial(_ffn_kernel, alpha=alpha, n_split=4),
        grid=(batch, length // tm),
        in_specs=[pl.BlockSpec((pl.Element(tm), pl.Element(D_MODEL)), lambda bi, j: (first_row(bi, j), 0))]
                 + [pl.BlockSpec(a.shape, lambda bi, j, nd=a.ndim: (0,) * nd) for a in consts],
        out_specs=pl.BlockSpec((None, tm, D_MODEL), lambda bi, j: (bi, j, 0)),
        out_shape=jax.ShapeDtypeStruct((batch, length, D_MODEL), F32),
        compiler_params=_cparams(2),
        name="ffn_window",
    )(h1.reshape(batch * rows, D_MODEL), *consts)


def _pack_weights(w_in, w_gla_a2, b_gla_a):
    points = []
    acc = 0
    for s in IN_SIZES[:-1]:
        acc += s
        points.append(acc)
    wq, wk, wv, wqi, wki, wwi, wgq, wgk, wgv, wog, wa1, wgt = jnp.split(w_in, points, axis=-1)
    wq = wq * (math.log2(math.e) * HEAD_DIM ** -0.5)
    wqi = wqi * IDX_DIM ** -0.5
    pad_cols = lambda a, n: jnp.pad(a, ((0, 0), (0, n - a.shape[1])))
    wv_heads = wv.T.reshape(N_KV_HEADS, HEAD_DIM, D_MODEL)
    wv_aug = jnp.pad(wv_heads, ((0, 0), (0, LANES - HEAD_DIM), (0, 0))).reshape(N_KV_HEADS * LANES, D_MODEL)
    w = {
        "qT": wq.T, "qiT": wqi.T, "vT": wv_aug,
        "wiT": jnp.pad(wwi.T, ((0, 2 * SUBLANES - IDX_HEADS), (0, 0))),
        "row_p": pad_cols(jnp.concatenate([wk, wv, wki], axis=1), 2 * KV_WIDTH + LANES),
        "row_s": pad_cols(jnp.concatenate([wq, wqi, wk, wv, wki, wwi], axis=1),
                          ATT_WIDTH + IDX_HEADS * IDX_DIM + 2 * KV_WIDTH + LANES),
        "gla": jnp.concatenate([wgq, wgk, wgv, wog], axis=1),
        "a1": pad_cols(wa1, LANES),
        "a2": jnp.pad(w_gla_a2, ((0, LANES - GLA_GATE_RANK), (0, 0))),
        "gt": wgt,
    }
    w = {name: a.astype(BF16) for name, a in w.items()}
    w["ba"] = b_gla_a.astype(F32)[None]
    return w


def _round_up(x, m):
    return -(-x // m) * m


def _row_tile(n, pref, unit=2 * SUBLANES):
    best = unit
    for t in range(unit, min(n, pref) + 1, unit):
        if n % t == 0:
            best = t
    assert n % best == 0
    return best


def kernel(x_prompt, x_sample, cache_k, cache_v, cache_kidx, state_gla, page_table, meta_tokens, ln_in_g, ln_in_b, w_in, w_gla_a2, b_gla_a, gla_norm_g, w_proj_attn, w_proj_gla, w_out, ln1_g, ln1_b, w_ff_up, w_ff_down, ln2_g, ln2_b):
    depth = w_in.shape[0]
    assert depth == 1, "single-layer step only"
    B, S_p, D = x_prompt.shape
    DB, S_s, _ = x_sample.shape
    n_pages = page_table.shape[1]
    past = n_pages * PAGE_SIZE
    assert D == D_MODEL and S_p % GLA_CHUNK == 0 and S_s <= SUBLANES and n_pages % (2 * PAGES_PER_STEP) == 0
    topk_prompt = min(TOPK_MAX, S_p // 4)
    topk_sample = min(TOPK_MAX, (past + S_s) // 4)
    alpha = (2 * depth) ** 0.25

    row = lambda a: a.astype(F32).reshape(1, -1)
    ln_g, ln_b = row(ln_in_g), row(ln_in_b)
    w = _pack_weights(w_in[0], w_gla_a2[0], b_gla_a[0])
    wa, wg, wo = (a[0].astype(BF16) for a in (w_proj_attn, w_proj_gla, w_out))
    wu, wd = w_ff_up[0].astype(BF16), w_ff_down[0].astype(BF16)
    gnorm = row(gla_norm_g[0])
    l1g, l1b, l2g, l2b = row(ln1_g[0]), row(ln1_b[0]), row(ln2_g[0]), row(ln2_b[0])

    T = S_p + N_META
    off = (-N_META) % GLA_CHUNK
    t_pad = _round_up(off + T, math.lcm(KEY_CHUNK, GLA_CHUNK))
    gla_tb = _row_tile(t_pad, GLA_TOKENS_PER_STEP, GLA_CHUNK)
    n_tok = B * t_pad
    meta = jnp.broadcast_to(meta_tokens.astype(x_prompt.dtype)[None], (B, N_META, D))
    xp = jnp.concatenate([jnp.zeros((B, off, D), x_prompt.dtype), meta, x_prompt,
                          jnp.zeros((B, t_pad - off - T, D), x_prompt.dtype)], axis=1).reshape(n_tok, D)

    qT, qiT, vT3, wiT, k32, v32, ki32, kb, kib = _proj_attn_prompt(xp, ln_g, ln_b, w, n_tok)
    att_p = _dsa_prompt(qT, qiT, wiT,
                        kb.reshape(n_tok // KEY_CHUNK, KEY_CHUNK, KV_WIDTH),
                        kib.reshape(n_tok // KEY_CHUNK, KEY_CHUNK, IDX_DIM), vT3,
                        batch=B, t_pad=t_pad, topk=topk_prompt, off=off)
    gq, gk, gv, og, la, gates_p = _proj_gla(xp, ln_g, ln_b, w, n_tok)
    gla_p, state_p = _gla(gq, gk, gv, og, la, gnorm, None, batch=B, t_pad=t_pad, tb=gla_tb,
                          chunk=GLA_CHUNK, off=off, t_end=off + T)
    h1_p = _merge(xp, att_p, gla_p, gates_p, ln_g, ln_b, wa, wg, wo, l1g, l1b, alpha=alpha)
    seq = lambda a: a.reshape((B, t_pad) + a.shape[1:])
    y_prompt = _ffn_window(seq(h1_p), wu, wd, l2g, l2b, alpha=alpha, start=off + N_META, length=S_p)

    k_prompt = seq(k32)[:, off:off + T].reshape(1, B, T, N_KV_HEADS, HEAD_DIM)
    v_prompt = seq(v32)[:, off:off + T].reshape(1, B, T, N_KV_HEADS, HEAD_DIM)
    kidx_prompt = seq(ki32)[:, off:off + T][None]
    gla_state_prompt = state_p[None]

    R = SAMPLE_ROWS
    n_tok_s = DB * R
    xs = jnp.pad(x_sample, ((0, 0), (0, R - S_s), (0, 0))).reshape(n_tok_s, D)
    q_s, qi_s, k_s, v_s, kiwi_s = _proj_attn_sample(xs, ln_g, ln_b, w, n_tok_s)
    sseq = lambda a: a.reshape((DB, R) + a.shape[1:])[:, :S_s]
    k_new, v_new = sseq(k_s), sseq(v_s)
    ki_new = sseq(kiwi_s)[..., :IDX_DIM]
    wi_new = sseq(kiwi_s)[..., IDX_DIM:IDX_DIM + IDX_HEADS]
    qi_stack = sseq(qi_s).reshape(DB, S_s, IDX_HEADS, IDX_DIM).transpose(0, 2, 1, 3).reshape(DB, IDX_HEADS * S_s, IDX_DIM)
    wcol = jnp.broadcast_to(wi_new.transpose(0, 2, 1).reshape(DB, IDX_HEADS * S_s, 1), (DB, IDX_HEADS * S_s, LANES))
    q_heads = sseq(q_s).reshape(DB, S_s, N_HEADS, HEAD_DIM).transpose(0, 2, 1, 3)
    kv_of_head = (jnp.arange(N_HEADS) // GROUP)[:, None] == jnp.arange(N_KV_HEADS)[None, :]
    qbd = jnp.where(kv_of_head[None, :, None, :, None], q_heads[:, :, :, None, :], jnp.zeros((), BF16))
    qbd = qbd.reshape(DB, N_HEADS * S_s, KV_WIDTH)
    pad_page = lambda a: jnp.pad(a, ((0, 0), (0, PAGE_SIZE - S_s), (0, 0))).astype(BF16).transpose(0, 2, 1)
    kinew_pg, knew_pg, vnew_pg = pad_page(ki_new), pad_page(k_new), pad_page(v_new)
    ck = cache_k[0].transpose(0, 2, 3, 1).reshape(-1, KV_WIDTH, PAGE_SIZE)
    cv = cache_v[0].transpose(0, 2, 3, 1).reshape(-1, KV_WIDTH, PAGE_SIZE)
    cki = cache_kidx[0].transpose(0, 2, 1)
    sc_past, sc_new = _idx_sample(page_table, qi_stack, wcol, kinew_pg, cki, s_q=S_s)
    thr, jmax = _sel_sample(sc_past, sc_new, topk=topk_sample)
    o_s = _att_sample(page_table, qbd, sc_past, sc_new, thr, jmax, knew_pg, vnew_pg, ck, cv, s_q=S_s)
    o_s = o_s.reshape(DB, N_KV_HEADS, GROUP, S_s, N_KV_HEADS, HEAD_DIM)
    att_s = jnp.stack([o_s[:, n, :, :, n, :] for n in range(N_KV_HEADS)], axis=1)
    att_s = att_s.transpose(0, 3, 1, 2, 4).reshape(DB, S_s, ATT_WIDTH).astype(BF16)
    att_s = jnp.pad(att_s, ((0, 0), (0, R - S_s), (0, 0))).reshape(n_tok_s, ATT_WIDTH)

    gq, gk, gv, og, la, gates_s = _proj_gla(xs, ln_g, ln_b, w, n_tok_s)
    gla_s, state_s = _gla(gq, gk, gv, og, la, gnorm, state_gla[0], batch=DB, t_pad=R, tb=R,
                          chunk=R, off=0, t_end=S_s)
    h1_s = _merge(xs, att_s, gla_s, gates_s, ln_g, ln_b, wa, wg, wo, l1g, l1b, alpha=alpha)
    y_s = _ffn(h1_s, wu, wd, l2g, l2b, alpha=alpha)

    y_sample = y_s.reshape(DB, R, D)[:, :S_s]
    k_sample = k_new.reshape(1, DB, S_s, N_KV_HEADS, HEAD_DIM)
    v_sample = v_new.reshape(1, DB, S_s, N_KV_HEADS, HEAD_DIM)
    kidx_sample = ki_new[None]
    gla_state_sample = state_s[None]
    return (y_prompt, y_sample, k_prompt, v_prompt, kidx_prompt, gla_state_prompt,
            k_sample, v_sample, kidx_sample, gla_state_sample)
```

```python
import functools
import math

import jax
import jax.numpy as jnp
from jax import lax
from jax.experimental import pallas as pl
from jax.experimental.pallas import tpu as pltpu

D_MODEL = 1024
PAGE_SIZE = 128
N_META = 16
N_HEADS = 16
HEAD_DIM = 64
N_KV_HEADS = 4
GROUP = N_HEADS // N_KV_HEADS
ATT_WIDTH = N_HEADS * HEAD_DIM
KV_WIDTH = N_KV_HEADS * HEAD_DIM
IDX_HEADS = 8
IDX_DIM = 64
TOPK_MAX = 256
GLA_HEADS = 4
GLA_DK = D_MODEL // 2 // GLA_HEADS
GLA_DV = D_MODEL // GLA_HEADS
GLA_KW = GLA_HEADS * GLA_DK
GLA_VW = GLA_HEADS * GLA_DV
GLA_GATE_RANK = 16
GLA_TAU = 16.0
GLA_CHUNK = 64
N_BRANCHES = 2
D_FF = 4 * D_MODEL
LN_EPS = 1e-5
IN_SIZES = (ATT_WIDTH, KV_WIDTH, KV_WIDTH, IDX_HEADS * IDX_DIM, IDX_DIM, IDX_HEADS,
            GLA_KW, GLA_KW, GLA_VW, GLA_VW, GLA_GATE_RANK, N_BRANCHES * D_MODEL)

LANES = 128
SUBLANES = 8
VMEM_LIMIT_BYTES = 56 * 1024 * 1024
COUNT_LANES = 4
Q_BLOCK = LANES
KEY_CHUNK = 3 * LANES
SAMPLE_ROWS = 16
PAGES_PER_STEP = 16
PAGE_RING = 3
IDX_PAGES_PER_DOT = 4
SEL_SEQS = 16
TM_PROJ_GLA = 512
PROJ_ATTN_CHUNKS = 2
TM_TOKEN = 512
GLA_TOKENS_PER_STEP = 6 * GLA_CHUNK

F32 = jnp.float32
BF16 = jnp.bfloat16
NEG_INF = float("-inf")
INT_MIN = -2 ** 31
NT_DIMS = (((1,), (1,)), ((), ()))
TN_DIMS = (((0,), (0,)), ((), ()))


def _cparams(n_grid):
    return pltpu.CompilerParams(dimension_semantics=("arbitrary",) * n_grid,
                                vmem_limit_bytes=VMEM_LIMIT_BYTES)


def _full_spec(shape):
    nd = len(shape)
    return pl.BlockSpec(shape, lambda *_: (0,) * nd, pipeline_mode=pl.Buffered(1))


def _layer_norm(x, g, b):
    mu = jnp.mean(x, axis=-1, keepdims=True)
    xc = x - mu
    var = jnp.mean(xc * xc, axis=-1, keepdims=True)
    return xc * lax.rsqrt(var + LN_EPS) * g + b


def _ukey_to_f32(u):
    bits = jnp.where(u < 0, u & jnp.int32(0x7FFFFFFF), ~u)
    return lax.bitcast_convert_type(bits, F32)


def _proj_attn_prompt_kernel(x_ref, g_ref, b_ref, wq_ref, wqi_ref, wv_ref, wwi_ref, wrow_ref,
                             qT_ref, qiT_ref, vT_ref, wiT_ref, k_ref, v_ref, ki_ref, kb_ref, kib_ref):
    hb = _layer_norm(x_ref[...], g_ref[...], b_ref[...]).astype(BF16)
    qT_ref[...] = lax.dot_general(wq_ref[...], hb, NT_DIMS, preferred_element_type=F32).astype(BF16)
    qiT_ref[...] = lax.dot_general(wqi_ref[...], hb, NT_DIMS, preferred_element_type=F32).astype(BF16)
    vt = lax.dot_general(wv_ref[...], hb, NT_DIMS, preferred_element_type=F32)
    row = lax.broadcasted_iota(jnp.int32, vt.shape, 0)
    vt = jnp.where(row % LANES == HEAD_DIM, 1.0, vt)
    for j in range(vT_ref.shape[0]):
        vT_ref[j] = vt[:, j * KEY_CHUNK:(j + 1) * KEY_CHUNK].astype(BF16)
    wi = lax.dot_general(wwi_ref[...], hb, NT_DIMS, preferred_element_type=F32)
    wiT_ref[...] = wi[:IDX_HEADS] * IDX_HEADS ** -0.5
    y = jnp.dot(hb, wrow_ref[...], preferred_element_type=F32)
    k = y[:, :KV_WIDTH]
    ki = y[:, 2 * KV_WIDTH:2 * KV_WIDTH + IDX_DIM]
    k_ref[...] = k
    v_ref[...] = y[:, KV_WIDTH:2 * KV_WIDTH]
    ki_ref[...] = ki
    kb_ref[...] = k.astype(BF16)
    kib_ref[...] = ki.astype(BF16)


def _proj_attn_prompt(x, ln_g, ln_b, w, n_tok):
    tm = _row_tile(n_tok, PROJ_ATTN_CHUNKS * KEY_CHUNK, KEY_CHUNK)
    n_steps = n_tok // tm
    cps = tm // KEY_CHUNK
    row = lambda n: pl.BlockSpec((tm, n), lambda i: (i, 0))
    col = lambda n: pl.BlockSpec((n, tm), lambda i: (0, i))
    out_shape = (
        jax.ShapeDtypeStruct((ATT_WIDTH, n_tok), BF16),
        jax.ShapeDtypeStruct((IDX_HEADS * IDX_DIM, n_tok), BF16),
        jax.ShapeDtypeStruct((n_tok // KEY_CHUNK, N_KV_HEADS * LANES, KEY_CHUNK), BF16),
        jax.ShapeDtypeStruct((IDX_HEADS, n_tok), F32),
        jax.ShapeDtypeStruct((n_tok, KV_WIDTH), F32),
        jax.ShapeDtypeStruct((n_tok, KV_WIDTH), F32),
        jax.ShapeDtypeStruct((n_tok, IDX_DIM), F32),
        jax.ShapeDtypeStruct((n_tok, KV_WIDTH), BF16),
        jax.ShapeDtypeStruct((n_tok, IDX_DIM), BF16),
    )
    out_specs = (col(ATT_WIDTH), col(IDX_HEADS * IDX_DIM),
                 pl.BlockSpec((cps, N_KV_HEADS * LANES, KEY_CHUNK), lambda i: (i, 0, 0)),
                 col(IDX_HEADS), row(KV_WIDTH), row(KV_WIDTH), row(IDX_DIM), row(KV_WIDTH), row(IDX_DIM))
    ws = (w["qT"], w["qiT"], w["vT"], w["wiT"], w["row_p"])
    return pl.pallas_call(
        _proj_attn_prompt_kernel,
        grid=(n_steps,),
        in_specs=[row(D_MODEL), _full_spec(ln_g.shape), _full_spec(ln_b.shape)] + [_full_spec(a.shape) for a in ws],
        out_specs=out_specs,
        out_shape=out_shape,
        compiler_params=_cparams(1),
        name="proj_attn_prompt",
    )(x, ln_g, ln_b, *ws)


def _proj_attn_sample_kernel(x_ref, g_ref, b_ref, w_ref, scale_ref, q_ref, qi_ref, k_ref, v_ref, kiwi_ref):
    hb = _layer_norm(x_ref[...], g_ref[...], b_ref[...]).astype(BF16)
    y = jnp.dot(hb, w_ref[...], preferred_element_type=F32)
    o = 0
    q_ref[...] = y[:, o:o + ATT_WIDTH].astype(BF16)
    o += ATT_WIDTH
    qi_ref[...] = y[:, o:o + IDX_HEADS * IDX_DIM].astype(BF16)
    o += IDX_HEADS * IDX_DIM
    k_ref[...] = y[:, o:o + KV_WIDTH]
    o += KV_WIDTH
    v_ref[...] = y[:, o:o + KV_WIDTH]
    o += KV_WIDTH
    kiwi_ref[...] = y[:, o:o + LANES] * scale_ref[...]


def _proj_attn_sample(x, ln_g, ln_b, w, n_tok):
    tm = _row_tile(n_tok, TM_TOKEN)
    row = lambda n: pl.BlockSpec((tm, n), lambda i: (i, 0))
    lane = lax.iota(jnp.int32, LANES)
    scale = jnp.where((lane >= IDX_DIM) & (lane < IDX_DIM + IDX_HEADS), IDX_HEADS ** -0.5, 1.0).astype(F32)[None]
    out_shape = (
        jax.ShapeDtypeStruct((n_tok, ATT_WIDTH), BF16),
        jax.ShapeDtypeStruct((n_tok, IDX_HEADS * IDX_DIM), BF16),
        jax.ShapeDtypeStruct((n_tok, KV_WIDTH), F32),
        jax.ShapeDtypeStruct((n_tok, KV_WIDTH), F32),
        jax.ShapeDtypeStruct((n_tok, LANES), F32),
    )
    return pl.pallas_call(
        _proj_attn_sample_kernel,
        grid=(n_tok // tm,),
        in_specs=[row(D_MODEL), _full_spec(ln_g.shape), _full_spec(ln_b.shape),
                  _full_spec(w["row_s"].shape), _full_spec(scale.shape)],
        out_specs=(row(ATT_WIDTH), row(IDX_HEADS * IDX_DIM), row(KV_WIDTH), row(KV_WIDTH), row(LANES)),
        out_shape=out_shape,
        compiler_params=_cparams(1),
        name="proj_attn_sample",
    )(x, ln_g, ln_b, w["row_s"], scale)


def _log_sigmoid(x):
    return jnp.minimum(x, 0.0) - jnp.log1p(jnp.exp(-jnp.abs(x)))


def _proj_gla_kernel(x_ref, g_ref, b_ref, wg_ref, wa1_ref, wa2_ref, ba_ref, wgt_ref,
                     gq_ref, gk_ref, gv_ref, og_ref, la_ref, gates_ref):
    hb = _layer_norm(x_ref[...], g_ref[...], b_ref[...]).astype(BF16)
    y = jnp.dot(hb, wg_ref[...], preferred_element_type=F32)
    gq_ref[...] = y[:, :GLA_KW]
    gk_ref[...] = y[:, GLA_KW:2 * GLA_KW]
    gv_ref[...] = y[:, 2 * GLA_KW:2 * GLA_KW + GLA_VW].astype(BF16)
    og_ref[...] = y[:, 2 * GLA_KW + GLA_VW:]
    a1 = jnp.dot(hb, wa1_ref[...], preferred_element_type=F32).astype(BF16)
    z = jnp.dot(a1, wa2_ref[...], preferred_element_type=F32) + ba_ref[...]
    la_ref[...] = _log_sigmoid(z) / GLA_TAU
    gt = jnp.dot(hb, wgt_ref[...], preferred_element_type=F32)
    gates_ref[...] = jax.nn.sigmoid(gt).astype(BF16)


def _proj_gla(x, ln_g, ln_b, w, n_tok):
    tm = _row_tile(n_tok, TM_PROJ_GLA)
    row = lambda n: pl.BlockSpec((tm, n), lambda i: (i, 0))
    ws = (w["gla"], w["a1"], w["a2"], w["ba"], w["gt"])
    out_shape = (
        jax.ShapeDtypeStruct((n_tok, GLA_KW), F32),
        jax.ShapeDtypeStruct((n_tok, GLA_KW), F32),
        jax.ShapeDtypeStruct((n_tok, GLA_VW), BF16),
        jax.ShapeDtypeStruct((n_tok, GLA_VW), F32),
        jax.ShapeDtypeStruct((n_tok, GLA_KW), F32),
        jax.ShapeDtypeStruct((n_tok, N_BRANCHES * D_MODEL), BF16),
    )
    return pl.pallas_call(
        _proj_gla_kernel,
        grid=(n_tok // tm,),
        in_specs=[row(D_MODEL), _full_spec(ln_g.shape), _full_spec(ln_b.shape)] + [_full_spec(a.shape) for a in ws],
        out_specs=(row(GLA_KW), row(GLA_KW), row(GLA_VW), row(GLA_VW), row(GLA_KW), row(N_BRANCHES * D_MODEL)),
        out_shape=out_shape,
        compiler_params=_cparams(1),
        name="proj_gla",
    )(x, ln_g, ln_b, *ws)


def _dsa_prompt_kernel(qT_ref, qiT_ref, wiT_ref, kb_ref, kib_ref, vT_ref, att_ref,
                       sc_ref, qn_ref, acc_ref, j_ref, sa_ref, sb_ref, *, topk, off, n_pos_bits):
    kc = KEY_CHUNK
    i = pl.program_id(1)
    n_chunks = (i * Q_BLOCK + Q_BLOCK + kc - 1) // kc
    qpos = i * Q_BLOCK + lax.broadcasted_iota(jnp.int32, (1, Q_BLOCK), 1)

    def key_pos(c):
        return c * kc + lax.broadcasted_iota(jnp.int32, (kc, Q_BLOCK), 0)

    def run_indexer():
        wi = wiT_ref[...]

        def chunk_scores(c):
            kic = kib_ref[c]
            acc = jnp.zeros((kc, Q_BLOCK), F32)
            for hp in range(IDX_HEADS // 2):
                r0 = 2 * hp * IDX_DIM
                rhs = jnp.concatenate([qiT_ref[r0:r0 + IDX_DIM, :], qiT_ref[r0 + IDX_DIM:r0 + 2 * IDX_DIM, :]], axis=1)
                s = jnp.maximum(jnp.dot(kic, rhs, preferred_element_type=F32), 0.0)
                acc = acc + s[:, :Q_BLOCK] * wi[2 * hp:2 * hp + 1, :] + s[:, Q_BLOCK:] * wi[2 * hp + 1:2 * hp + 2, :]
            kp = key_pos(c)
            valid = (kp >= off) & (kp <= qpos)
            sc_ref[c] = jnp.where(valid, acc, NEG_INF)

        def idx_body(t, carry):
            chunk_scores(2 * t)
            chunk_scores(jnp.minimum(2 * t + 1, n_chunks - 1))
            return carry

        lax.fori_loop(0, (n_chunks + 1) // 2, idx_body, 0)

    def chunk_counts(pred_c, accs):
        v = jnp.where(pred_c, 1.0, 0.0).reshape(COUNT_LANES, kc // (SUBLANES * COUNT_LANES), SUBLANES, Q_BLOCK)
        return tuple(a + jnp.sum(v[r], axis=0) for r, a in enumerate(accs))

    zero_accs = (jnp.zeros((SUBLANES, Q_BLOCK), F32),) * COUNT_LANES

    def total(accs):
        return jnp.sum(functools.reduce(lambda a, b: a + b, accs), axis=0, keepdims=True)

    def count(pred):
        return total(lax.fori_loop(0, n_chunks, lambda c, accs: chunk_counts(pred(c), accs), zero_accs))

    def search():
        def bit_body(t, res):
            trial = res | jnp.left_shift(jnp.int32(1), 31 - t)
            cand = _ukey_to_f32(trial)
            cnt = count(lambda c: sc_ref[c] >= cand)
            return jnp.where(cnt >= topk, trial, res)
        return lax.fori_loop(0, 32, bit_body, jnp.zeros((1, Q_BLOCK), jnp.int32))

    def finish_selection(res):
        few = (qpos - off + 1) <= topk
        thr = jnp.where(few, NEG_INF, _ukey_to_f32(res))
        cnt_gt = count(lambda c: sc_ref[c] > thr)
        cnt_ge = count(lambda c: sc_ref[c] >= thr)
        n_ties = topk - cnt_gt
        tie_rows = jnp.where(few, 0.0, jnp.where(cnt_ge > topk, 1.0, 0.0))
        j_ref[...] = jnp.full(j_ref.shape, 2 ** 30, jnp.int32)

        @pl.when(jnp.max(tie_rows) > 0.0)
        def _():
            def jbit_body(t, resj):
                trial = resj | jnp.left_shift(jnp.int32(1), n_pos_bits - 1 - t)
                below = count(lambda c: (sc_ref[c] == thr) & (key_pos(c) < trial))
                return jnp.where(below < n_ties, trial, resj)
            j_ref[...] = lax.fori_loop(0, n_pos_bits, jbit_body, jnp.zeros((1, Q_BLOCK), jnp.int32))

        jmax = j_ref[...]

        def bias_body(c, carry):
            sc = sc_ref[c]
            kp = key_pos(c)
            valid = (kp >= off) & (kp <= qpos)
            sel = (sc > thr) | ((sc == thr) & (kp <= jmax))
            sc_ref[c] = jnp.where(valid & sel, 0.0, NEG_INF)
            return carry

        lax.fori_loop(0, n_chunks, bias_body, 0)

    gw = GROUP * Q_BLOCK
    last = n_chunks - 1
    m_none = jnp.full((1, N_HEADS * Q_BLOCK), NEG_INF, F32)

    def score_stage(c, buf_ref, m_run):
        bias = jnp.concatenate([sc_ref[c]] * N_HEADS, axis=1)
        s = jnp.dot(kb_ref[c], qn_ref[...], preferred_element_type=F32) + bias
        buf_ref[...] = s
        return jnp.maximum(m_run, jnp.max(s, axis=0, keepdims=True))

    def prob_stage(c, buf_ref, m_before, m_with, weight):
        m_safe = jnp.maximum(m_with, -1e30)
        alpha = jnp.exp2(m_before - m_safe)
        p = jnp.exp2(buf_ref[...] - m_safe).astype(BF16)
        vt = vT_ref[c]
        for n in range(N_KV_HEADS):
            cols = slice(n * gw, (n + 1) * gw)
            pv = jnp.dot(vt[n * LANES:(n + 1) * LANES, :], p[:, cols], preferred_element_type=F32)
            acc_ref[n] = acc_ref[n] * alpha[:, cols] + weight * pv

    def attention_start():
        qn_ref[...] = jnp.zeros(qn_ref.shape, BF16)
        for h in range(N_HEADS):
            n = h // GROUP
            qn_ref[n * HEAD_DIM:(n + 1) * HEAD_DIM, h * Q_BLOCK:(h + 1) * Q_BLOCK] = qT_ref[h * HEAD_DIM:(h + 1) * HEAD_DIM, :]
        acc_ref[...] = jnp.zeros(acc_ref.shape, F32)
        return m_none, score_stage(0, sa_ref, m_none)

    def pair_step(t, m_prev, m_cur):
        c0 = 2 * t
        on = lambda c: jnp.where(c <= last, 1.0, 0.0)
        clamp = lambda c: jnp.minimum(c, last)
        m_1 = score_stage(clamp(c0 + 1), sb_ref, m_cur)
        prob_stage(clamp(c0), sa_ref, m_prev, m_cur, on(c0))
        m_2 = score_stage(clamp(c0 + 2), sa_ref, m_1)
        prob_stage(clamp(c0 + 1), sb_ref, m_cur, m_1, on(c0 + 1))
        return m_1, m_2

    def attention_finish():
        heads = []
        for n in range(N_KV_HEADS):
            a = acc_ref[n]
            denom = a[HEAD_DIM:HEAD_DIM + 1, :]
            o = a[:HEAD_DIM, :] / jnp.where(denom > 0.0, denom, 1.0)
            for g in range(GROUP):
                heads.append(o[:, g * Q_BLOCK:(g + 1) * Q_BLOCK])
        att_ref[...] = jnp.concatenate(heads, axis=0).T.astype(BF16)

    run_indexer()
    finish_selection(search())
    lax.fori_loop(0, (n_chunks + 1) // 2, lambda t, ms: pair_step(t, *ms), attention_start())
    attention_finish()


def _dsa_prompt(qT, qiT, wiT, kb3, kib3, vT3, *, batch, t_pad, topk, off):
    kc = KEY_CHUNK
    nqb = t_pad // Q_BLOCK
    ncb = t_pad // kc
    n_tok = batch * t_pad
    qcol = lambda n: pl.BlockSpec((n, Q_BLOCK), lambda b, i: (0, b * nqb + i))
    kern = functools.partial(_dsa_prompt_kernel, topk=topk, off=off,
                             n_pos_bits=max(1, math.ceil(math.log2(t_pad))))
    return pl.pallas_call(
        kern,
        grid=(batch, nqb),
        in_specs=[qcol(ATT_WIDTH), qcol(IDX_HEADS * IDX_DIM), qcol(IDX_HEADS),
                  pl.BlockSpec((ncb, kc, KV_WIDTH), lambda b, i: (b, 0, 0)),
                  pl.BlockSpec((ncb, kc, IDX_DIM), lambda b, i: (b, 0, 0)),
                  pl.BlockSpec((ncb, N_KV_HEADS * LANES, kc), lambda b, i: (b, 0, 0))],
        out_specs=pl.BlockSpec((Q_BLOCK, ATT_WIDTH), lambda b, i: (b * nqb + i, 0)),
        out_shape=jax.ShapeDtypeStruct((n_tok, ATT_WIDTH), BF16),
        scratch_shapes=[pltpu.VMEM((ncb, kc, Q_BLOCK), F32),
                        pltpu.VMEM((KV_WIDTH, N_HEADS * Q_BLOCK), BF16),
                        pltpu.VMEM((N_KV_HEADS, LANES, GROUP * Q_BLOCK), F32),
                        pltpu.VMEM((1, Q_BLOCK), jnp.int32),
                        pltpu.VMEM((kc, N_HEADS * Q_BLOCK), F32),
                        pltpu.VMEM((kc, N_HEADS * Q_BLOCK), F32)],
        compiler_params=_cparams(2),
        name="dsa_prompt",
    )(qT, qiT, wiT, kb3, kib3, vT3)


def _idx_sample_kernel(pt_ref, qi_ref, wcol_ref, kinew_ref, cki_hbm, past_ref, new_ref, kibuf, sem):
    n_pages = kibuf.shape[1]
    s_q = new_ref.shape[0]
    b = pl.program_id(0)
    slot = b % 2

    def page_copies(seq, dst_slot):
        return [pltpu.make_async_copy(cki_hbm.at[pt_ref[seq, r]], kibuf.at[dst_slot, r], sem.at[dst_slot])
                for r in range(n_pages)]

    @pl.when(b == 0)
    def _():
        for cp in page_copies(0, 0):
            cp.start()

    @pl.when(b + 1 < pl.num_programs(0))
    def _():
        for cp in page_copies(b + 1, 1 - slot):
            cp.start()

    qi = qi_ref[...]

    def scores(keys_t_bf16):
        s = jnp.dot(qi, keys_t_bf16, preferred_element_type=F32)
        wcol = jnp.concatenate([wcol_ref[...]] * (s.shape[1] // LANES), axis=1)
        s = jnp.maximum(s, 0.0) * wcol
        return jnp.sum(s.reshape(IDX_HEADS, s_q, s.shape[1]), axis=0)

    s_new = scores(kinew_ref[...])
    qrow = lax.broadcasted_iota(jnp.int32, s_new.shape, 0)
    kcol = lax.broadcasted_iota(jnp.int32, s_new.shape, 1)
    new_ref[...] = jnp.where(kcol <= qrow, s_new, NEG_INF)

    for cp in page_copies(b, slot):
        cp.wait()
    npg = PAGES_PER_STEP
    for r0 in range(0, n_pages, IDX_PAGES_PER_DOT):
        keys = jnp.concatenate([kibuf[slot, r0 + j].astype(BF16) for j in range(IDX_PAGES_PER_DOT)], axis=1)
        c, p = divmod(r0, npg)
        past_ref[c, :, p * PAGE_SIZE:(p + IDX_PAGES_PER_DOT) * PAGE_SIZE] = scores(keys)


def _idx_sample(page_table, qi_stack, wcol, kinew, cache_kidx, *, s_q):
    db, n_pages = page_table.shape
    npg = PAGES_PER_STEP
    n_steps = n_pages // npg
    rows = IDX_HEADS * s_q
    per_seq = lambda r, n: pl.BlockSpec((None, r, n), lambda b, pt: (b, 0, 0))
    grid_spec = pltpu.PrefetchScalarGridSpec(
        num_scalar_prefetch=1,
        grid=(db,),
        in_specs=[per_seq(rows, IDX_DIM), per_seq(rows, LANES), per_seq(IDX_DIM, PAGE_SIZE),
                  pl.BlockSpec(memory_space=pl.ANY)],
        out_specs=(pl.BlockSpec((n_steps, None, s_q, npg * PAGE_SIZE), lambda b, pt: (0, b, 0, 0)),
                   per_seq(s_q, PAGE_SIZE)),
        scratch_shapes=[pltpu.VMEM((2, n_pages, IDX_DIM, PAGE_SIZE), F32), pltpu.SemaphoreType.DMA((2,))],
    )
    return pl.pallas_call(
        _idx_sample_kernel,
        grid_spec=grid_spec,
        out_shape=(jax.ShapeDtypeStruct((n_steps, db, s_q, npg * PAGE_SIZE), F32),
                   jax.ShapeDtypeStruct((db, s_q, PAGE_SIZE), F32)),
        compiler_params=_cparams(1),
        name="idx_sample",
    )(page_table, qi_stack, wcol, kinew, cache_kidx)


def _sel_sample_kernel(past_ref, new_ref, thr_ref, j_ref, *, topk, n_pos_bits):
    n_c, g, s_q, w = past_ref.shape
    rows = g * s_q
    l_past = n_c * w
    lane_pos = lax.broadcasted_iota(jnp.int32, (rows, LANES), 1)

    def lane_tile_sum(v):
        return functools.reduce(lambda a, b: a + b, [v[:, t * LANES:(t + 1) * LANES] for t in range(v.shape[1] // LANES)])

    def count(pred):
        def body(cc, acc):
            x = past_ref[cc].reshape(rows, w)
            pos = cc * w + lax.broadcasted_iota(jnp.int32, (rows, w), 1)
            return acc + lane_tile_sum(jnp.where(pred(x, pos), 1.0, 0.0))
        acc = lax.fori_loop(0, n_c, body, jnp.zeros((rows, LANES), F32))
        acc = acc + jnp.where(pred(new_ref[...].reshape(rows, PAGE_SIZE), l_past + lane_pos), 1.0, 0.0)
        return jnp.sum(acc, axis=1, keepdims=True)

    def bit_body(t, res):
        trial = res | jnp.left_shift(jnp.int32(1), 31 - t)
        cand = _ukey_to_f32(trial)
        return jnp.where(count(lambda x, pos: x >= cand) >= topk, trial, res)

    res = lax.fori_loop(0, 32, bit_body, jnp.zeros((rows, 1), jnp.int32))
    thr = _ukey_to_f32(res)
    n_ties = topk - count(lambda x, pos: x > thr)
    cnt_ge = count(lambda x, pos: x >= thr)
    thr_ref[...] = jnp.broadcast_to(thr, (rows, LANES))
    j_ref[...] = jnp.full((rows, LANES), 2 ** 30, jnp.int32)

    @pl.when(jnp.max(jnp.where(cnt_ge > topk, 1.0, 0.0)) > 0.0)
    def _():
        def jbit_body(t, resj):
            trial = resj | jnp.left_shift(jnp.int32(1), n_pos_bits - 1 - t)
            below = count(lambda x, pos: (x == thr) & (pos < trial))
            return jnp.where(below < n_ties, trial, resj)
        jmax = lax.fori_loop(0, n_pos_bits, jbit_body, jnp.zeros((rows, 1), jnp.int32))
        j_ref[...] = jnp.broadcast_to(jmax, (rows, LANES))


def _sel_sample(sc_past, sc_new, *, topk):
    n_c, db, s_q, w = sc_past.shape
    g = math.gcd(SEL_SEQS, db)
    kern = functools.partial(_sel_sample_kernel, topk=topk,
                             n_pos_bits=max(1, math.ceil(math.log2(n_c * w + PAGE_SIZE))))
    return pl.pallas_call(
        kern,
        grid=(db // g,),
        in_specs=[pl.BlockSpec((n_c, g, s_q, w), lambda i: (0, i, 0, 0)),
                  pl.BlockSpec((g, s_q, PAGE_SIZE), lambda i: (i, 0, 0))],
        out_specs=(pl.BlockSpec((g * s_q, LANES), lambda i: (i, 0)),
                   pl.BlockSpec((g * s_q, LANES), lambda i: (i, 0))),
        out_shape=(jax.ShapeDtypeStruct((db * s_q, LANES), F32),
                   jax.ShapeDtypeStruct((db * s_q, LANES), jnp.int32)),
        compiler_params=_cparams(1),
        name="sel_sample",
    )(sc_past, sc_new)


def _att_sample_kernel(pt_ref, qbd_ref, sc_past_ref, sc_new_ref, thr_ref, j_ref, knew_ref, vnew_ref, ck_hbm, cv_hbm,
                       att_ref, kbuf, vbuf, ksem, vsem, acc_ref, m_ref, l_ref):
    ring, npg = kbuf.shape[:2]
    ahead = ring - 1
    n_steps = sc_past_ref.shape[0]
    b = pl.program_id(0)
    n_seq = pl.num_programs(0)
    qbd = qbd_ref[...]
    thr = thr_ref[...]
    jmax = j_ref[...]

    def page_copies(seq, step, slot):
        cps = []
        for r in range(npg):
            page = pt_ref[seq, step * npg + r]
            cps.append(pltpu.make_async_copy(ck_hbm.at[page], kbuf.at[slot, r], ksem.at[slot]))
            cps.append(pltpu.make_async_copy(cv_hbm.at[page], vbuf.at[slot, r], vsem.at[slot]))
        return cps

    @pl.when(b == 0)
    def _():
        for g in range(ahead):
            for cp in page_copies(0, g, g):
                cp.start()

    acc_ref[...] = jnp.zeros(acc_ref.shape, F32)
    m_ref[...] = jnp.full(m_ref.shape, NEG_INF, F32)
    l_ref[...] = jnp.zeros(l_ref.shape, F32)

    def masked_scores(keys_t_bf16, sc, pos0):
        pos = pos0 + lax.broadcasted_iota(jnp.int32, sc.shape, 1)
        tile = lambda a: jnp.concatenate([a] * (sc.shape[1] // LANES), axis=1)
        sel = (sc > tile(thr)) | ((sc == tile(thr)) & (pos <= tile(jmax)))
        bias = jnp.where(sel, 0.0, NEG_INF)
        s = jnp.dot(qbd, keys_t_bf16, preferred_element_type=F32)
        return s + jnp.concatenate([bias] * N_HEADS, axis=0)

    def accumulate(s_list, vt_list):
        m_old = m_ref[...]
        m_new = jnp.maximum(m_old, jnp.max(functools.reduce(jnp.maximum, s_list), axis=1, keepdims=True))
        m_safe = jnp.maximum(m_new, -1e30)
        alpha = jnp.exp2(m_old - m_safe)
        acc = acc_ref[...] * alpha
        p_list = [jnp.exp2(s - m_safe) for s in s_list]
        for p, vt in zip(p_list, vt_list):
            acc = acc + lax.dot_general(p.astype(BF16), vt, NT_DIMS, preferred_element_type=F32)
        acc_ref[...] = acc
        l_ref[...] = l_ref[...] * alpha + jnp.sum(functools.reduce(lambda a, b: a + b, p_list), axis=1, keepdims=True)
        m_ref[...] = m_new

    for c in range(n_steps):
        group = b * n_steps + c
        slot = group % ring
        slot_ahead = (group + ahead) % ring
        if c + ahead < n_steps:
            for cp in page_copies(b, c + ahead, slot_ahead):
                cp.start()
        else:
            @pl.when(b + 1 < n_seq)
            def _():
                for cp in page_copies(b + 1, c + ahead - n_steps, slot_ahead):
                    cp.start()
        for cp in page_copies(b, c, slot):
            cp.wait()
        s_list, v_list = [], []
        for p in range(0, npg, 2):
            sc = sc_past_ref[c, :, p * PAGE_SIZE:(p + 2) * PAGE_SIZE]
            keys = jnp.concatenate([kbuf[slot, p].astype(BF16), kbuf[slot, p + 1].astype(BF16)], axis=1)
            s_list.append(masked_scores(keys, sc, (c * npg + p) * PAGE_SIZE))
            v_list.append(jnp.concatenate([vbuf[slot, p].astype(BF16), vbuf[slot, p + 1].astype(BF16)], axis=1))
        accumulate(s_list, v_list)

    accumulate([masked_scores(knew_ref[...], sc_new_ref[...], n_steps * npg * PAGE_SIZE)], [vnew_ref[...]])
    att_ref[...] = acc_ref[...] / l_ref[...]


def _att_sample(page_table, qbd, sc_past, sc_new, thr, jmax, knew, vnew, cache_k, cache_v, *, s_q):
    db, n_pages = page_table.shape
    npg = PAGES_PER_STEP
    n_steps = n_pages // npg
    rows = N_HEADS * s_q
    assert n_steps >= PAGE_RING - 1
    per_seq = lambda r, n: pl.BlockSpec((None, r, n), lambda b, pt: (b, 0, 0))
    page_buf = pltpu.VMEM((PAGE_RING, npg, KV_WIDTH, PAGE_SIZE), F32)
    ring_sem = pltpu.SemaphoreType.DMA((PAGE_RING,))
    grid_spec = pltpu.PrefetchScalarGridSpec(
        num_scalar_prefetch=1,
        grid=(db,),
        in_specs=[per_seq(rows, KV_WIDTH),
                  pl.BlockSpec((n_steps, None, s_q, npg * PAGE_SIZE), lambda b, pt: (0, b, 0, 0)),
                  per_seq(s_q, PAGE_SIZE),
                  pl.BlockSpec((s_q, LANES), lambda b, pt: (b, 0)),
                  pl.BlockSpec((s_q, LANES), lambda b, pt: (b, 0)),
                  per_seq(KV_WIDTH, PAGE_SIZE), per_seq(KV_WIDTH, PAGE_SIZE),
                  pl.BlockSpec(memory_space=pl.ANY), pl.BlockSpec(memory_space=pl.ANY)],
        out_specs=per_seq(rows, KV_WIDTH),
        scratch_shapes=[page_buf, page_buf, ring_sem, ring_sem,
                        pltpu.VMEM((rows, KV_WIDTH), F32), pltpu.VMEM((rows, 1), F32), pltpu.VMEM((rows, 1), F32)],
    )
    return pl.pallas_call(
        _att_sample_kernel,
        grid_spec=grid_spec,
        out_shape=jax.ShapeDtypeStruct((db, rows, KV_WIDTH), F32),
        compiler_params=_cparams(1),
        name="att_sample",
    )(page_table, qbd, sc_past, sc_new, thr, jmax, knew, vnew, cache_k, cache_v)


def _gla_kernel(*refs, chunk, n_chunks, off, t_end, has_s0):
    if has_s0:
        gq_ref, gk_ref, gv_ref, og_ref, la_ref, gn_ref, s0_ref, o_ref, sfin_ref, st_ref = refs
    else:
        gq_ref, gk_ref, gv_ref, og_ref, la_ref, gn_ref, o_ref, sfin_ref, st_ref = refs
    j = pl.program_id(1)
    tb = chunk * n_chunks

    @pl.when(j == 0)
    def _():
        for hh in range(GLA_HEADS):
            st_ref[hh] = s0_ref[hh].T if has_s0 else jnp.zeros((GLA_DV, GLA_DK), F32)

    r_i = lax.broadcasted_iota(jnp.int32, (chunk, chunk), 0)
    c_i = lax.broadcasted_iota(jnp.int32, (chunk, chunk), 1)
    causal = r_i >= c_i
    tril = jnp.where(causal, 1.0, 0.0)
    gn = gn_ref[...]

    def chunk_body(c, carry):
        r0 = pl.multiple_of(c * chunk, chunk)
        rows = pl.ds(r0, chunk)
        pos = j * tb + r0 + lax.broadcasted_iota(jnp.int32, (chunk, GLA_KW), 0)
        valid = (pos >= off) & (pos < t_end)
        la = jnp.where(valid, la_ref[rows, :], 0.0)
        k = jnp.where(valid, gk_ref[rows, :], 0.0)
        q = gq_ref[rows, :] * GLA_DK ** -0.5
        v = gv_ref[rows, :]
        b = jnp.dot(tril, la, preferred_element_type=F32, precision=lax.Precision.HIGHEST)
        b_last = b[chunk - 1:chunk, :]
        qd = (q * jnp.exp(b)).astype(BF16)
        kd = (k * jnp.exp(-b)).astype(BF16)
        ke = (k * jnp.exp(b_last - b)).astype(BF16)
        decay = jnp.exp(b_last)
        outs = []
        for hh in range(GLA_HEADS):
            ks = slice(hh * GLA_DK, (hh + 1) * GLA_DK)
            vh = v[:, hh * GLA_DV:(hh + 1) * GLA_DV]
            a = lax.dot_general(qd[:, ks], kd[:, ks], NT_DIMS, preferred_element_type=F32)
            a = jnp.where(causal, a, 0.0).astype(BF16)
            st = st_ref[hh]
            o = (jnp.dot(a, vh, preferred_element_type=F32)
                 + lax.dot_general(qd[:, ks], st.astype(BF16), NT_DIMS, preferred_element_type=F32))
            u_t = lax.dot_general(vh, ke[:, ks], TN_DIMS, preferred_element_type=F32)
            st_ref[hh] = decay[:, ks] * st + u_t
            o = o * lax.rsqrt(jnp.mean(o * o, axis=-1, keepdims=True) + LN_EPS) * gn
            outs.append(o)
        og = og_ref[rows, :]
        o_ref[rows, :] = (jnp.concatenate(outs, axis=1) * (og * jax.nn.sigmoid(og))).astype(BF16)
        return carry

    per_step = next(k for k in (6, 3, 2, 1) if n_chunks % k == 0)

    def step_body(t, carry):
        for k in range(per_step):
            chunk_body(per_step * t + k, carry)
        return carry

    lax.fori_loop(0, n_chunks // per_step, step_body, 0)

    @pl.when(j == pl.num_programs(1) - 1)
    def _():
        for hh in range(GLA_HEADS):
            sfin_ref[hh] = st_ref[hh].T


def _gla(gq, gk, gv, og, la, gnorm, s0, *, batch, t_pad, tb, chunk, off, t_end):
    n_steps = t_pad // tb
    tok = lambda n: pl.BlockSpec((tb, n), lambda b, j: (b * n_steps + j, 0))
    state = pl.BlockSpec((None, GLA_HEADS, GLA_DK, GLA_DV), lambda b, j: (b, 0, 0, 0))
    has_s0 = s0 is not None
    kern = functools.partial(_gla_kernel, chunk=chunk, n_chunks=tb // chunk, off=off, t_end=t_end, has_s0=has_s0)
    in_specs = [tok(GLA_KW), tok(GLA_KW), tok(GLA_VW), tok(GLA_VW), tok(GLA_KW), _full_spec(gnorm.shape)]
    args = [gq, gk, gv, og, la, gnorm]
    if has_s0:
        in_specs.append(state)
        args.append(s0)
    return pl.pallas_call(
        kern,
        grid=(batch, n_steps),
        in_specs=in_specs,
        out_specs=(tok(GLA_VW), state),
        out_shape=(jax.ShapeDtypeStruct((batch * t_pad, GLA_VW), BF16),
                   jax.ShapeDtypeStruct((batch, GLA_HEADS, GLA_DK, GLA_DV), F32)),
        scratch_shapes=[pltpu.VMEM((GLA_HEADS, GLA_DV, GLA_DK), F32)],
        compiler_params=_cparams(2),
        name="gla",
    )(*args)


def _merge_kernel(x_ref, att_ref, gla_ref, gates_ref, lng_ref, lnb_ref, wa_ref, wg_ref, wo_ref,
                  l1g_ref, l1b_ref, h1_ref, *, alpha):
    h = _layer_norm(x_ref[...], lng_ref[...], lnb_ref[...])
    pa = jnp.dot(att_ref[...], wa_ref[...], preferred_element_type=F32)
    pg = jnp.dot(gla_ref[...], wg_ref[...], preferred_element_type=F32)
    gates = gates_ref[...].astype(F32)
    merged = gates[:, :D_MODEL] * pa + gates[:, D_MODEL:] * pg
    mix = jnp.dot(merged.astype(BF16), wo_ref[...], preferred_element_type=F32)
    h1_ref[...] = _layer_norm(alpha * h + mix, l1g_ref[...], l1b_ref[...])


def _merge(x, att, gla, gates, ln_g, ln_b, wa, wg, wo, l1g, l1b, *, alpha):
    n_tok = x.shape[0]
    tm = _row_tile(n_tok, TM_TOKEN)
    row = lambda n: pl.BlockSpec((tm, n), lambda i: (i, 0))
    consts = (ln_g, ln_b, wa, wg, wo, l1g, l1b)
    return pl.pallas_call(
        functools.partial(_merge_kernel, alpha=alpha),
        grid=(n_tok // tm,),
        in_specs=[row(D_MODEL), row(ATT_WIDTH), row(GLA_VW), row(N_BRANCHES * D_MODEL)]
                 + [_full_spec(a.shape) for a in consts],
        out_specs=row(D_MODEL),
        out_shape=jax.ShapeDtypeStruct((n_tok, D_MODEL), F32),
        compiler_params=_cparams(1),
        name="merge",
    )(x, att, gla, gates, *consts)


def _ffn_kernel(h_ref, wu_ref, wd_ref, g_ref, b_ref, y_ref, *, alpha, n_split):
    h = h_ref[...]
    hb = h.astype(BF16)
    w = D_FF // n_split
    ff = jnp.zeros(h.shape, F32)
    for s in range(n_split):
        u = jnp.dot(hb, wu_ref[:, s * w:(s + 1) * w], preferred_element_type=F32)
        u = jnp.square(jnp.maximum(u, 0.0)).astype(BF16)
        ff = ff + jnp.dot(u, wd_ref[s * w:(s + 1) * w, :], preferred_element_type=F32)
    y_ref[...] = _layer_norm(alpha * h + ff, g_ref[...], b_ref[...])


def _ffn(h1, wu, wd, g, b, *, alpha):
    n_tok = h1.shape[0]
    tm = _row_tile(n_tok, TM_TOKEN)
    row = pl.BlockSpec((tm, D_MODEL), lambda i: (i, 0))
    return pl.pallas_call(
        functools.partial(_ffn_kernel, alpha=alpha, n_split=4),
        grid=(n_tok // tm,),
        in_specs=[row] + [_full_spec(a.shape) for a in (wu, wd, g, b)],
        out_specs=row,
        out_shape=jax.ShapeDtypeStruct((n_tok, D_MODEL), F32),
        compiler_params=_cparams(1),
        name="ffn",
    )(h1, wu, wd, g, b)


def _ffn_window(h1, wu, wd, g, b, *, alpha, start, length):
    batch, rows, _ = h1.shape
    tm = _row_tile(length, TM_TOKEN)
    consts = (wu, wd, g, b)
    assert rows % SUBLANES == 0 and start % SUBLANES == 0
    first_row = lambda bi, j: pl.multiple_of(bi * rows + start + j * tm, SUBLANES)
    return pl.pallas_call(
        functools.partial(_ffn_kernel, alpha=alpha, n_split=4),
        grid=(batch, length // tm),
        in_specs=[pl.BlockSpec((pl.Element(tm), pl.Element(D_MODEL)), lambda bi, j: (first_row(bi, j), 0))]
                 + [_full_spec(a.shape) for a in consts],
        out_specs=pl.BlockSpec((None, tm, D_MODEL), lambda bi, j: (bi, j, 0)),
        out_shape=jax.ShapeDtypeStruct((batch, length, D_MODEL), F32),
        compiler_params=_cparams(2),
        name="ffn_window",
    )(h1.reshape(batch * rows, D_MODEL), *consts)


def _pack_weights(w_in, w_gla_a2, b_gla_a):
    points = []
    acc = 0
    for s in IN_SIZES[:-1]:
        acc += s
        points.append(acc)
    wq, wk, wv, wqi, wki, wwi, wgq, wgk, wgv, wog, wa1, wgt = jnp.split(w_in, points, axis=-1)
    wq = wq * (math.log2(math.e) * HEAD_DIM ** -0.5)
    wqi = wqi * IDX_DIM ** -0.5
    pad_cols = lambda a, n: jnp.pad(a, ((0, 0), (0, n - a.shape[1])))
    wv_heads = wv.T.reshape(N_KV_HEADS, HEAD_DIM, D_MODEL)
    wv_aug = jnp.pad(wv_heads, ((0, 0), (0, LANES - HEAD_DIM), (0, 0))).reshape(N_KV_HEADS * LANES, D_MODEL)
    w = {
        "qT": wq.T, "qiT": wqi.T, "vT": wv_aug,
        "wiT": jnp.pad(wwi.T, ((0, 2 * SUBLANES - IDX_HEADS), (0, 0))),
        "row_p": pad_cols(jnp.concatenate([wk, wv, wki], axis=1), 2 * KV_WIDTH + LANES),
        "row_s": pad_cols(jnp.concatenate([wq, wqi, wk, wv, wki, wwi], axis=1),
                          ATT_WIDTH + IDX_HEADS * IDX_DIM + 2 * KV_WIDTH + LANES),
        "gla": jnp.concatenate([wgq, wgk, wgv, wog], axis=1),
        "a1": pad_cols(wa1, LANES),
        "a2": jnp.pad(w_gla_a2, ((0, LANES - GLA_GATE_RANK), (0, 0))),
        "gt": wgt,
    }
    w = {name: a.astype(BF16) for name, a in w.items()}
    w["ba"] = b_gla_a.astype(F32)[None]
    return w


def _round_up(x, m):
    return -(-x // m) * m


def _row_tile(n, pref, unit=2 * SUBLANES):
    best = unit
    for t in range(unit, min(n, pref) + 1, unit):
        if n % t == 0:
            best = t
    assert n % best == 0
    return best


def kernel(x_prompt, x_sample, cache_k, cache_v, cache_kidx, state_gla, page_table, meta_tokens, ln_in_g, ln_in_b, w_in, w_gla_a2, b_gla_a, gla_norm_g, w_proj_attn, w_proj_gla, w_out, ln1_g, ln1_b, w_ff_up, w_ff_down, ln2_g, ln2_b):
    depth = w_in.shape[0]
    assert depth == 1, "single-layer step only"
    B, S_p, D = x_prompt.shape
    DB, S_s, _ = x_sample.shape
    n_pages = page_table.shape[1]
    past = n_pages * PAGE_SIZE
    assert D == D_MODEL and S_p % GLA_CHUNK == 0 and S_s <= SUBLANES and n_pages % PAGES_PER_STEP == 0
    topk_prompt = min(TOPK_MAX, S_p // 4)
    topk_sample = min(TOPK_MAX, (past + S_s) // 4)
    alpha = (2 * depth) ** 0.25

    row = lambda a: a.astype(F32).reshape(1, -1)
    ln_g, ln_b = row(ln_in_g), row(ln_in_b)
    w = _pack_weights(w_in[0], w_gla_a2[0], b_gla_a[0])
    wa, wg, wo = (a[0].astype(BF16) for a in (w_proj_attn, w_proj_gla, w_out))
    wu, wd = w_ff_up[0].astype(BF16), w_ff_down[0].astype(BF16)
    gnorm = row(gla_norm_g[0])
    l1g, l1b, l2g, l2b = row(ln1_g[0]), row(ln1_b[0]), row(ln2_g[0]), row(ln2_b[0])

    T = S_p + N_META
    off = (-N_META) % GLA_CHUNK
    t_pad = _round_up(off + T, math.lcm(KEY_CHUNK, GLA_CHUNK))
    gla_tb = _row_tile(t_pad, GLA_TOKENS_PER_STEP, GLA_CHUNK)
    n_tok = B * t_pad
    meta = jnp.broadcast_to(meta_tokens.astype(x_prompt.dtype)[None], (B, N_META, D))
    xp = jnp.concatenate([jnp.zeros((B, off, D), x_prompt.dtype), meta, x_prompt,
                          jnp.zeros((B, t_pad - off - T, D), x_prompt.dtype)], axis=1).reshape(n_tok, D)

    qT, qiT, vT3, wiT, k32, v32, ki32, kb, kib = _proj_attn_prompt(xp, ln_g, ln_b, w, n_tok)
    att_p = _dsa_prompt(qT, qiT, wiT,
                        kb.reshape(n_tok // KEY_CHUNK, KEY_CHUNK, KV_WIDTH),
                        kib.reshape(n_tok // KEY_CHUNK, KEY_CHUNK, IDX_DIM), vT3,
                        batch=B, t_pad=t_pad, topk=topk_prompt, off=off)
    gq, gk, gv, og, la, gates_p = _proj_gla(xp, ln_g, ln_b, w, n_tok)
    gla_p, state_p = _gla(gq, gk, gv, og, la, gnorm, None, batch=B, t_pad=t_pad, tb=gla_tb,
                          chunk=GLA_CHUNK, off=off, t_end=off + T)
    h1_p = _merge(xp, att_p, gla_p, gates_p, ln_g, ln_b, wa, wg, wo, l1g, l1b, alpha=alpha)
    seq = lambda a: a.reshape((B, t_pad) + a.shape[1:])
    y_prompt = _ffn_window(seq(h1_p), wu, wd, l2g, l2b, alpha=alpha, start=off + N_META, length=S_p)

    k_prompt = seq(k32)[:, off:off + T].reshape(1, B, T, N_KV_HEADS, HEAD_DIM)
    v_prompt = seq(v32)[:, off:off + T].reshape(1, B, T, N_KV_HEADS, HEAD_DIM)
    kidx_prompt = seq(ki32)[:, off:off + T][None]
    gla_state_prompt = state_p[None]

    R = SAMPLE_ROWS
    n_tok_s = DB * R
    xs = jnp.pad(x_sample, ((0, 0), (0, R - S_s), (0, 0))).reshape(n_tok_s, D)
    q_s, qi_s, k_s, v_s, kiwi_s = _proj_attn_sample(xs, ln_g, ln_b, w, n_tok_s)
    sseq = lambda a: a.reshape((DB, R) + a.shape[1:])[:, :S_s]
    k_new, v_new = sseq(k_s), sseq(v_s)
    ki_new = sseq(kiwi_s)[..., :IDX_DIM]
    wi_new = sseq(kiwi_s)[..., IDX_DIM:IDX_DIM + IDX_HEADS]
    qi_stack = sseq(qi_s).reshape(DB, S_s, IDX_HEADS, IDX_DIM).transpose(0, 2, 1, 3).reshape(DB, IDX_HEADS * S_s, IDX_DIM)
    wcol = jnp.broadcast_to(wi_new.transpose(0, 2, 1).reshape(DB, IDX_HEADS * S_s, 1), (DB, IDX_HEADS * S_s, LANES))
    q_heads = sseq(q_s).reshape(DB, S_s, N_HEADS, HEAD_DIM).transpose(0, 2, 1, 3)
    kv_of_head = (jnp.arange(N_HEADS) // GROUP)[:, None] == jnp.arange(N_KV_HEADS)[None, :]
    qbd = jnp.where(kv_of_head[None, :, None, :, None], q_heads[:, :, :, None, :], jnp.zeros((), BF16))
    qbd = qbd.reshape(DB, N_HEADS * S_s, KV_WIDTH)
    pad_page = lambda a: jnp.pad(a, ((0, 0), (0, PAGE_SIZE - S_s), (0, 0))).astype(BF16).transpose(0, 2, 1)
    kinew_pg, knew_pg, vnew_pg = pad_page(ki_new), pad_page(k_new), pad_page(v_new)
    ck = cache_k[0].transpose(0, 2, 3, 1).reshape(-1, KV_WIDTH, PAGE_SIZE)
    cv = cache_v[0].transpose(0, 2, 3, 1).reshape(-1, KV_WIDTH, PAGE_SIZE)
    cki = cache_kidx[0].transpose(0, 2, 1)
    sc_past, sc_new = _idx_sample(page_table, qi_stack, wcol, kinew_pg, cki, s_q=S_s)
    thr, jmax = _sel_sample(sc_past, sc_new, topk=topk_sample)
    o_s = _att_sample(page_table, qbd, sc_past, sc_new, thr, jmax, knew_pg, vnew_pg, ck, cv, s_q=S_s)
    o_s = o_s.reshape(DB, N_KV_HEADS, GROUP, S_s, N_KV_HEADS, HEAD_DIM)
    att_s = jnp.stack([o_s[:, n, :, :, n, :] for n in range(N_KV_HEADS)], axis=1)
    att_s = att_s.transpose(0, 3, 1, 2, 4).reshape(DB, S_s, ATT_WIDTH).astype(BF16)
    att_s = jnp.pad(att_s, ((0, 0), (0, R - S_s), (0, 0))).reshape(n_tok_s, ATT_WIDTH)

    gq, gk, gv, og, la, gates_s = _proj_gla(xs, ln_g, ln_b, w, n_tok_s)
    gla_s, state_s = _gla(gq, gk, gv, og, la, gnorm, state_gla[0], batch=DB, t_pad=R, tb=R,
                          chunk=R, off=0, t_end=S_s)
    h1_s = _merge(xs, att_s, gla_s, gates_s, ln_g, ln_b, wa, wg, wo, l1g, l1b, alpha=alpha)
    y_s = _ffn(h1_s, wu, wd, l2g, l2b, alpha=alpha)

    y_sample = y_s.reshape(DB, R, D)[:, :S_s]
    k_sample = k_new.reshape(1, DB, S_s, N_KV_HEADS, HEAD_DIM)
    v_sample = v_new.reshape(1, DB, S_s, N_KV_HEADS, HEAD_DIM)
    kidx_sample = ki_new[None]
    gla_state_sample = state_s[None]
    return (y_prompt, y_sample, k_prompt, v_prompt, kidx_prompt, gla_state_prompt,
            k_sample, v_sample, kidx_sample, gla_state_sample)
```

```python
import functools
import math

import jax
import jax.numpy as jnp
from jax import lax
from jax.experimental import pallas as pl
from jax.experimental.pallas import tpu as pltpu

D_MODEL = 1024
PAGE_SIZE = 128
N_META = 16
N_HEADS = 16
HEAD_DIM = 64
N_KV_HEADS = 4
GROUP = N_HEADS // N_KV_HEADS
ATT_WIDTH = N_HEADS * HEAD_DIM
KV_WIDTH = N_KV_HEADS * HEAD_DIM
IDX_HEADS = 8
IDX_DIM = 64
TOPK_MAX = 256
GLA_HEADS = 4
GLA_DK = D_MODEL // 2 // GLA_HEADS
GLA_DV = D_MODEL // GLA_HEADS
GLA_KW = GLA_HEADS * GLA_DK
GLA_VW = GLA_HEADS * GLA_DV
GLA_GATE_RANK = 16
GLA_TAU = 16.0
GLA_CHUNK = 64
N_BRANCHES = 2
D_FF = 4 * D_MODEL
LN_EPS = 1e-5
IN_SIZES = (ATT_WIDTH, KV_WIDTH, KV_WIDTH, IDX_HEADS * IDX_DIM, IDX_DIM, IDX_HEADS,
            GLA_KW, GLA_KW, GLA_VW, GLA_VW, GLA_GATE_RANK, N_BRANCHES * D_MODEL)

LANES = 128
SUBLANES = 8
VMEM_LIMIT_BYTES = 56 * 1024 * 1024
COUNT_LANES = 4
Q_BLOCK = LANES
KEY_CHUNK = 3 * LANES
SAMPLE_ROWS = 16
PAGES_PER_STEP = 16
PAGE_RING = 3
IDX_PAGES_PER_DOT = 4
SEL_SEQS = 16
TM_PROJ_GLA = 512
PROJ_ATTN_CHUNKS = 2
TM_TOKEN = 512
GLA_TOKENS_PER_STEP = 6 * GLA_CHUNK

F32 = jnp.float32
BF16 = jnp.bfloat16
NEG_INF = float("-inf")
INT_MIN = -2 ** 31
NT_DIMS = (((1,), (1,)), ((), ()))
TN_DIMS = (((0,), (0,)), ((), ()))


def _cparams(n_grid):
    return pltpu.CompilerParams(dimension_semantics=("arbitrary",) * n_grid,
                                vmem_limit_bytes=VMEM_LIMIT_BYTES)


def _full_spec(shape):
    nd = len(shape)
    return pl.BlockSpec(shape, lambda *_: (0,) * nd, pipeline_mode=pl.Buffered(1))


def _layer_norm(x, g, b):
    mu = jnp.mean(x, axis=-1, keepdims=True)
    xc = x - mu
    var = jnp.mean(xc * xc, axis=-1, keepdims=True)
    return xc * lax.rsqrt(var + LN_EPS) * g + b


def _ukey_to_f32(u):
    bits = jnp.where(u < 0, u & jnp.int32(0x7FFFFFFF), ~u)
    return lax.bitcast_convert_type(bits, F32)


def _proj_attn_prompt_kernel(x_ref, g_ref, b_ref, wq_ref, wqi_ref, wv_ref, wwi_ref, wrow_ref,
                             qT_ref, qiT_ref, vT_ref, wiT_ref, k_ref, v_ref, ki_ref, kb_ref, kib_ref):
    hb = _layer_norm(x_ref[...], g_ref[...], b_ref[...]).astype(BF16)
    qT_ref[...] = lax.dot_general(wq_ref[...], hb, NT_DIMS, preferred_element_type=F32).astype(BF16)
    qiT_ref[...] = lax.dot_general(wqi_ref[...], hb, NT_DIMS, preferred_element_type=F32).astype(BF16)
    vt = lax.dot_general(wv_ref[...], hb, NT_DIMS, preferred_element_type=F32)
    row = lax.broadcasted_iota(jnp.int32, vt.shape, 0)
    vt = jnp.where(row % LANES == HEAD_DIM, 1.0, vt)
    for j in range(vT_ref.shape[0]):
        vT_ref[j] = vt[:, j * KEY_CHUNK:(j + 1) * KEY_CHUNK].astype(BF16)
    wi = lax.dot_general(wwi_ref[...], hb, NT_DIMS, preferred_element_type=F32)
    wiT_ref[...] = wi[:IDX_HEADS] * IDX_HEADS ** -0.5
    y = jnp.dot(hb, wrow_ref[...], preferred_element_type=F32)
    k = y[:, :KV_WIDTH]
    ki = y[:, 2 * KV_WIDTH:2 * KV_WIDTH + IDX_DIM]
    k_ref[...] = k
    v_ref[...] = y[:, KV_WIDTH:2 * KV_WIDTH]
    ki_ref[...] = ki
    kb_ref[...] = k.astype(BF16)
    kib_ref[...] = ki.astype(BF16)


def _proj_attn_prompt(x, ln_g, ln_b, w, n_tok):
    tm = _row_tile(n_tok, PROJ_ATTN_CHUNKS * KEY_CHUNK, KEY_CHUNK)
    n_steps = n_tok // tm
    cps = tm // KEY_CHUNK
    row = lambda n: pl.BlockSpec((tm, n), lambda i: (i, 0))
    col = lambda n: pl.BlockSpec((n, tm), lambda i: (0, i))
    out_shape = (
        jax.ShapeDtypeStruct((ATT_WIDTH, n_tok), BF16),
        jax.ShapeDtypeStruct((IDX_HEADS * IDX_DIM, n_tok), BF16),
        jax.ShapeDtypeStruct((n_tok // KEY_CHUNK, N_KV_HEADS * LANES, KEY_CHUNK), BF16),
        jax.ShapeDtypeStruct((IDX_HEADS, n_tok), F32),
        jax.ShapeDtypeStruct((n_tok, KV_WIDTH), F32),
        jax.ShapeDtypeStruct((n_tok, KV_WIDTH), F32),
        jax.ShapeDtypeStruct((n_tok, IDX_DIM), F32),
        jax.ShapeDtypeStruct((n_tok, KV_WIDTH), BF16),
        jax.ShapeDtypeStruct((n_tok, IDX_DIM), BF16),
    )
    out_specs = (col(ATT_WIDTH), col(IDX_HEADS * IDX_DIM),
                 pl.BlockSpec((cps, N_KV_HEADS * LANES, KEY_CHUNK), lambda i: (i, 0, 0)),
                 col(IDX_HEADS), row(KV_WIDTH), row(KV_WIDTH), row(IDX_DIM), row(KV_WIDTH), row(IDX_DIM))
    ws = (w["qT"], w["qiT"], w["vT"], w["wiT"], w["row_p"])
    return pl.pallas_call(
        _proj_attn_prompt_kernel,
        grid=(n_steps,),
        in_specs=[row(D_MODEL), _full_spec(ln_g.shape), _full_spec(ln_b.shape)] + [_full_spec(a.shape) for a in ws],
        out_specs=out_specs,
        out_shape=out_shape,
        compiler_params=_cparams(1),
        name="proj_attn_prompt",
    )(x, ln_g, ln_b, *ws)


def _proj_attn_sample_kernel(x_ref, g_ref, b_ref, w_ref, scale_ref, q_ref, qi_ref, k_ref, v_ref, kiwi_ref):
    hb = _layer_norm(x_ref[...], g_ref[...], b_ref[...]).astype(BF16)
    y = jnp.dot(hb, w_ref[...], preferred_element_type=F32)
    o = 0
    q_ref[...] = y[:, o:o + ATT_WIDTH].astype(BF16)
    o += ATT_WIDTH
    qi_ref[...] = y[:, o:o + IDX_HEADS * IDX_DIM].astype(BF16)
    o += IDX_HEADS * IDX_DIM
    k_ref[...] = y[:, o:o + KV_WIDTH]
    o += KV_WIDTH
    v_ref[...] = y[:, o:o + KV_WIDTH]
    o += KV_WIDTH
    kiwi_ref[...] = y[:, o:o + LANES] * scale_ref[...]


def _proj_attn_sample(x, ln_g, ln_b, w, n_tok):
    tm = _row_tile(n_tok, TM_TOKEN)
    row = lambda n: pl.BlockSpec((tm, n), lambda i: (i, 0))
    lane = lax.iota(jnp.int32, LANES)
    scale = jnp.where((lane >= IDX_DIM) & (lane < IDX_DIM + IDX_HEADS), IDX_HEADS ** -0.5, 1.0).astype(F32)[None]
    out_shape = (
        jax.ShapeDtypeStruct((n_tok, ATT_WIDTH), BF16),
        jax.ShapeDtypeStruct((n_tok, IDX_HEADS * IDX_DIM), BF16),
        jax.ShapeDtypeStruct((n_tok, KV_WIDTH), F32),
        jax.ShapeDtypeStruct((n_tok, KV_WIDTH), F32),
        jax.ShapeDtypeStruct((n_tok, LANES), F32),
    )
    return pl.pallas_call(
        _proj_attn_sample_kernel,
        grid=(n_tok // tm,),
        in_specs=[row(D_MODEL), _full_spec(ln_g.shape), _full_spec(ln_b.shape),
                  _full_spec(w["row_s"].shape), _full_spec(scale.shape)],
        out_specs=(row(ATT_WIDTH), row(IDX_HEADS * IDX_DIM), row(KV_WIDTH), row(KV_WIDTH), row(LANES)),
        out_shape=out_shape,
        compiler_params=_cparams(1),
        name="proj_attn_sample",
    )(x, ln_g, ln_b, w["row_s"], scale)


def _log_sigmoid(x):
    return jnp.minimum(x, 0.0) - jnp.log1p(jnp.exp(-jnp.abs(x)))


def _proj_gla_kernel(x_ref, g_ref, b_ref, wg_ref, wa1_ref, wa2_ref, ba_ref, wgt_ref,
                     gq_ref, gk_ref, gv_ref, og_ref, la_ref, gates_ref):
    hb = _layer_norm(x_ref[...], g_ref[...], b_ref[...]).astype(BF16)
    y = jnp.dot(hb, wg_ref[...], preferred_element_type=F32)
    gq_ref[...] = y[:, :GLA_KW]
    gk_ref[...] = y[:, GLA_KW:2 * GLA_KW]
    gv_ref[...] = y[:, 2 * GLA_KW:2 * GLA_KW + GLA_VW].astype(BF16)
    og_ref[...] = y[:, 2 * GLA_KW + GLA_VW:]
    a1 = jnp.dot(hb, wa1_ref[...], preferred_element_type=F32).astype(BF16)
    z = jnp.dot(a1, wa2_ref[...], preferred_element_type=F32) + ba_ref[...]
    la_ref[...] = _log_sigmoid(z) / GLA_TAU
    gt = jnp.dot(hb, wgt_ref[...], preferred_element_type=F32)
    gates_ref[...] = jax.nn.sigmoid(gt).astype(BF16)


def _proj_gla(x, ln_g, ln_b, w, n_tok):
    tm = _row_tile(n_tok, TM_PROJ_GLA)
    row = lambda n: pl.BlockSpec((tm, n), lambda i: (i, 0))
    ws = (w["gla"], w["a1"], w["a2"], w["ba"], w["gt"])
    out_shape = (
        jax.ShapeDtypeStruct((n_tok, GLA_KW), F32),
        jax.ShapeDtypeStruct((n_tok, GLA_KW), F32),
        jax.ShapeDtypeStruct((n_tok, GLA_VW), BF16),
        jax.ShapeDtypeStruct((n_tok, GLA_VW), F32),
        jax.ShapeDtypeStruct((n_tok, GLA_KW), F32),
        jax.ShapeDtypeStruct((n_tok, N_BRANCHES * D_MODEL), BF16),
    )
    return pl.pallas_call(
        _proj_gla_kernel,
        grid=(n_tok // tm,),
        in_specs=[row(D_MODEL), _full_spec(ln_g.shape), _full_spec(ln_b.shape)] + [_full_spec(a.shape) for a in ws],
        out_specs=(row(GLA_KW), row(GLA_KW), row(GLA_VW), row(GLA_VW), row(GLA_KW), row(N_BRANCHES * D_MODEL)),
        out_shape=out_shape,
        compiler_params=_cparams(1),
        name="proj_gla",
    )(x, ln_g, ln_b, *ws)


def _dsa_prompt_kernel(qT_ref, qiT_ref, wiT_ref, kb_ref, kib_ref, vT_ref, att_ref,
                       sc_ref, qn_ref, acc_ref, j_ref, sa_ref, sb_ref, mrun_ref, *, topk, off, n_pos_bits):
    kc = KEY_CHUNK
    i = pl.program_id(1)
    n_chunks = (i * Q_BLOCK + Q_BLOCK + kc - 1) // kc
    qpos = i * Q_BLOCK + lax.broadcasted_iota(jnp.int32, (1, Q_BLOCK), 1)

    def key_pos(c):
        return c * kc + lax.broadcasted_iota(jnp.int32, (kc, Q_BLOCK), 0)

    def run_indexer():
        wi = wiT_ref[...]

        def chunk_scores(c):
            kic = kib_ref[c]
            acc = jnp.zeros((kc, Q_BLOCK), F32)
            for hp in range(IDX_HEADS // 2):
                r0 = 2 * hp * IDX_DIM
                rhs = jnp.concatenate([qiT_ref[r0:r0 + IDX_DIM, :], qiT_ref[r0 + IDX_DIM:r0 + 2 * IDX_DIM, :]], axis=1)
                s = jnp.maximum(jnp.dot(kic, rhs, preferred_element_type=F32), 0.0)
                acc = acc + s[:, :Q_BLOCK] * wi[2 * hp:2 * hp + 1, :] + s[:, Q_BLOCK:] * wi[2 * hp + 1:2 * hp + 2, :]
            kp = key_pos(c)
            valid = (kp >= off) & (kp <= qpos)
            sc_ref[c] = jnp.where(valid, acc, NEG_INF)

        def idx_body(t, carry):
            chunk_scores(2 * t)
            chunk_scores(jnp.minimum(2 * t + 1, n_chunks - 1))
            return carry

        lax.fori_loop(0, (n_chunks + 1) // 2, idx_body, 0)

    def chunk_counts(pred_c, accs):
        v = jnp.where(pred_c, 1.0, 0.0).reshape(COUNT_LANES, kc // (SUBLANES * COUNT_LANES), SUBLANES, Q_BLOCK)
        return tuple(a + jnp.sum(v[r], axis=0) for r, a in enumerate(accs))

    zero_accs = (jnp.zeros((SUBLANES, Q_BLOCK), F32),) * COUNT_LANES

    def total(accs):
        return jnp.sum(functools.reduce(lambda a, b: a + b, accs), axis=0, keepdims=True)

    def count(pred):
        return total(lax.fori_loop(0, n_chunks, lambda c, accs: chunk_counts(pred(c), accs), zero_accs))

    def search():
        def bit_body(t, res):
            trial = res | jnp.left_shift(jnp.int32(1), 31 - t)
            cand = _ukey_to_f32(trial)
            cnt = count(lambda c: sc_ref[c] >= cand)
            return jnp.where(cnt >= topk, trial, res)
        return lax.fori_loop(0, 32, bit_body, jnp.zeros((1, Q_BLOCK), jnp.int32))

    def finish_selection(res):
        few = (qpos - off + 1) <= topk
        thr = jnp.where(few, NEG_INF, _ukey_to_f32(res))
        cnt_gt = count(lambda c: sc_ref[c] > thr)
        cnt_ge = count(lambda c: sc_ref[c] >= thr)
        n_ties = topk - cnt_gt
        tie_rows = jnp.where(few, 0.0, jnp.where(cnt_ge > topk, 1.0, 0.0))
        j_ref[...] = jnp.full(j_ref.shape, 2 ** 30, jnp.int32)

        @pl.when(jnp.max(tie_rows) > 0.0)
        def _():
            def jbit_body(t, resj):
                trial = resj | jnp.left_shift(jnp.int32(1), n_pos_bits - 1 - t)
                below = count(lambda c: (sc_ref[c] == thr) & (key_pos(c) < trial))
                return jnp.where(below < n_ties, trial, resj)
            j_ref[...] = lax.fori_loop(0, n_pos_bits, jbit_body, jnp.zeros((1, Q_BLOCK), jnp.int32))

        jmax = j_ref[...]

        def bias_body(c, carry):
            sc = sc_ref[c]
            kp = key_pos(c)
            valid = (kp >= off) & (kp <= qpos)
            sel = (sc > thr) | ((sc == thr) & (kp <= jmax))
            sc_ref[c] = jnp.where(valid & sel, 0.0, NEG_INF)
            return carry

        lax.fori_loop(0, n_chunks, bias_body, 0)

    gw = GROUP * Q_BLOCK
    last = n_chunks - 1
    m_none = jnp.full((1, N_HEADS * Q_BLOCK), NEG_INF, F32)

    def score_stage(c, buf_ref, m_run):
        bias = jnp.concatenate([sc_ref[c]] * N_HEADS, axis=1)
        s = jnp.dot(kb_ref[c], qn_ref[...], preferred_element_type=F32) + bias
        buf_ref[...] = s
        return jnp.maximum(m_run, jnp.max(s, axis=0, keepdims=True))

    def prob_stage(c, buf_ref, m_before, m_with):
        m_safe = jnp.maximum(m_with, -1e30)
        alpha = jnp.exp2(m_before - m_safe)
        p = jnp.exp2(buf_ref[...] - m_safe).astype(BF16)
        vt = vT_ref[c]
        for n in range(N_KV_HEADS):
            cols = slice(n * gw, (n + 1) * gw)
            pv = jnp.dot(vt[n * LANES:(n + 1) * LANES, :], p[:, cols], preferred_element_type=F32)
            acc_ref[n] = acc_ref[n] * alpha[:, cols] + pv

    def attention_start():
        qn_ref[...] = jnp.zeros(qn_ref.shape, BF16)
        for h in range(N_HEADS):
            n = h // GROUP
            qn_ref[n * HEAD_DIM:(n + 1) * HEAD_DIM, h * Q_BLOCK:(h + 1) * Q_BLOCK] = qT_ref[h * HEAD_DIM:(h + 1) * HEAD_DIM, :]
        acc_ref[...] = jnp.zeros(acc_ref.shape, F32)
        return m_none, score_stage(0, sa_ref, m_none)

    def pair_step(t, m_prev, m_cur):
        c0 = 2 * t
        m_1 = score_stage(c0 + 1, sb_ref, m_cur)
        prob_stage(c0, sa_ref, m_prev, m_cur)
        m_2 = score_stage(c0 + 2, sa_ref, m_1)
        prob_stage(c0 + 1, sb_ref, m_cur, m_1)
        return m_1, m_2

    def attention_loop():
        n_pairs = (n_chunks - 1) // 2
        m_prev, m_cur = lax.fori_loop(0, n_pairs, lambda t, ms: pair_step(t, *ms), attention_start())
        mrun_ref[0] = m_prev
        mrun_ref[1] = m_cur
        c_a = 2 * n_pairs

        @pl.when(c_a == last)
        def _():
            prob_stage(c_a, sa_ref, mrun_ref[0], mrun_ref[1])

        @pl.when(c_a < last)
        def _():
            m_1 = score_stage(c_a + 1, sb_ref, mrun_ref[1])
            prob_stage(c_a, sa_ref, mrun_ref[0], mrun_ref[1])
            prob_stage(c_a + 1, sb_ref, mrun_ref[1], m_1)

    def attention_finish():
        heads = []
        for n in range(N_KV_HEADS):
            a = acc_ref[n]
            denom = a[HEAD_DIM:HEAD_DIM + 1, :]
            o = a[:HEAD_DIM, :] / jnp.where(denom > 0.0, denom, 1.0)
            for g in range(GROUP):
                heads.append(o[:, g * Q_BLOCK:(g + 1) * Q_BLOCK])
        att_ref[...] = jnp.concatenate(heads, axis=0).T.astype(BF16)

    run_indexer()
    finish_selection(search())
    attention_loop()
    attention_finish()


def _dsa_prompt(qT, qiT, wiT, kb3, kib3, vT3, *, batch, t_pad, topk, off):
    kc = KEY_CHUNK
    nqb = t_pad // Q_BLOCK
    ncb = t_pad // kc
    n_tok = batch * t_pad
    qcol = lambda n: pl.BlockSpec((n, Q_BLOCK), lambda b, i: (0, b * nqb + i))
    kern = functools.partial(_dsa_prompt_kernel, topk=topk, off=off,
                             n_pos_bits=max(1, math.ceil(math.log2(t_pad))))
    return pl.pallas_call(
        kern,
        grid=(batch, nqb),
        in_specs=[qcol(ATT_WIDTH), qcol(IDX_HEADS * IDX_DIM), qcol(IDX_HEADS),
                  pl.BlockSpec((ncb, kc, KV_WIDTH), lambda b, i: (b, 0, 0)),
                  pl.BlockSpec((ncb, kc, IDX_DIM), lambda b, i: (b, 0, 0)),
                  pl.BlockSpec((ncb, N_KV_HEADS * LANES, kc), lambda b, i: (b, 0, 0))],
        out_specs=pl.BlockSpec((Q_BLOCK, ATT_WIDTH), lambda b, i: (b * nqb + i, 0)),
        out_shape=jax.ShapeDtypeStruct((n_tok, ATT_WIDTH), BF16),
        scratch_shapes=[pltpu.VMEM((ncb, kc, Q_BLOCK), F32),
                        pltpu.VMEM((KV_WIDTH, N_HEADS * Q_BLOCK), BF16),
                        pltpu.VMEM((N_KV_HEADS, LANES, GROUP * Q_BLOCK), F32),
                        pltpu.VMEM((1, Q_BLOCK), jnp.int32),
                        pltpu.VMEM((kc, N_HEADS * Q_BLOCK), F32),
                        pltpu.VMEM((kc, N_HEADS * Q_BLOCK), F32),
                        pltpu.VMEM((2, 1, N_HEADS * Q_BLOCK), F32)],
        compiler_params=_cparams(2),
        name="dsa_prompt",
    )(qT, qiT, wiT, kb3, kib3, vT3)


def _idx_sample_kernel(pt_ref, qi_ref, wcol_ref, kinew_ref, cki_hbm, past_ref, new_ref, kibuf, sem):
    n_pages = kibuf.shape[1]
    s_q = new_ref.shape[0]
    b = pl.program_id(0)
    slot = b % 2

    def page_copies(seq, dst_slot):
        return [pltpu.make_async_copy(cki_hbm.at[pt_ref[seq, r]], kibuf.at[dst_slot, r], sem.at[dst_slot])
                for r in range(n_pages)]

    @pl.when(b == 0)
    def _():
        for cp in page_copies(0, 0):
            cp.start()

    @pl.when(b + 1 < pl.num_programs(0))
    def _():
        for cp in page_copies(b + 1, 1 - slot):
            cp.start()

    qi = qi_ref[...]

    def scores(keys_t_bf16):
        s = jnp.dot(qi, keys_t_bf16, preferred_element_type=F32)
        wcol = jnp.concatenate([wcol_ref[...]] * (s.shape[1] // LANES), axis=1)
        s = jnp.maximum(s, 0.0) * wcol
        return jnp.sum(s.reshape(IDX_HEADS, s_q, s.shape[1]), axis=0)

    s_new = scores(kinew_ref[...])
    qrow = lax.broadcasted_iota(jnp.int32, s_new.shape, 0)
    kcol = lax.broadcasted_iota(jnp.int32, s_new.shape, 1)
    new_ref[...] = jnp.where(kcol <= qrow, s_new, NEG_INF)

    for cp in page_copies(b, slot):
        cp.wait()
    npg = PAGES_PER_STEP
    for r0 in range(0, n_pages, IDX_PAGES_PER_DOT):
        keys = jnp.concatenate([kibuf[slot, r0 + j].astype(BF16) for j in range(IDX_PAGES_PER_DOT)], axis=1)
        c, p = divmod(r0, npg)
        past_ref[c, :, p * PAGE_SIZE:(p + IDX_PAGES_PER_DOT) * PAGE_SIZE] = scores(keys)


def _idx_sample(page_table, qi_stack, wcol, kinew, cache_kidx, *, s_q):
    db, n_pages = page_table.shape
    npg = PAGES_PER_STEP
    n_steps = n_pages // npg
    rows = IDX_HEADS * s_q
    per_seq = lambda r, n: pl.BlockSpec((None, r, n), lambda b, pt: (b, 0, 0))
    grid_spec = pltpu.PrefetchScalarGridSpec(
        num_scalar_prefetch=1,
        grid=(db,),
        in_specs=[per_seq(rows, IDX_DIM), per_seq(rows, LANES), per_seq(IDX_DIM, PAGE_SIZE),
                  pl.BlockSpec(memory_space=pl.ANY)],
        out_specs=(pl.BlockSpec((n_steps, None, s_q, npg * PAGE_SIZE), lambda b, pt: (0, b, 0, 0)),
                   per_seq(s_q, PAGE_SIZE)),
        scratch_shapes=[pltpu.VMEM((2, n_pages, IDX_DIM, PAGE_SIZE), F32), pltpu.SemaphoreType.DMA((2,))],
    )
    return pl.pallas_call(
        _idx_sample_kernel,
        grid_spec=grid_spec,
        out_shape=(jax.ShapeDtypeStruct((n_steps, db, s_q, npg * PAGE_SIZE), F32),
                   jax.ShapeDtypeStruct((db, s_q, PAGE_SIZE), F32)),
        compiler_params=_cparams(1),
        name="idx_sample",
    )(page_table, qi_stack, wcol, kinew, cache_kidx)


def _sel_sample_kernel(past_ref, new_ref, thr_ref, j_ref, *, topk, n_pos_bits):
    n_c, g, s_q, w = past_ref.shape
    rows = g * s_q
    l_past = n_c * w
    lane_pos = lax.broadcasted_iota(jnp.int32, (rows, LANES), 1)

    def lane_tile_sum(v):
        return functools.reduce(lambda a, b: a + b, [v[:, t * LANES:(t + 1) * LANES] for t in range(v.shape[1] // LANES)])

    def count(pred):
        def body(cc, acc):
            x = past_ref[cc].reshape(rows, w)
            pos = cc * w + lax.broadcasted_iota(jnp.int32, (rows, w), 1)
            return acc + lane_tile_sum(jnp.where(pred(x, pos), 1.0, 0.0))
        acc = lax.fori_loop(0, n_c, body, jnp.zeros((rows, LANES), F32))
        acc = acc + jnp.where(pred(new_ref[...].reshape(rows, PAGE_SIZE), l_past + lane_pos), 1.0, 0.0)
        return jnp.sum(acc, axis=1, keepdims=True)

    def bit_body(t, res):
        trial = res | jnp.left_shift(jnp.int32(1), 31 - t)
        cand = _ukey_to_f32(trial)
        return jnp.where(count(lambda x, pos: x >= cand) >= topk, trial, res)

    res = lax.fori_loop(0, 32, bit_body, jnp.zeros((rows, 1), jnp.int32))
    thr = _ukey_to_f32(res)
    n_ties = topk - count(lambda x, pos: x > thr)
    cnt_ge = count(lambda x, pos: x >= thr)
    thr_ref[...] = jnp.broadcast_to(thr, (rows, LANES))
    j_ref[...] = jnp.full((rows, LANES), 2 ** 30, jnp.int32)

    @pl.when(jnp.max(jnp.where(cnt_ge > topk, 1.0, 0.0)) > 0.0)
    def _():
        def jbit_body(t, resj):
            trial = resj | jnp.left_shift(jnp.int32(1), n_pos_bits - 1 - t)
            below = count(lambda x, pos: (x == thr) & (pos < trial))
            return jnp.where(below < n_ties, trial, resj)
        jmax = lax.fori_loop(0, n_pos_bits, jbit_body, jnp.zeros((rows, 1), jnp.int32))
        j_ref[...] = jnp.broadcast_to(jmax, (rows, LANES))


def _sel_sample(sc_past, sc_new, *, topk):
    n_c, db, s_q, w = sc_past.shape
    g = math.gcd(SEL_SEQS, db)
    kern = functools.partial(_sel_sample_kernel, topk=topk,
                             n_pos_bits=max(1, math.ceil(math.log2(n_c * w + PAGE_SIZE))))
    return pl.pallas_call(
        kern,
        grid=(db // g,),
        in_specs=[pl.BlockSpec((n_c, g, s_q, w), lambda i: (0, i, 0, 0)),
                  pl.BlockSpec((g, s_q, PAGE_SIZE), lambda i: (i, 0, 0))],
        out_specs=(pl.BlockSpec((g * s_q, LANES), lambda i: (i, 0)),
                   pl.BlockSpec((g * s_q, LANES), lambda i: (i, 0))),
        out_shape=(jax.ShapeDtypeStruct((db * s_q, LANES), F32),
                   jax.ShapeDtypeStruct((db * s_q, LANES), jnp.int32)),
        compiler_params=_cparams(1),
        name="sel_sample",
    )(sc_past, sc_new)


def _att_sample_kernel(pt_ref, qbd_ref, sc_past_ref, sc_new_ref, thr_ref, j_ref, knew_ref, vnew_ref, ck_hbm, cv_hbm,
                       att_ref, kbuf, vbuf, ksem, vsem, acc_ref, m_ref, l_ref):
    ring, npg = kbuf.shape[:2]
    ahead = ring - 1
    n_steps = sc_past_ref.shape[0]
    b = pl.program_id(0)
    n_seq = pl.num_programs(0)
    qbd = qbd_ref[...]
    thr = thr_ref[...]
    jmax = j_ref[...]

    def page_copies(seq, step, slot):
        cps = []
        for r in range(npg):
            page = pt_ref[seq, step * npg + r]
            cps.append(pltpu.make_async_copy(ck_hbm.at[page], kbuf.at[slot, r], ksem.at[slot]))
            cps.append(pltpu.make_async_copy(cv_hbm.at[page], vbuf.at[slot, r], vsem.at[slot]))
        return cps

    @pl.when(b == 0)
    def _():
        for g in range(ahead):
            for cp in page_copies(0, g, g):
                cp.start()

    acc_ref[...] = jnp.zeros(acc_ref.shape, F32)
    m_ref[...] = jnp.full(m_ref.shape, NEG_INF, F32)
    l_ref[...] = jnp.zeros(l_ref.shape, F32)

    def masked_scores(keys_t_bf16, sc, pos0):
        pos = pos0 + lax.broadcasted_iota(jnp.int32, sc.shape, 1)
        tile = lambda a: jnp.concatenate([a] * (sc.shape[1] // LANES), axis=1)
        sel = (sc > tile(thr)) | ((sc == tile(thr)) & (pos <= tile(jmax)))
        bias = jnp.where(sel, 0.0, NEG_INF)
        s = jnp.dot(qbd, keys_t_bf16, preferred_element_type=F32)
        return s + jnp.concatenate([bias] * N_HEADS, axis=0)

    def accumulate(s_list, vt_list):
        m_old = m_ref[...]
        m_new = jnp.maximum(m_old, jnp.max(functools.reduce(jnp.maximum, s_list), axis=1, keepdims=True))
        m_safe = jnp.maximum(m_new, -1e30)
        alpha = jnp.exp2(m_old - m_safe)
        acc = acc_ref[...] * alpha
        p_list = [jnp.exp2(s - m_safe) for s in s_list]
        for p, vt in zip(p_list, vt_list):
            acc = acc + lax.dot_general(p.astype(BF16), vt, NT_DIMS, preferred_element_type=F32)
        acc_ref[...] = acc
        l_ref[...] = l_ref[...] * alpha + jnp.sum(functools.reduce(lambda a, b: a + b, p_list), axis=1, keepdims=True)
        m_ref[...] = m_new

    for c in range(n_steps):
        group = b * n_steps + c
        slot = group % ring
        slot_ahead = (group + ahead) % ring
        if c + ahead < n_steps:
            for cp in page_copies(b, c + ahead, slot_ahead):
                cp.start()
        else:
            @pl.when(b + 1 < n_seq)
            def _():
                for cp in page_copies(b + 1, c + ahead - n_steps, slot_ahead):
                    cp.start()
        for cp in page_copies(b, c, slot):
            cp.wait()
        s_list, v_list = [], []
        for p in range(0, npg, 2):
            sc = sc_past_ref[c, :, p * PAGE_SIZE:(p + 2) * PAGE_SIZE]
            keys = jnp.concatenate([kbuf[slot, p].astype(BF16), kbuf[slot, p + 1].astype(BF16)], axis=1)
            s_list.append(masked_scores(keys, sc, (c * npg + p) * PAGE_SIZE))
            v_list.append(jnp.concatenate([vbuf[slot, p].astype(BF16), vbuf[slot, p + 1].astype(BF16)], axis=1))
        accumulate(s_list, v_list)

    accumulate([masked_scores(knew_ref[...], sc_new_ref[...], n_steps * npg * PAGE_SIZE)], [vnew_ref[...]])
    att_ref[...] = acc_ref[...] / l_ref[...]


def _att_sample(page_table, qbd, sc_past, sc_new, thr, jmax, knew, vnew, cache_k, cache_v, *, s_q):
    db, n_pages = page_table.shape
    npg = PAGES_PER_STEP
    n_steps = n_pages // npg
    rows = N_HEADS * s_q
    assert n_steps >= PAGE_RING - 1
    per_seq = lambda r, n: pl.BlockSpec((None, r, n), lambda b, pt: (b, 0, 0))
    page_buf = pltpu.VMEM((PAGE_RING, npg, KV_WIDTH, PAGE_SIZE), F32)
    ring_sem = pltpu.SemaphoreType.DMA((PAGE_RING,))
    grid_spec = pltpu.PrefetchScalarGridSpec(
        num_scalar_prefetch=1,
        grid=(db,),
        in_specs=[per_seq(rows, KV_WIDTH),
                  pl.BlockSpec((n_steps, None, s_q, npg * PAGE_SIZE), lambda b, pt: (0, b, 0, 0)),
                  per_seq(s_q, PAGE_SIZE),
                  pl.BlockSpec((s_q, LANES), lambda b, pt: (b, 0)),
                  pl.BlockSpec((s_q, LANES), lambda b, pt: (b, 0)),
                  per_seq(KV_WIDTH, PAGE_SIZE), per_seq(KV_WIDTH, PAGE_SIZE),
                  pl.BlockSpec(memory_space=pl.ANY), pl.BlockSpec(memory_space=pl.ANY)],
        out_specs=per_seq(rows, KV_WIDTH),
        scratch_shapes=[page_buf, page_buf, ring_sem, ring_sem,
                        pltpu.VMEM((rows, KV_WIDTH), F32), pltpu.VMEM((rows, 1), F32), pltpu.VMEM((rows, 1), F32)],
    )
    return pl.pallas_call(
        _att_sample_kernel,
        grid_spec=grid_spec,
        out_shape=jax.ShapeDtypeStruct((db, rows, KV_WIDTH), F32),
        compiler_params=_cparams(1),
        name="att_sample",
    )(page_table, qbd, sc_past, sc_new, thr, jmax, knew, vnew, cache_k, cache_v)


def _gla_kernel(*refs, chunk, n_chunks, off, t_end, has_s0):
    if has_s0:
        gq_ref, gk_ref, gv_ref, og_ref, la_ref, gn_ref, s0_ref, o_ref, sfin_ref, st_ref = refs
    else:
        gq_ref, gk_ref, gv_ref, og_ref, la_ref, gn_ref, o_ref, sfin_ref, st_ref = refs
    j = pl.program_id(1)
    tb = chunk * n_chunks

    @pl.when(j == 0)
    def _():
        for hh in range(GLA_HEADS):
            st_ref[hh] = s0_ref[hh].T if has_s0 else jnp.zeros((GLA_DV, GLA_DK), F32)

    r_i = lax.broadcasted_iota(jnp.int32, (chunk, chunk), 0)
    c_i = lax.broadcasted_iota(jnp.int32, (chunk, chunk), 1)
    causal = r_i >= c_i
    tril = jnp.where(causal, 1.0, 0.0)
    gn = gn_ref[...]

    def chunk_body(c, carry):
        r0 = pl.multiple_of(c * chunk, chunk)
        rows = pl.ds(r0, chunk)
        pos = j * tb + r0 + lax.broadcasted_iota(jnp.int32, (chunk, GLA_KW), 0)
        valid = (pos >= off) & (pos < t_end)
        la = jnp.where(valid, la_ref[rows, :], 0.0)
        k = jnp.where(valid, gk_ref[rows, :], 0.0)
        q = gq_ref[rows, :] * GLA_DK ** -0.5
        v = gv_ref[rows, :]
        b = jnp.dot(tril, la, preferred_element_type=F32, precision=lax.Precision.HIGHEST)
        b_last = b[chunk - 1:chunk, :]
        qd = (q * jnp.exp(b)).astype(BF16)
        kd = (k * jnp.exp(-b)).astype(BF16)
        ke = (k * jnp.exp(b_last - b)).astype(BF16)
        decay = jnp.exp(b_last)
        outs = []
        for hh in range(GLA_HEADS):
            ks = slice(hh * GLA_DK, (hh + 1) * GLA_DK)
            vh = v[:, hh * GLA_DV:(hh + 1) * GLA_DV]
            a = lax.dot_general(qd[:, ks], kd[:, ks], NT_DIMS, preferred_element_type=F32)
            a = jnp.where(causal, a, 0.0).astype(BF16)
            st = st_ref[hh]
            o = (jnp.dot(a, vh, preferred_element_type=F32)
                 + lax.dot_general(qd[:, ks], st.astype(BF16), NT_DIMS, preferred_element_type=F32))
            u_t = lax.dot_general(vh, ke[:, ks], TN_DIMS, preferred_element_type=F32)
            st_ref[hh] = decay[:, ks] * st + u_t
            o = o * lax.rsqrt(jnp.mean(o * o, axis=-1, keepdims=True) + LN_EPS) * gn
            outs.append(o)
        og = og_ref[rows, :]
        o_ref[rows, :] = (jnp.concatenate(outs, axis=1) * (og * jax.nn.sigmoid(og))).astype(BF16)
        return carry

    per_step = next(k for k in (6, 3, 2, 1) if n_chunks % k == 0)

    def step_body(t, carry):
        for k in range(per_step):
            chunk_body(per_step * t + k, carry)
        return carry

    lax.fori_loop(0, n_chunks // per_step, step_body, 0)

    @pl.when(j == pl.num_programs(1) - 1)
    def _():
        for hh in range(GLA_HEADS):
            sfin_ref[hh] = st_ref[hh].T


def _gla(gq, gk, gv, og, la, gnorm, s0, *, batch, t_pad, tb, chunk, off, t_end):
    n_steps = t_pad // tb
    tok = lambda n: pl.BlockSpec((tb, n), lambda b, j: (b * n_steps + j, 0))
    state = pl.BlockSpec((None, GLA_HEADS, GLA_DK, GLA_DV), lambda b, j: (b, 0, 0, 0))
    has_s0 = s0 is not None
    kern = functools.partial(_gla_kernel, chunk=chunk, n_chunks=tb // chunk, off=off, t_end=t_end, has_s0=has_s0)
    in_specs = [tok(GLA_KW), tok(GLA_KW), tok(GLA_VW), tok(GLA_VW), tok(GLA_KW), _full_spec(gnorm.shape)]
    args = [gq, gk, gv, og, la, gnorm]
    if has_s0:
        in_specs.append(state)
        args.append(s0)
    return pl.pallas_call(
        kern,
        grid=(batch, n_steps),
        in_specs=in_specs,
        out_specs=(tok(GLA_VW), state),
        out_shape=(jax.ShapeDtypeStruct((batch * t_pad, GLA_VW), BF16),
                   jax.ShapeDtypeStruct((batch, GLA_HEADS, GLA_DK, GLA_DV), F32)),
        scratch_shapes=[pltpu.VMEM((GLA_HEADS, GLA_DV, GLA_DK), F32)],
        compiler_params=_cparams(2),
        name="gla",
    )(*args)


def _merge_kernel(x_ref, att_ref, gla_ref, gates_ref, lng_ref, lnb_ref, wa_ref, wg_ref, wo_ref,
                  l1g_ref, l1b_ref, h1_ref, *, alpha):
    h = _layer_norm(x_ref[...], lng_ref[...], lnb_ref[...])
    pa = jnp.dot(att_ref[...], wa_ref[...], preferred_element_type=F32)
    pg = jnp.dot(gla_ref[...], wg_ref[...], preferred_element_type=F32)
    gates = gates_ref[...].astype(F32)
    merged = gates[:, :D_MODEL] * pa + gates[:, D_MODEL:] * pg
    mix = jnp.dot(merged.astype(BF16), wo_ref[...], preferred_element_type=F32)
    h1_ref[...] = _layer_norm(alpha * h + mix, l1g_ref[...], l1b_ref[...])


def _merge(x, att, gla, gates, ln_g, ln_b, wa, wg, wo, l1g, l1b, *, alpha):
    n_tok = x.shape[0]
    tm = _row_tile(n_tok, TM_TOKEN)
    row = lambda n: pl.BlockSpec((tm, n), lambda i: (i, 0))
    consts = (ln_g, ln_b, wa, wg, wo, l1g, l1b)
    return pl.pallas_call(
        functools.partial(_merge_kernel, alpha=alpha),
        grid=(n_tok // tm,),
        in_specs=[row(D_MODEL), row(ATT_WIDTH), row(GLA_VW), row(N_BRANCHES * D_MODEL)]
                 + [_full_spec(a.shape) for a in consts],
        out_specs=row(D_MODEL),
        out_shape=jax.ShapeDtypeStruct((n_tok, D_MODEL), F32),
        compiler_params=_cparams(1),
        name="merge",
    )(x, att, gla, gates, *consts)


def _ffn_kernel(h_ref, wu_ref, wd_ref, g_ref, b_ref, y_ref, *, alpha, n_split):
    h = h_ref[...]
    hb = h.astype(BF16)
    w = D_FF // n_split
    ff = jnp.zeros(h.shape, F32)
    for s in range(n_split):
        u = jnp.dot(hb, wu_ref[:, s * w:(s + 1) * w], preferred_element_type=F32)
        u = jnp.square(jnp.maximum(u, 0.0)).astype(BF16)
        ff = ff + jnp.dot(u, wd_ref[s * w:(s + 1) * w, :], preferred_element_type=F32)
    y_ref[...] = _layer_norm(alpha * h + ff, g_ref[...], b_ref[...])


def _ffn(h1, wu, wd, g, b, *, alpha):
    n_tok = h1.shape[0]
    tm = _row_tile(n_tok, TM_TOKEN)
    row = pl.BlockSpec((tm, D_MODEL), lambda i: (i, 0))
    return pl.pallas_call(
        functools.partial(_ffn_kernel, alpha=alpha, n_split=4),
        grid=(n_tok // tm,),
        in_specs=[row] + [_full_spec(a.shape) for a in (wu, wd, g, b)],
        out_specs=row,
        out_shape=jax.ShapeDtypeStruct((n_tok, D_MODEL), F32),
        compiler_params=_cparams(1),
        name="ffn",
    )(h1, wu, wd, g, b)


def _ffn_window(h1, wu, wd, g, b, *, alpha, start, length):
    batch, rows, _ = h1.shape
    tm = _row_tile(length, TM_TOKEN)
    consts = (wu, wd, g, b)
    assert rows % SUBLANES == 0 and start % SUBLANES == 0
    first_row = lambda bi, j: pl.multiple_of(bi * rows + start + j * tm, SUBLANES)
    return pl.pallas_call(
        functools.partial(_ffn_kernel, alpha=alpha, n_split=4),
        grid=(batch, length // tm),
        in_specs=[pl.BlockSpec((pl.Element(tm), pl.Element(D_MODEL)), lambda bi, j: (first_row(bi, j), 0))]
                 + [_full_spec(a.shape) for a in consts],
        out_specs=pl.BlockSpec((None, tm, D_MODEL), lambda bi, j: (bi, j, 0)),
        out_shape=jax.ShapeDtypeStruct((batch, length, D_MODEL), F32),
        compiler_params=_cparams(2),
        name="ffn_window",
    )(h1.reshape(batch * rows, D_MODEL), *consts)


def _pack_weights(w_in, w_gla_a2, b_gla_a):
    points = []
    acc = 0
    for s in IN_SIZES[:-1]:
        acc += s
        points.append(acc)
    wq, wk, wv, wqi, wki, wwi, wgq, wgk, wgv, wog, wa1, wgt = jnp.split(w_in, points, axis=-1)
    wq = wq * (math.log2(math.e) * HEAD_DIM ** -0.5)
    wqi = wqi * IDX_DIM ** -0.5
    pad_cols = lambda a, n: jnp.pad(a, ((0, 0), (0, n - a.shape[1])))
    wv_heads = wv.T.reshape(N_KV_HEADS, HEAD_DIM, D_MODEL)
    wv_aug = jnp.pad(wv_heads, ((0, 0), (0, LANES - HEAD_DIM), (0, 0))).reshape(N_KV_HEADS * LANES, D_MODEL)
    w = {
        "qT": wq.T, "qiT": wqi.T, "vT": wv_aug,
        "wiT": jnp.pad(wwi.T, ((0, 2 * SUBLANES - IDX_HEADS), (0, 0))),
        "row_p": pad_cols(jnp.concatenate([wk, wv, wki], axis=1), 2 * KV_WIDTH + LANES),
        "row_s": pad_cols(jnp.concatenate([wq, wqi, wk, wv, wki, wwi], axis=1),
                          ATT_WIDTH + IDX_HEADS * IDX_DIM + 2 * KV_WIDTH + LANES),
        "gla": jnp.concatenate([wgq, wgk, wgv, wog], axis=1),
        "a1": pad_cols(wa1, LANES),
        "a2": jnp.pad(w_gla_a2, ((0, LANES - GLA_GATE_RANK), (0, 0))),
        "gt": wgt,
    }
    w = {name: a.astype(BF16) for name, a in w.items()}
    w["ba"] = b_gla_a.astype(F32)[None]
    return w


def _round_up(x, m):
    return -(-x // m) * m


def _row_tile(n, pref, unit=2 * SUBLANES):
    best = unit
    for t in range(unit, min(n, pref) + 1, unit):
        if n % t == 0:
            best = t
    assert n % best == 0
    return best


def kernel(x_prompt, x_sample, cache_k, cache_v, cache_kidx, state_gla, page_table, meta_tokens, ln_in_g, ln_in_b, w_in, w_gla_a2, b_gla_a, gla_norm_g, w_proj_attn, w_proj_gla, w_out, ln1_g, ln1_b, w_ff_up, w_ff_down, ln2_g, ln2_b):
    depth = w_in.shape[0]
    assert depth == 1, "single-layer step only"
    B, S_p, D = x_prompt.shape
    DB, S_s, _ = x_sample.shape
    n_pages = page_table.shape[1]
    past = n_pages * PAGE_SIZE
    assert D == D_MODEL and S_p % GLA_CHUNK == 0 and S_s <= SUBLANES and n_pages % PAGES_PER_STEP == 0
    topk_prompt = min(TOPK_MAX, S_p // 4)
    topk_sample = min(TOPK_MAX, (past + S_s) // 4)
    alpha = (2 * depth) ** 0.25

    row = lambda a: a.astype(F32).reshape(1, -1)
    ln_g, ln_b = row(ln_in_g), row(ln_in_b)
    w = _pack_weights(w_in[0], w_gla_a2[0], b_gla_a[0])
    wa, wg, wo = (a[0].astype(BF16) for a in (w_proj_attn, w_proj_gla, w_out))
    wu, wd = w_ff_up[0].astype(BF16), w_ff_down[0].astype(BF16)
    gnorm = row(gla_norm_g[0])
    l1g, l1b, l2g, l2b = row(ln1_g[0]), row(ln1_b[0]), row(ln2_g[0]), row(ln2_b[0])

    T = S_p + N_META
    off = (-N_META) % GLA_CHUNK
    t_pad = _round_up(off + T, math.lcm(KEY_CHUNK, GLA_CHUNK))
    gla_tb = _row_tile(t_pad, GLA_TOKENS_PER_STEP, GLA_CHUNK)
    n_tok = B * t_pad
    meta = jnp.broadcast_to(meta_tokens.astype(x_prompt.dtype)[None], (B, N_META, D))
    xp = jnp.concatenate([jnp.zeros((B, off, D), x_prompt.dtype), meta, x_prompt,
                          jnp.zeros((B, t_pad - off - T, D), x_prompt.dtype)], axis=1).reshape(n_tok, D)

    qT, qiT, vT3, wiT, k32, v32, ki32, kb, kib = _proj_attn_prompt(xp, ln_g, ln_b, w, n_tok)
    att_p = _dsa_prompt(qT, qiT, wiT,
                        kb.reshape(n_tok // KEY_CHUNK, KEY_CHUNK, KV_WIDTH),
                        kib.reshape(n_tok // KEY_CHUNK, KEY_CHUNK, IDX_DIM), vT3,
                        batch=B, t_pad=t_pad, topk=topk_prompt, off=off)
    gq, gk, gv, og, la, gates_p = _proj_gla(xp, ln_g, ln_b, w, n_tok)
    gla_p, state_p = _gla(gq, gk, gv, og, la, gnorm, None, batch=B, t_pad=t_pad, tb=gla_tb,
                          chunk=GLA_CHUNK, off=off, t_end=off + T)
    h1_p = _merge(xp, att_p, gla_p, gates_p, ln_g, ln_b, wa, wg, wo, l1g, l1b, alpha=alpha)
    seq = lambda a: a.reshape((B, t_pad) + a.shape[1:])
    y_prompt = _ffn_window(seq(h1_p), wu, wd, l2g, l2b, alpha=alpha, start=off + N_META, length=S_p)

    k_prompt = seq(k32)[:, off:off + T].reshape(1, B, T, N_KV_HEADS, HEAD_DIM)
    v_prompt = seq(v32)[:, off:off + T].reshape(1, B, T, N_KV_HEADS, HEAD_DIM)
    kidx_prompt = seq(ki32)[:, off:off + T][None]
    gla_state_prompt = state_p[None]

    R = SAMPLE_ROWS
    n_tok_s = DB * R
    xs = jnp.pad(x_sample, ((0, 0), (0, R - S_s), (0, 0))).reshape(n_tok_s, D)
    q_s, qi_s, k_s, v_s, kiwi_s = _proj_attn_sample(xs, ln_g, ln_b, w, n_tok_s)
    sseq = lambda a: a.reshape((DB, R) + a.shape[1:])[:, :S_s]
    k_new, v_new = sseq(k_s), sseq(v_s)
    ki_new = sseq(kiwi_s)[..., :IDX_DIM]
    wi_new = sseq(kiwi_s)[..., IDX_DIM:IDX_DIM + IDX_HEADS]
    qi_stack = sseq(qi_s).reshape(DB, S_s, IDX_HEADS, IDX_DIM).transpose(0, 2, 1, 3).reshape(DB, IDX_HEADS * S_s, IDX_DIM)
    wcol = jnp.broadcast_to(wi_new.transpose(0, 2, 1).reshape(DB, IDX_HEADS * S_s, 1), (DB, IDX_HEADS * S_s, LANES))
    q_heads = sseq(q_s).reshape(DB, S_s, N_HEADS, HEAD_DIM).transpose(0, 2, 1, 3)
    kv_of_head = (jnp.arange(N_HEADS) // GROUP)[:, None] == jnp.arange(N_KV_HEADS)[None, :]
    qbd = jnp.where(kv_of_head[None, :, None, :, None], q_heads[:, :, :, None, :], jnp.zeros((), BF16))
    qbd = qbd.reshape(DB, N_HEADS * S_s, KV_WIDTH)
    pad_page = lambda a: jnp.pad(a, ((0, 0), (0, PAGE_SIZE - S_s), (0, 0))).astype(BF16).transpose(0, 2, 1)
    kinew_pg, knew_pg, vnew_pg = pad_page(ki_new), pad_page(k_new), pad_page(v_new)
    ck = cache_k[0].transpose(0, 2, 3, 1).reshape(-1, KV_WIDTH, PAGE_SIZE)
    cv = cache_v[0].transpose(0, 2, 3, 1).reshape(-1, KV_WIDTH, PAGE_SIZE)
    cki = cache_kidx[0].transpose(0, 2, 1)
    sc_past, sc_new = _idx_sample(page_table, qi_stack, wcol, kinew_pg, cki, s_q=S_s)
    thr, jmax = _sel_sample(sc_past, sc_new, topk=topk_sample)
    o_s = _att_sample(page_table, qbd, sc_past, sc_new, thr, jmax, knew_pg, vnew_pg, ck, cv, s_q=S_s)
    o_s = o_s.reshape(DB, N_KV_HEADS, GROUP, S_s, N_KV_HEADS, HEAD_DIM)
    att_s = jnp.stack([o_s[:, n, :, :, n, :] for n in range(N_KV_HEADS)], axis=1)
    att_s = att_s.transpose(0, 3, 1, 2, 4).reshape(DB, S_s, ATT_WIDTH).astype(BF16)
    att_s = jnp.pad(att_s, ((0, 0), (0, R - S_s), (0, 0))).reshape(n_tok_s, ATT_WIDTH)

    gq, gk, gv, og, la, gates_s = _proj_gla(xs, ln_g, ln_b, w, n_tok_s)
    gla_s, state_s = _gla(gq, gk, gv, og, la, gnorm, state_gla[0], batch=DB, t_pad=R, tb=R,
                          chunk=R, off=0, t_end=S_s)
    h1_s = _merge(xs, att_s, gla_s, gates_s, ln_g, ln_b, wa, wg, wo, l1g, l1b, alpha=alpha)
    y_s = _ffn(h1_s, wu, wd, l2g, l2b, alpha=alpha)

    y_sample = y_s.reshape(DB, R, D)[:, :S_s]
    k_sample = k_new.reshape(1, DB, S_s, N_KV_HEADS, HEAD_DIM)
    v_sample = v_new.reshape(1, DB, S_s, N_KV_HEADS, HEAD_DIM)
    kidx_sample = ki_new[None]
    gla_state_sample = state_s[None]
    return (y_prompt, y_sample, k_prompt, v_prompt, kidx_prompt, gla_state_prompt,
            k_sample, v_sample, kidx_sample, gla_state_sample)
```

```python
import functools
import math

import jax
import jax.numpy as jnp
from jax import lax
from jax.experimental import pallas as pl
from jax.experimental.pallas import tpu as pltpu

D_MODEL = 1024
PAGE_SIZE = 128
N_META = 16
N_HEADS = 16
HEAD_DIM = 64
N_KV_HEADS = 4
GROUP = N_HEADS // N_KV_HEADS
ATT_WIDTH = N_HEADS * HEAD_DIM
KV_WIDTH = N_KV_HEADS * HEAD_DIM
IDX_HEADS = 8
IDX_DIM = 64
TOPK_MAX = 256
GLA_HEADS = 4
GLA_DK = D_MODEL // 2 // GLA_HEADS
GLA_DV = D_MODEL // GLA_HEADS
GLA_KW = GLA_HEADS * GLA_DK
GLA_VW = GLA_HEADS * GLA_DV
GLA_GATE_RANK = 16
GLA_TAU = 16.0
GLA_CHUNK = 64
N_BRANCHES = 2
D_FF = 4 * D_MODEL
LN_EPS = 1e-5
IN_SIZES = (ATT_WIDTH, KV_WIDTH, KV_WIDTH, IDX_HEADS * IDX_DIM, IDX_DIM, IDX_HEADS,
            GLA_KW, GLA_KW, GLA_VW, GLA_VW, GLA_GATE_RANK, N_BRANCHES * D_MODEL)

LANES = 128
SUBLANES = 8
VMEM_LIMIT_BYTES = 56 * 1024 * 1024
PLANE_KEYS = 32
COUNT_LANES = 4
Q_BLOCK = LANES
KEY_CHUNK = 3 * LANES
SAMPLE_ROWS = 16
PAGES_PER_STEP = 16
PAGE_RING = 3
IDX_PAGES_PER_DOT = 4
SEL_SEQS = 16
TM_PROJ_GLA = 512
PROJ_ATTN_CHUNKS = 2
TM_TOKEN = 512
GLA_TOKENS_PER_STEP = 6 * GLA_CHUNK

F32 = jnp.float32
BF16 = jnp.bfloat16
NEG_INF = float("-inf")
INT_MIN = -2 ** 31
NT_DIMS = (((1,), (1,)), ((), ()))
TN_DIMS = (((0,), (0,)), ((), ()))


def _cparams(n_grid):
    return pltpu.CompilerParams(dimension_semantics=("arbitrary",) * n_grid,
                                vmem_limit_bytes=VMEM_LIMIT_BYTES)


def _full_spec(shape):
    nd = len(shape)
    return pl.BlockSpec(shape, lambda *_: (0,) * nd, pipeline_mode=pl.Buffered(1))


def _layer_norm(x, g, b):
    mu = jnp.mean(x, axis=-1, keepdims=True)
    xc = x - mu
    var = jnp.mean(xc * xc, axis=-1, keepdims=True)
    return xc * lax.rsqrt(var + LN_EPS) * g + b


def _bit_planes(words):
    a = list(words)
    for dist, mask in ((16, 0x0000FFFF), (8, 0x00FF00FF), (4, 0x0F0F0F0F), (2, 0x33333333), (1, 0x55555555)):
        k = 0
        while k < PLANE_KEYS:
            t = (a[k] ^ lax.shift_right_logical(a[k + dist], jnp.int32(dist))) & jnp.int32(mask)
            a[k] = a[k] ^ t
            a[k + dist] = a[k + dist] ^ lax.shift_left(t, jnp.int32(dist))
            k = (k + dist + 1) & ~dist
    return a


def _ukey_to_f32(u):
    bits = jnp.where(u < 0, u & jnp.int32(0x7FFFFFFF), ~u)
    return lax.bitcast_convert_type(bits, F32)


def _proj_attn_prompt_kernel(x_ref, g_ref, b_ref, wq_ref, wqi_ref, wv_ref, wwi_ref, wrow_ref,
                             qT_ref, qiT_ref, vT_ref, wiT_ref, k_ref, v_ref, ki_ref, kb_ref, kib_ref):
    hb = _layer_norm(x_ref[...], g_ref[...], b_ref[...]).astype(BF16)
    qT_ref[...] = lax.dot_general(wq_ref[...], hb, NT_DIMS, preferred_element_type=F32).astype(BF16)
    qiT_ref[...] = lax.dot_general(wqi_ref[...], hb, NT_DIMS, preferred_element_type=F32).astype(BF16)
    vt = lax.dot_general(wv_ref[...], hb, NT_DIMS, preferred_element_type=F32)
    row = lax.broadcasted_iota(jnp.int32, vt.shape, 0)
    vt = jnp.where(row % LANES == HEAD_DIM, 1.0, vt)
    for j in range(vT_ref.shape[0]):
        vT_ref[j] = vt[:, j * KEY_CHUNK:(j + 1) * KEY_CHUNK].astype(BF16)
    wi = lax.dot_general(wwi_ref[...], hb, NT_DIMS, preferred_element_type=F32)
    wiT_ref[...] = wi[:IDX_HEADS] * IDX_HEADS ** -0.5
    y = jnp.dot(hb, wrow_ref[...], preferred_element_type=F32)
    k = y[:, :KV_WIDTH]
    ki = y[:, 2 * KV_WIDTH:2 * KV_WIDTH + IDX_DIM]
    k_ref[...] = k
    v_ref[...] = y[:, KV_WIDTH:2 * KV_WIDTH]
    ki_ref[...] = ki
    kb_ref[...] = k.astype(BF16)
    kib_ref[...] = ki.astype(BF16)


def _proj_attn_prompt(x, ln_g, ln_b, w, n_tok):
    tm = _row_tile(n_tok, PROJ_ATTN_CHUNKS * KEY_CHUNK, KEY_CHUNK)
    n_steps = n_tok // tm
    cps = tm // KEY_CHUNK
    row = lambda n: pl.BlockSpec((tm, n), lambda i: (i, 0))
    col = lambda n: pl.BlockSpec((n, tm), lambda i: (0, i))
    out_shape = (
        jax.ShapeDtypeStruct((ATT_WIDTH, n_tok), BF16),
        jax.ShapeDtypeStruct((IDX_HEADS * IDX_DIM, n_tok), BF16),
        jax.ShapeDtypeStruct((n_tok // KEY_CHUNK, N_KV_HEADS * LANES, KEY_CHUNK), BF16),
        jax.ShapeDtypeStruct((IDX_HEADS, n_tok), F32),
        jax.ShapeDtypeStruct((n_tok, KV_WIDTH), F32),
        jax.ShapeDtypeStruct((n_tok, KV_WIDTH), F32),
        jax.ShapeDtypeStruct((n_tok, IDX_DIM), F32),
        jax.ShapeDtypeStruct((n_tok, KV_WIDTH), BF16),
        jax.ShapeDtypeStruct((n_tok, IDX_DIM), BF16),
    )
    out_specs = (col(ATT_WIDTH), col(IDX_HEADS * IDX_DIM),
                 pl.BlockSpec((cps, N_KV_HEADS * LANES, KEY_CHUNK), lambda i: (i, 0, 0)),
                 col(IDX_HEADS), row(KV_WIDTH), row(KV_WIDTH), row(IDX_DIM), row(KV_WIDTH), row(IDX_DIM))
    ws = (w["qT"], w["qiT"], w["vT"], w["wiT"], w["row_p"])
    return pl.pallas_call(
        _proj_attn_prompt_kernel,
        grid=(n_steps,),
        in_specs=[row(D_MODEL), _full_spec(ln_g.shape), _full_spec(ln_b.shape)] + [_full_spec(a.shape) for a in ws],
        out_specs=out_specs,
        out_shape=out_shape,
        compiler_params=_cparams(1),
        name="proj_attn_prompt",
    )(x, ln_g, ln_b, *ws)


def _proj_attn_sample_kernel(x_ref, g_ref, b_ref, w_ref, scale_ref, q_ref, qi_ref, k_ref, v_ref, kiwi_ref):
    hb = _layer_norm(x_ref[...], g_ref[...], b_ref[...]).astype(BF16)
    y = jnp.dot(hb, w_ref[...], preferred_element_type=F32)
    o = 0
    q_ref[...] = y[:, o:o + ATT_WIDTH].astype(BF16)
    o += ATT_WIDTH
    qi_ref[...] = y[:, o:o + IDX_HEADS * IDX_DIM].astype(BF16)
    o += IDX_HEADS * IDX_DIM
    k_ref[...] = y[:, o:o + KV_WIDTH]
    o += KV_WIDTH
    v_ref[...] = y[:, o:o + KV_WIDTH]
    o += KV_WIDTH
    kiwi_ref[...] = y[:, o:o + LANES] * scale_ref[...]


def _proj_attn_sample(x, ln_g, ln_b, w, n_tok):
    tm = _row_tile(n_tok, TM_TOKEN)
    row = lambda n: pl.BlockSpec((tm, n), lambda i: (i, 0))
    lane = lax.iota(jnp.int32, LANES)
    scale = jnp.where((lane >= IDX_DIM) & (lane < IDX_DIM + IDX_HEADS), IDX_HEADS ** -0.5, 1.0).astype(F32)[None]
    out_shape = (
        jax.ShapeDtypeStruct((n_tok, ATT_WIDTH), BF16),
        jax.ShapeDtypeStruct((n_tok, IDX_HEADS * IDX_DIM), BF16),
        jax.ShapeDtypeStruct((n_tok, KV_WIDTH), F32),
        jax.ShapeDtypeStruct((n_tok, KV_WIDTH), F32),
        jax.ShapeDtypeStruct((n_tok, LANES), F32),
    )
    return pl.pallas_call(
        _proj_attn_sample_kernel,
        grid=(n_tok // tm,),
        in_specs=[row(D_MODEL), _full_spec(ln_g.shape), _full_spec(ln_b.shape),
                  _full_spec(w["row_s"].shape), _full_spec(scale.shape)],
        out_specs=(row(ATT_WIDTH), row(IDX_HEADS * IDX_DIM), row(KV_WIDTH), row(KV_WIDTH), row(LANES)),
        out_shape=out_shape,
        compiler_params=_cparams(1),
        name="proj_attn_sample",
    )(x, ln_g, ln_b, w["row_s"], scale)


def _log_sigmoid(x):
    return jnp.minimum(x, 0.0) - jnp.log1p(jnp.exp(-jnp.abs(x)))


def _proj_gla_kernel(x_ref, g_ref, b_ref, wg_ref, wa1_ref, wa2_ref, ba_ref, wgt_ref,
                     gq_ref, gk_ref, gv_ref, og_ref, la_ref, gates_ref):
    hb = _layer_norm(x_ref[...], g_ref[...], b_ref[...]).astype(BF16)
    y = jnp.dot(hb, wg_ref[...], preferred_element_type=F32)
    gq_ref[...] = y[:, :GLA_KW]
    gk_ref[...] = y[:, GLA_KW:2 * GLA_KW]
    gv_ref[...] = y[:, 2 * GLA_KW:2 * GLA_KW + GLA_VW].astype(BF16)
    og_ref[...] = y[:, 2 * GLA_KW + GLA_VW:]
    a1 = jnp.dot(hb, wa1_ref[...], preferred_element_type=F32).astype(BF16)
    z = jnp.dot(a1, wa2_ref[...], preferred_element_type=F32) + ba_ref[...]
    la_ref[...] = _log_sigmoid(z) / GLA_TAU
    gt = jnp.dot(hb, wgt_ref[...], preferred_element_type=F32)
    gates_ref[...] = jax.nn.sigmoid(gt).astype(BF16)


def _proj_gla(x, ln_g, ln_b, w, n_tok):
    tm = _row_tile(n_tok, TM_PROJ_GLA)
    row = lambda n: pl.BlockSpec((tm, n), lambda i: (i, 0))
    ws = (w["gla"], w["a1"], w["a2"], w["ba"], w["gt"])
    out_shape = (
        jax.ShapeDtypeStruct((n_tok, GLA_KW), F32),
        jax.ShapeDtypeStruct((n_tok, GLA_KW), F32),
        jax.ShapeDtypeStruct((n_tok, GLA_VW), BF16),
        jax.ShapeDtypeStruct((n_tok, GLA_VW), F32),
        jax.ShapeDtypeStruct((n_tok, GLA_KW), F32),
        jax.ShapeDtypeStruct((n_tok, N_BRANCHES * D_MODEL), BF16),
    )
    return pl.pallas_call(
        _proj_gla_kernel,
        grid=(n_tok // tm,),
        in_specs=[row(D_MODEL), _full_spec(ln_g.shape), _full_spec(ln_b.shape)] + [_full_spec(a.shape) for a in ws],
        out_specs=(row(GLA_KW), row(GLA_KW), row(GLA_VW), row(GLA_VW), row(GLA_KW), row(N_BRANCHES * D_MODEL)),
        out_shape=out_shape,
        compiler_params=_cparams(1),
        name="proj_gla",
    )(x, ln_g, ln_b, *ws)


def _dsa_prompt_kernel(qT_ref, qiT_ref, wiT_ref, kb_ref, kib_ref, vT_ref, att_ref,
                       sc_ref, planes_ref, qn_ref, acc_ref, j_ref, sa_ref, sb_ref, mrun_ref, *, topk, off, n_pos_bits):
    kc = KEY_CHUNK
    groups_per_chunk = planes_ref.shape[0] // sc_ref.shape[0]
    i = pl.program_id(1)

    @pl.when((pl.program_id(0) == 0) & (i == 0))
    def _():
        planes_ref[...] = jnp.zeros(planes_ref.shape, jnp.int32)

    n_chunks = (i * Q_BLOCK + Q_BLOCK + kc - 1) // kc
    qpos = i * Q_BLOCK + lax.broadcasted_iota(jnp.int32, (1, Q_BLOCK), 1)

    def key_pos(c):
        return c * kc + lax.broadcasted_iota(jnp.int32, (kc, Q_BLOCK), 0)

    def run_indexer():
        wi = wiT_ref[...]

        def chunk_scores(c):
            kic = kib_ref[c]
            acc = jnp.zeros((kc, Q_BLOCK), F32)
            for hp in range(IDX_HEADS // 2):
                r0 = 2 * hp * IDX_DIM
                rhs = jnp.concatenate([qiT_ref[r0:r0 + IDX_DIM, :], qiT_ref[r0 + IDX_DIM:r0 + 2 * IDX_DIM, :]], axis=1)
                s = jnp.maximum(jnp.dot(kic, rhs, preferred_element_type=F32), 0.0)
                acc = acc + s[:, :Q_BLOCK] * wi[2 * hp:2 * hp + 1, :] + s[:, Q_BLOCK:] * wi[2 * hp + 1:2 * hp + 2, :]
            kp = key_pos(c)
            valid = (kp >= off) & (kp <= qpos)
            sc = jnp.where(valid, acc, NEG_INF)
            sc_ref[c] = sc
            bits = lax.bitcast_convert_type(sc, jnp.int32)
            ukey = (bits ^ ((bits >> 31) | jnp.int32(INT_MIN))).reshape(kc // SUBLANES, SUBLANES, Q_BLOCK)
            no_key = jnp.zeros((SUBLANES, Q_BLOCK), jnp.int32)
            for g in range(groups_per_chunk):
                words = [ukey[g * PLANE_KEYS + k] if g * PLANE_KEYS + k < kc // SUBLANES else no_key
                         for k in range(PLANE_KEYS)]
                for t, plane in enumerate(_bit_planes(words)):
                    planes_ref[groups_per_chunk * c + g, t] = plane

        def idx_body(t, carry):
            chunk_scores(2 * t)
            chunk_scores(jnp.minimum(2 * t + 1, n_chunks - 1))
            return carry

        lax.fori_loop(0, (n_chunks + 1) // 2, idx_body, 0)

    def chunk_counts(pred_c, accs):
        v = jnp.where(pred_c, 1.0, 0.0).reshape(COUNT_LANES, kc // (SUBLANES * COUNT_LANES), SUBLANES, Q_BLOCK)
        return tuple(a + jnp.sum(v[r], axis=0) for r, a in enumerate(accs))

    zero_accs = (jnp.zeros((SUBLANES, Q_BLOCK), F32),) * COUNT_LANES

    def total(accs):
        return jnp.sum(functools.reduce(lambda a, b: a + b, accs), axis=0, keepdims=True)

    def count(pred):
        return total(lax.fori_loop(0, n_chunks, lambda c, accs: chunk_counts(pred(c), accs), zero_accs))

    def search():
        n_groups = planes_ref.shape[0]

        def step(t, carry):
            res, want, alive = carry
            planes = [planes_ref[g, t] for g in range(n_groups)]
            ones = functools.reduce(lambda x, y: x + y, [lax.population_count(a & p) for a, p in zip(alive, planes)])
            cnt = jnp.sum(ones, axis=0, keepdims=True)
            take = cnt >= want
            res = jnp.where(take, res | jnp.left_shift(jnp.int32(1), 31 - t), res)
            want = jnp.where(take, want, want - cnt)
            keep = jnp.where(take, jnp.int32(0), jnp.int32(-1))
            return res, want, tuple(a & (p ^ keep) for a, p in zip(alive, planes))

        alive = tuple(jnp.full((SUBLANES, Q_BLOCK), jnp.where(g // groups_per_chunk < n_chunks, -1, 0), jnp.int32)
                      for g in range(n_groups))
        start = (jnp.zeros((1, Q_BLOCK), jnp.int32), jnp.full((1, Q_BLOCK), topk, jnp.int32), alive)
        return lax.fori_loop(0, 32, step, start)[0]

    def finish_selection(res):
        few = (qpos - off + 1) <= topk
        thr = jnp.where(few, NEG_INF, _ukey_to_f32(res))
        cnt_gt = count(lambda c: sc_ref[c] > thr)
        cnt_ge = count(lambda c: sc_ref[c] >= thr)
        n_ties = topk - cnt_gt
        tie_rows = jnp.where(few, 0.0, jnp.where(cnt_ge > topk, 1.0, 0.0))
        j_ref[...] = jnp.full(j_ref.shape, 2 ** 30, jnp.int32)

        @pl.when(jnp.max(tie_rows) > 0.0)
        def _():
            def jbit_body(t, resj):
                trial = resj | jnp.left_shift(jnp.int32(1), n_pos_bits - 1 - t)
                below = count(lambda c: (sc_ref[c] == thr) & (key_pos(c) < trial))
                return jnp.where(below < n_ties, trial, resj)
            j_ref[...] = lax.fori_loop(0, n_pos_bits, jbit_body, jnp.zeros((1, Q_BLOCK), jnp.int32))

        jmax = j_ref[...]

        def bias_body(c, carry):
            sc = sc_ref[c]
            kp = key_pos(c)
            valid = (kp >= off) & (kp <= qpos)
            sel = (sc > thr) | ((sc == thr) & (kp <= jmax))
            sc_ref[c] = jnp.where(valid & sel, 0.0, NEG_INF)
            return carry

        lax.fori_loop(0, n_chunks, bias_body, 0)

    gw = GROUP * Q_BLOCK
    last = n_chunks - 1
    m_none = jnp.full((1, N_HEADS * Q_BLOCK), NEG_INF, F32)

    def score_stage(c, buf_ref, m_run):
        bias = jnp.concatenate([sc_ref[c]] * N_HEADS, axis=1)
        s = jnp.dot(kb_ref[c], qn_ref[...], preferred_element_type=F32) + bias
        buf_ref[...] = s
        return jnp.maximum(m_run, jnp.max(s, axis=0, keepdims=True))

    def prob_stage(c, buf_ref, m_before, m_with):
        m_safe = jnp.maximum(m_with, -1e30)
        alpha = jnp.exp2(m_before - m_safe)
        p = jnp.exp2(buf_ref[...] - m_safe).astype(BF16)
        vt = vT_ref[c]
        for n in range(N_KV_HEADS):
            cols = slice(n * gw, (n + 1) * gw)
            pv = jnp.dot(vt[n * LANES:(n + 1) * LANES, :], p[:, cols], preferred_element_type=F32)
            acc_ref[n] = acc_ref[n] * alpha[:, cols] + pv

    def attention_start():
        qn_ref[...] = jnp.zeros(qn_ref.shape, BF16)
        for h in range(N_HEADS):
            n = h // GROUP
            qn_ref[n * HEAD_DIM:(n + 1) * HEAD_DIM, h * Q_BLOCK:(h + 1) * Q_BLOCK] = qT_ref[h * HEAD_DIM:(h + 1) * HEAD_DIM, :]
        acc_ref[...] = jnp.zeros(acc_ref.shape, F32)
        return m_none, score_stage(0, sa_ref, m_none)

    def pair_step(t, m_prev, m_cur):
        c0 = 2 * t
        m_1 = score_stage(c0 + 1, sb_ref, m_cur)
        prob_stage(c0, sa_ref, m_prev, m_cur)
        m_2 = score_stage(c0 + 2, sa_ref, m_1)
        prob_stage(c0 + 1, sb_ref, m_cur, m_1)
        return m_1, m_2

    def attention_loop():
        n_pairs = (n_chunks - 1) // 2
        m_prev, m_cur = lax.fori_loop(0, n_pairs, lambda t, ms: pair_step(t, *ms), attention_start())
        mrun_ref[0] = m_prev
        mrun_ref[1] = m_cur
        c_a = 2 * n_pairs

        @pl.when(c_a == last)
        def _():
            prob_stage(c_a, sa_ref, mrun_ref[0], mrun_ref[1])

        @pl.when(c_a < last)
        def _():
            m_1 = score_stage(c_a + 1, sb_ref, mrun_ref[1])
            prob_stage(c_a, sa_ref, mrun_ref[0], mrun_ref[1])
            prob_stage(c_a + 1, sb_ref, mrun_ref[1], m_1)

    def attention_finish():
        heads = []
        for n in range(N_KV_HEADS):
            a = acc_ref[n]
            denom = a[HEAD_DIM:HEAD_DIM + 1, :]
            o = a[:HEAD_DIM, :] / jnp.where(denom > 0.0, denom, 1.0)
            for g in range(GROUP):
                heads.append(o[:, g * Q_BLOCK:(g + 1) * Q_BLOCK])
        att_ref[...] = jnp.concatenate(heads, axis=0).T.astype(BF16)

    run_indexer()
    finish_selection(search())
    attention_loop()
    attention_finish()


def _dsa_prompt(qT, qiT, wiT, kb3, kib3, vT3, *, batch, t_pad, topk, off):
    kc = KEY_CHUNK
    nqb = t_pad // Q_BLOCK
    ncb = t_pad // kc
    n_tok = batch * t_pad
    qcol = lambda n: pl.BlockSpec((n, Q_BLOCK), lambda b, i: (0, b * nqb + i))
    kern = functools.partial(_dsa_prompt_kernel, topk=topk, off=off,
                             n_pos_bits=max(1, math.ceil(math.log2(t_pad))))
    return pl.pallas_call(
        kern,
        grid=(batch, nqb),
        in_specs=[qcol(ATT_WIDTH), qcol(IDX_HEADS * IDX_DIM), qcol(IDX_HEADS),
                  pl.BlockSpec((ncb, kc, KV_WIDTH), lambda b, i: (b, 0, 0)),
                  pl.BlockSpec((ncb, kc, IDX_DIM), lambda b, i: (b, 0, 0)),
                  pl.BlockSpec((ncb, N_KV_HEADS * LANES, kc), lambda b, i: (b, 0, 0))],
        out_specs=pl.BlockSpec((Q_BLOCK, ATT_WIDTH), lambda b, i: (b * nqb + i, 0)),
        out_shape=jax.ShapeDtypeStruct((n_tok, ATT_WIDTH), BF16),
        scratch_shapes=[pltpu.VMEM((ncb, kc, Q_BLOCK), F32),
                        pltpu.VMEM((ncb * -(-kc // (SUBLANES * PLANE_KEYS)), PLANE_KEYS, SUBLANES, Q_BLOCK), jnp.int32),
                        pltpu.VMEM((KV_WIDTH, N_HEADS * Q_BLOCK), BF16),
                        pltpu.VMEM((N_KV_HEADS, LANES, GROUP * Q_BLOCK), F32),
                        pltpu.VMEM((1, Q_BLOCK), jnp.int32),
                        pltpu.VMEM((kc, N_HEADS * Q_BLOCK), F32),
                        pltpu.VMEM((kc, N_HEADS * Q_BLOCK), F32),
                        pltpu.VMEM((2, 1, N_HEADS * Q_BLOCK), F32)],
        compiler_params=_cparams(2),
        name="dsa_prompt",
    )(qT, qiT, wiT, kb3, kib3, vT3)


def _idx_sample_kernel(pt_ref, qi_ref, wcol_ref, kinew_ref, cki_hbm, past_ref, new_ref, kibuf, sem):
    n_pages = kibuf.shape[1]
    s_q = new_ref.shape[0]
    b = pl.program_id(0)
    slot = b % 2

    def page_copies(seq, dst_slot):
        return [pltpu.make_async_copy(cki_hbm.at[pt_ref[seq, r]], kibuf.at[dst_slot, r], sem.at[dst_slot])
                for r in range(n_pages)]

    @pl.when(b == 0)
    def _():
        for cp in page_copies(0, 0):
            cp.start()

    @pl.when(b + 1 < pl.num_programs(0))
    def _():
        for cp in page_copies(b + 1, 1 - slot):
            cp.start()

    qi = qi_ref[...]

    def scores(keys_t_bf16):
        s = jnp.dot(qi, keys_t_bf16, preferred_element_type=F32)
        wcol = jnp.concatenate([wcol_ref[...]] * (s.shape[1] // LANES), axis=1)
        s = jnp.maximum(s, 0.0) * wcol
        return jnp.sum(s.reshape(IDX_HEADS, s_q, s.shape[1]), axis=0)

    s_new = scores(kinew_ref[...])
    qrow = lax.broadcasted_iota(jnp.int32, s_new.shape, 0)
    kcol = lax.broadcasted_iota(jnp.int32, s_new.shape, 1)
    new_ref[...] = jnp.where(kcol <= qrow, s_new, NEG_INF)

    for cp in page_copies(b, slot):
        cp.wait()
    npg = PAGES_PER_STEP
    for r0 in range(0, n_pages, IDX_PAGES_PER_DOT):
        keys = jnp.concatenate([kibuf[slot, r0 + j].astype(BF16) for j in range(IDX_PAGES_PER_DOT)], axis=1)
        c, p = divmod(r0, npg)
        past_ref[c, :, p * PAGE_SIZE:(p + IDX_PAGES_PER_DOT) * PAGE_SIZE] = scores(keys)


def _idx_sample(page_table, qi_stack, wcol, kinew, cache_kidx, *, s_q):
    db, n_pages = page_table.shape
    npg = PAGES_PER_STEP
    n_steps = n_pages // npg
    rows = IDX_HEADS * s_q
    per_seq = lambda r, n: pl.BlockSpec((None, r, n), lambda b, pt: (b, 0, 0))
    grid_spec = pltpu.PrefetchScalarGridSpec(
        num_scalar_prefetch=1,
        grid=(db,),
        in_specs=[per_seq(rows, IDX_DIM), per_seq(rows, LANES), per_seq(IDX_DIM, PAGE_SIZE),
                  pl.BlockSpec(memory_space=pl.ANY)],
        out_specs=(pl.BlockSpec((n_steps, None, s_q, npg * PAGE_SIZE), lambda b, pt: (0, b, 0, 0)),
                   per_seq(s_q, PAGE_SIZE)),
        scratch_shapes=[pltpu.VMEM((2, n_pages, IDX_DIM, PAGE_SIZE), F32), pltpu.SemaphoreType.DMA((2,))],
    )
    return pl.pallas_call(
        _idx_sample_kernel,
        grid_spec=grid_spec,
        out_shape=(jax.ShapeDtypeStruct((n_steps, db, s_q, npg * PAGE_SIZE), F32),
                   jax.ShapeDtypeStruct((db, s_q, PAGE_SIZE), F32)),
        compiler_params=_cparams(1),
        name="idx_sample",
    )(page_table, qi_stack, wcol, kinew, cache_kidx)


def _sel_sample_kernel(past_ref, new_ref, thr_ref, j_ref, *, topk, n_pos_bits):
    n_c, g, s_q, w = past_ref.shape
    rows = g * s_q
    l_past = n_c * w
    lane_pos = lax.broadcasted_iota(jnp.int32, (rows, LANES), 1)

    def lane_tile_sum(v):
        return functools.reduce(lambda a, b: a + b, [v[:, t * LANES:(t + 1) * LANES] for t in range(v.shape[1] // LANES)])

    def count(pred):
        def body(cc, acc):
            x = past_ref[cc].reshape(rows, w)
            pos = cc * w + lax.broadcasted_iota(jnp.int32, (rows, w), 1)
            return acc + lane_tile_sum(jnp.where(pred(x, pos), 1.0, 0.0))
        acc = lax.fori_loop(0, n_c, body, jnp.zeros((rows, LANES), F32))
        acc = acc + jnp.where(pred(new_ref[...].reshape(rows, PAGE_SIZE), l_past + lane_pos), 1.0, 0.0)
        return jnp.sum(acc, axis=1, keepdims=True)

    def bit_body(t, res):
        trial = res | jnp.left_shift(jnp.int32(1), 31 - t)
        cand = _ukey_to_f32(trial)
        return jnp.where(count(lambda x, pos: x >= cand) >= topk, trial, res)

    res = lax.fori_loop(0, 32, bit_body, jnp.zeros((rows, 1), jnp.int32))
    thr = _ukey_to_f32(res)
    n_ties = topk - count(lambda x, pos: x > thr)
    cnt_ge = count(lambda x, pos: x >= thr)
    thr_ref[...] = jnp.broadcast_to(thr, (rows, LANES))
    j_ref[...] = jnp.full((rows, LANES), 2 ** 30, jnp.int32)

    @pl.when(jnp.max(jnp.where(cnt_ge > topk, 1.0, 0.0)) > 0.0)
    def _():
        def jbit_body(t, resj):
            trial = resj | jnp.left_shift(jnp.int32(1), n_pos_bits - 1 - t)
            below = count(lambda x, pos: (x == thr) & (pos < trial))
            return jnp.where(below < n_ties, trial, resj)
        jmax = lax.fori_loop(0, n_pos_bits, jbit_body, jnp.zeros((rows, 1), jnp.int32))
        j_ref[...] = jnp.broadcast_to(jmax, (rows, LANES))


def _sel_sample(sc_past, sc_new, *, topk):
    n_c, db, s_q, w = sc_past.shape
    g = math.gcd(SEL_SEQS, db)
    kern = functools.partial(_sel_sample_kernel, topk=topk,
                             n_pos_bits=max(1, math.ceil(math.log2(n_c * w + PAGE_SIZE))))
    return pl.pallas_call(
        kern,
        grid=(db // g,),
        in_specs=[pl.BlockSpec((n_c, g, s_q, w), lambda i: (0, i, 0, 0)),
                  pl.BlockSpec((g, s_q, PAGE_SIZE), lambda i: (i, 0, 0))],
        out_specs=(pl.BlockSpec((g * s_q, LANES), lambda i: (i, 0)),
                   pl.BlockSpec((g * s_q, LANES), lambda i: (i, 0))),
        out_shape=(jax.ShapeDtypeStruct((db * s_q, LANES), F32),
                   jax.ShapeDtypeStruct((db * s_q, LANES), jnp.int32)),
        compiler_params=_cparams(1),
        name="sel_sample",
    )(sc_past, sc_new)


def _att_sample_kernel(pt_ref, qbd_ref, sc_past_ref, sc_new_ref, thr_ref, j_ref, knew_ref, vnew_ref, ck_hbm, cv_hbm,
                       att_ref, kbuf, vbuf, ksem, vsem, acc_ref, m_ref, l_ref):
    ring, npg = kbuf.shape[:2]
    ahead = ring - 1
    n_steps = sc_past_ref.shape[0]
    b = pl.program_id(0)
    n_seq = pl.num_programs(0)
    qbd = qbd_ref[...]
    thr = thr_ref[...]
    jmax = j_ref[...]

    def page_copies(seq, step, slot):
        cps = []
        for r in range(npg):
            page = pt_ref[seq, step * npg + r]
            cps.append(pltpu.make_async_copy(ck_hbm.at[page], kbuf.at[slot, r], ksem.at[slot]))
            cps.append(pltpu.make_async_copy(cv_hbm.at[page], vbuf.at[slot, r], vsem.at[slot]))
        return cps

    @pl.when(b == 0)
    def _():
        for g in range(ahead):
            for cp in page_copies(0, g, g):
                cp.start()

    acc_ref[...] = jnp.zeros(acc_ref.shape, F32)
    m_ref[...] = jnp.full(m_ref.shape, NEG_INF, F32)
    l_ref[...] = jnp.zeros(l_ref.shape, F32)

    def masked_scores(keys_t_bf16, sc, pos0):
        pos = pos0 + lax.broadcasted_iota(jnp.int32, sc.shape, 1)
        tile = lambda a: jnp.concatenate([a] * (sc.shape[1] // LANES), axis=1)
        sel = (sc > tile(thr)) | ((sc == tile(thr)) & (pos <= tile(jmax)))
        bias = jnp.where(sel, 0.0, NEG_INF)
        s = jnp.dot(qbd, keys_t_bf16, preferred_element_type=F32)
        return s + jnp.concatenate([bias] * N_HEADS, axis=0)

    def accumulate(s_list, vt_list):
        m_old = m_ref[...]
        m_new = jnp.maximum(m_old, jnp.max(functools.reduce(jnp.maximum, s_list), axis=1, keepdims=True))
        m_safe = jnp.maximum(m_new, -1e30)
        alpha = jnp.exp2(m_old - m_safe)
        acc = acc_ref[...] * alpha
        p_list = [jnp.exp2(s - m_safe) for s in s_list]
        for p, vt in zip(p_list, vt_list):
            acc = acc + lax.dot_general(p.astype(BF16), vt, NT_DIMS, preferred_element_type=F32)
        acc_ref[...] = acc
        l_ref[...] = l_ref[...] * alpha + jnp.sum(functools.reduce(lambda a, b: a + b, p_list), axis=1, keepdims=True)
        m_ref[...] = m_new

    for c in range(n_steps):
        group = b * n_steps + c
        slot = group % ring
        slot_ahead = (group + ahead) % ring
        if c + ahead < n_steps:
            for cp in page_copies(b, c + ahead, slot_ahead):
                cp.start()
        else:
            @pl.when(b + 1 < n_seq)
            def _():
                for cp in page_copies(b + 1, c + ahead - n_steps, slot_ahead):
                    cp.start()
        for cp in page_copies(b, c, slot):
            cp.wait()
        s_list, v_list = [], []
        for p in range(0, npg, 2):
            sc = sc_past_ref[c, :, p * PAGE_SIZE:(p + 2) * PAGE_SIZE]
            keys = jnp.concatenate([kbuf[slot, p].astype(BF16), kbuf[slot, p + 1].astype(BF16)], axis=1)
            s_list.append(masked_scores(keys, sc, (c * npg + p) * PAGE_SIZE))
            v_list.append(jnp.concatenate([vbuf[slot, p].astype(BF16), vbuf[slot, p + 1].astype(BF16)], axis=1))
        accumulate(s_list, v_list)

    accumulate([masked_scores(knew_ref[...], sc_new_ref[...], n_steps * npg * PAGE_SIZE)], [vnew_ref[...]])
    att_ref[...] = acc_ref[...] / l_ref[...]


def _att_sample(page_table, qbd, sc_past, sc_new, thr, jmax, knew, vnew, cache_k, cache_v, *, s_q):
    db, n_pages = page_table.shape
    npg = PAGES_PER_STEP
    n_steps = n_pages // npg
    rows = N_HEADS * s_q
    assert n_steps >= PAGE_RING - 1
    per_seq = lambda r, n: pl.BlockSpec((None, r, n), lambda b, pt: (b, 0, 0))
    page_buf = pltpu.VMEM((PAGE_RING, npg, KV_WIDTH, PAGE_SIZE), F32)
    ring_sem = pltpu.SemaphoreType.DMA((PAGE_RING,))
    grid_spec = pltpu.PrefetchScalarGridSpec(
        num_scalar_prefetch=1,
        grid=(db,),
        in_specs=[per_seq(rows, KV_WIDTH),
                  pl.BlockSpec((n_steps, None, s_q, npg * PAGE_SIZE), lambda b, pt: (0, b, 0, 0)),
                  per_seq(s_q, PAGE_SIZE),
                  pl.BlockSpec((s_q, LANES), lambda b, pt: (b, 0)),
                  pl.BlockSpec((s_q, LANES), lambda b, pt: (b, 0)),
                  per_seq(KV_WIDTH, PAGE_SIZE), per_seq(KV_WIDTH, PAGE_SIZE),
                  pl.BlockSpec(memory_space=pl.ANY), pl.BlockSpec(memory_space=pl.ANY)],
        out_specs=per_seq(rows, KV_WIDTH),
        scratch_shapes=[page_buf, page_buf, ring_sem, ring_sem,
                        pltpu.VMEM((rows, KV_WIDTH), F32), pltpu.VMEM((rows, 1), F32), pltpu.VMEM((rows, 1), F32)],
    )
    return pl.pallas_call(
        _att_sample_kernel,
        grid_spec=grid_spec,
        out_shape=jax.ShapeDtypeStruct((db, rows, KV_WIDTH), F32),
        compiler_params=_cparams(1),
        name="att_sample",
    )(page_table, qbd, sc_past, sc_new, thr, jmax, knew, vnew, cache_k, cache_v)


def _gla_kernel(*refs, chunk, n_chunks, off, t_end, has_s0):
    if has_s0:
        gq_ref, gk_ref, gv_ref, og_ref, la_ref, gn_ref, s0_ref, o_ref, sfin_ref, st_ref = refs
    else:
        gq_ref, gk_ref, gv_ref, og_ref, la_ref, gn_ref, o_ref, sfin_ref, st_ref = refs
    j = pl.program_id(1)
    tb = chunk * n_chunks

    @pl.when(j == 0)
    def _():
        for hh in range(GLA_HEADS):
            st_ref[hh] = s0_ref[hh].T if has_s0 else jnp.zeros((GLA_DV, GLA_DK), F32)

    r_i = lax.broadcasted_iota(jnp.int32, (chunk, chunk), 0)
    c_i = lax.broadcasted_iota(jnp.int32, (chunk, chunk), 1)
    causal = r_i >= c_i
    tril = jnp.where(causal, 1.0, 0.0)
    gn = gn_ref[...]

    def chunk_body(c, carry):
        r0 = pl.multiple_of(c * chunk, chunk)
        rows = pl.ds(r0, chunk)
        pos = j * tb + r0 + lax.broadcasted_iota(jnp.int32, (chunk, GLA_KW), 0)
        valid = (pos >= off) & (pos < t_end)
        la = jnp.where(valid, la_ref[rows, :], 0.0)
        k = jnp.where(valid, gk_ref[rows, :], 0.0)
        q = gq_ref[rows, :] * GLA_DK ** -0.5
        v = gv_ref[rows, :]
        b = jnp.dot(tril, la, preferred_element_type=F32, precision=lax.Precision.HIGHEST)
        b_last = b[chunk - 1:chunk, :]
        qd = (q * jnp.exp(b)).astype(BF16)
        kd = (k * jnp.exp(-b)).astype(BF16)
        ke = (k * jnp.exp(b_last - b)).astype(BF16)
        decay = jnp.exp(b_last)
        outs = []
        for hh in range(GLA_HEADS):
            ks = slice(hh * GLA_DK, (hh + 1) * GLA_DK)
            vh = v[:, hh * GLA_DV:(hh + 1) * GLA_DV]
            a = lax.dot_general(qd[:, ks], kd[:, ks], NT_DIMS, preferred_element_type=F32)
            a = jnp.where(causal, a, 0.0).astype(BF16)
            st = st_ref[hh]
            o = (jnp.dot(a, vh, preferred_element_type=F32)
                 + lax.dot_general(qd[:, ks], st.astype(BF16), NT_DIMS, preferred_element_type=F32))
            u_t = lax.dot_general(vh, ke[:, ks], TN_DIMS, preferred_element_type=F32)
            st_ref[hh] = decay[:, ks] * st + u_t
            o = o * lax.rsqrt(jnp.mean(o * o, axis=-1, keepdims=True) + LN_EPS) * gn
            outs.append(o)
        og = og_ref[rows, :]
        o_ref[rows, :] = (jnp.concatenate(outs, axis=1) * (og * jax.nn.sigmoid(og))).astype(BF16)
        return carry

    per_step = next(k for k in (6, 3, 2, 1) if n_chunks % k == 0)

    def step_body(t, carry):
        for k in range(per_step):
            chunk_body(per_step * t + k, carry)
        return carry

    lax.fori_loop(0, n_chunks // per_step, step_body, 0)

    @pl.when(j == pl.num_programs(1) - 1)
    def _():
        for hh in range(GLA_HEADS):
            sfin_ref[hh] = st_ref[hh].T


def _gla(gq, gk, gv, og, la, gnorm, s0, *, batch, t_pad, tb, chunk, off, t_end):
    n_steps = t_pad // tb
    tok = lambda n: pl.BlockSpec((tb, n), lambda b, j: (b * n_steps + j, 0))
    state = pl.BlockSpec((None, GLA_HEADS, GLA_DK, GLA_DV), lambda b, j: (b, 0, 0, 0))
    has_s0 = s0 is not None
    kern = functools.partial(_gla_kernel, chunk=chunk, n_chunks=tb // chunk, off=off, t_end=t_end, has_s0=has_s0)
    in_specs = [tok(GLA_KW), tok(GLA_KW), tok(GLA_VW), tok(GLA_VW), tok(GLA_KW), _full_spec(gnorm.shape)]
    args = [gq, gk, gv, og, la, gnorm]
    if has_s0:
        in_specs.append(state)
        args.append(s0)
    return pl.pallas_call(
        kern,
        grid=(batch, n_steps),
        in_specs=in_specs,
        out_specs=(tok(GLA_VW), state),
        out_shape=(jax.ShapeDtypeStruct((batch * t_pad, GLA_VW), BF16),
                   jax.ShapeDtypeStruct((batch, GLA_HEADS, GLA_DK, GLA_DV), F32)),
        scratch_shapes=[pltpu.VMEM((GLA_HEADS, GLA_DV, GLA_DK), F32)],
        compiler_params=_cparams(2),
        name="gla",
    )(*args)


def _merge_kernel(x_ref, att_ref, gla_ref, gates_ref, lng_ref, lnb_ref, wa_ref, wg_ref, wo_ref,
                  l1g_ref, l1b_ref, h1_ref, *, alpha):
    h = _layer_norm(x_ref[...], lng_ref[...], lnb_ref[...])
    pa = jnp.dot(att_ref[...], wa_ref[...], preferred_element_type=F32)
    pg = jnp.dot(gla_ref[...], wg_ref[...], preferred_element_type=F32)
    gates = gates_ref[...].astype(F32)
    merged = gates[:, :D_MODEL] * pa + gates[:, D_MODEL:] * pg
    mix = jnp.dot(merged.astype(BF16), wo_ref[...], preferred_element_type=F32)
    h1_ref[...] = _layer_norm(alpha * h + mix, l1g_ref[...], l1b_ref[...])


def _merge(x, att, gla, gates, ln_g, ln_b, wa, wg, wo, l1g, l1b, *, alpha):
    n_tok = x.shape[0]
    tm = _row_tile(n_tok, TM_TOKEN)
    row = lambda n: pl.BlockSpec((tm, n), lambda i: (i, 0))
    consts = (ln_g, ln_b, wa, wg, wo, l1g, l1b)
    return pl.pallas_call(
        functools.partial(_merge_kernel, alpha=alpha),
        grid=(n_tok // tm,),
        in_specs=[row(D_MODEL), row(ATT_WIDTH), row(GLA_VW), row(N_BRANCHES * D_MODEL)]
                 + [_full_spec(a.shape) for a in consts],
        out_specs=row(D_MODEL),
        out_shape=jax.ShapeDtypeStruct((n_tok, D_MODEL), F32),
        compiler_params=_cparams(1),
        name="merge",
    )(x, att, gla, gates, *consts)


def _ffn_kernel(h_ref, wu_ref, wd_ref, g_ref, b_ref, y_ref, *, alpha, n_split):
    h = h_ref[...]
    hb = h.astype(BF16)
    w = D_FF // n_split
    ff = jnp.zeros(h.shape, F32)
    for s in range(n_split):
        u = jnp.dot(hb, wu_ref[:, s * w:(s + 1) * w], preferred_element_type=F32)
        u = jnp.square(jnp.maximum(u, 0.0)).astype(BF16)
        ff = ff + jnp.dot(u, wd_ref[s * w:(s + 1) * w, :], preferred_element_type=F32)
    y_ref[...] = _layer_norm(alpha * h + ff, g_ref[...], b_ref[...])


def _ffn(h1, wu, wd, g, b, *, alpha):
    n_tok = h1.shape[0]
    tm = _row_tile(n_tok, TM_TOKEN)
    row = pl.BlockSpec((tm, D_MODEL), lambda i: (i, 0))
    return pl.pallas_call(
        functools.partial(_ffn_kernel, alpha=alpha, n_split=4),
        grid=(n_tok // tm,),
        in_specs=[row] + [_full_spec(a.shape) for a in (wu, wd, g, b)],
        out_specs=row,
        out_shape=jax.ShapeDtypeStruct((n_tok, D_MODEL), F32),
        compiler_params=_cparams(1),
        name="ffn",
    )(h1, wu, wd, g, b)


def _ffn_window(h1, wu, wd, g, b, *, alpha, start, length):
    batch, rows, _ = h1.shape
    tm = _row_tile(length, TM_TOKEN)
    consts = (wu, wd, g, b)
    assert rows % SUBLANES == 0 and start % SUBLANES == 0
    first_row = lambda bi, j: pl.multiple_of(bi * rows + start + j * tm, SUBLANES)
    return pl.pallas_call(
        functools.partial(_ffn_kernel, alpha=alpha, n_split=4),
        grid=(batch, length // tm),
        in_specs=[pl.BlockSpec((pl.Element(tm), pl.Element(D_MODEL)), lambda bi, j: (first_row(bi, j), 0))]
                 + [_full_spec(a.shape) for a in consts],
        out_specs=pl.BlockSpec((None, tm, D_MODEL), lambda bi, j: (bi, j, 0)),
        out_shape=jax.ShapeDtypeStruct((batch, length, D_MODEL), F32),
        compiler_params=_cparams(2),
        name="ffn_window",
    )(h1.reshape(batch * rows, D_MODEL), *consts)


def _pack_weights(w_in, w_gla_a2, b_gla_a):
    points = []
    acc = 0
    for s in IN_SIZES[:-1]:
        acc += s
        points.append(acc)
    wq, wk, wv, wqi, wki, wwi, wgq, wgk, wgv, wog, wa1, wgt = jnp.split(w_in, points, axis=-1)
    wq = wq * (math.log2(math.e) * HEAD_DIM ** -0.5)
    wqi = wqi * IDX_DIM ** -0.5
    pad_cols = lambda a, n: jnp.pad(a, ((0, 0), (0, n - a.shape[1])))
    wv_heads = wv.T.reshape(N_KV_HEADS, HEAD_DIM, D_MODEL)
    wv_aug = jnp.pad(wv_heads, ((0, 0), (0, LANES - HEAD_DIM), (0, 0))).reshape(N_KV_HEADS * LANES, D_MODEL)
    w = {
        "qT": wq.T, "qiT": wqi.T, "vT": wv_aug,
        "wiT": jnp.pad(wwi.T, ((0, 2 * SUBLANES - IDX_HEADS), (0, 0))),
        "row_p": pad_cols(jnp.concatenate([wk, wv, wki], axis=1), 2 * KV_WIDTH + LANES),
        "row_s": pad_cols(jnp.concatenate([wq, wqi, wk, wv, wki, wwi], axis=1),
                          ATT_WIDTH + IDX_HEADS * IDX_DIM + 2 * KV_WIDTH + LANES),
        "gla": jnp.concatenate([wgq, wgk, wgv, wog], axis=1),
        "a1": pad_cols(wa1, LANES),
        "a2": jnp.pad(w_gla_a2, ((0, LANES - GLA_GATE_RANK), (0, 0))),
        "gt": wgt,
    }
    w = {name: a.astype(BF16) for name, a in w.items()}
    w["ba"] = b_gla_a.astype(F32)[None]
    return w


def _round_up(x, m):
    return -(-x // m) * m


def _row_tile(n, pref, unit=2 * SUBLANES):
    best = unit
    for t in range(unit, min(n, pref) + 1, unit):
        if n % t == 0:
            best = t
    assert n % best == 0
    return best


def kernel(x_prompt, x_sample, cache_k, cache_v, cache_kidx, state_gla, page_table, meta_tokens, ln_in_g, ln_in_b, w_in, w_gla_a2, b_gla_a, gla_norm_g, w_proj_attn, w_proj_gla, w_out, ln1_g, ln1_b, w_ff_up, w_ff_down, ln2_g, ln2_b):
    depth = w_in.shape[0]
    assert depth == 1, "single-layer step only"
    B, S_p, D = x_prompt.shape
    DB, S_s, _ = x_sample.shape
    n_pages = page_table.shape[1]
    past = n_pages * PAGE_SIZE
    assert D == D_MODEL and S_p % GLA_CHUNK == 0 and S_s <= SUBLANES and n_pages % PAGES_PER_STEP == 0
    topk_prompt = min(TOPK_MAX, S_p // 4)
    topk_sample = min(TOPK_MAX, (past + S_s) // 4)
    alpha = (2 * depth) ** 0.25

    row = lambda a: a.astype(F32).reshape(1, -1)
    ln_g, ln_b = row(ln_in_g), row(ln_in_b)
    w = _pack_weights(w_in[0], w_gla_a2[0], b_gla_a[0])
    wa, wg, wo = (a[0].astype(BF16) for a in (w_proj_attn, w_proj_gla, w_out))
    wu, wd = w_ff_up[0].astype(BF16), w_ff_down[0].astype(BF16)
    gnorm = row(gla_norm_g[0])
    l1g, l1b, l2g, l2b = row(ln1_g[0]), row(ln1_b[0]), row(ln2_g[0]), row(ln2_b[0])

    T = S_p + N_META
    off = (-N_META) % GLA_CHUNK
    t_pad = _round_up(off + T, math.lcm(KEY_CHUNK, GLA_CHUNK))
    gla_tb = _row_tile(t_pad, GLA_TOKENS_PER_STEP, GLA_CHUNK)
    n_tok = B * t_pad
    meta = jnp.broadcast_to(meta_tokens.astype(x_prompt.dtype)[None], (B, N_META, D))
    xp = jnp.concatenate([jnp.zeros((B, off, D), x_prompt.dtype), meta, x_prompt,
                          jnp.zeros((B, t_pad - off - T, D), x_prompt.dtype)], axis=1).reshape(n_tok, D)

    qT, qiT, vT3, wiT, k32, v32, ki32, kb, kib = _proj_attn_prompt(xp, ln_g, ln_b, w, n_tok)
    att_p = _dsa_prompt(qT, qiT, wiT,
                        kb.reshape(n_tok // KEY_CHUNK, KEY_CHUNK, KV_WIDTH),
                        kib.reshape(n_tok // KEY_CHUNK, KEY_CHUNK, IDX_DIM), vT3,
                        batch=B, t_pad=t_pad, topk=topk_prompt, off=off)
    gq, gk, gv, og, la, gates_p = _proj_gla(xp, ln_g, ln_b, w, n_tok)
    gla_p, state_p = _gla(gq, gk, gv, og, la, gnorm, None, batch=B, t_pad=t_pad, tb=gla_tb,
                          chunk=GLA_CHUNK, off=off, t_end=off + T)
    h1_p = _merge(xp, att_p, gla_p, gates_p, ln_g, ln_b, wa, wg, wo, l1g, l1b, alpha=alpha)
    seq = lambda a: a.reshape((B, t_pad) + a.shape[1:])
    y_prompt = _ffn_window(seq(h1_p), wu, wd, l2g, l2b, alpha=alpha, start=off + N_META, length=S_p)

    k_prompt = seq(k32)[:, off:off + T].reshape(1, B, T, N_KV_HEADS, HEAD_DIM)
    v_prompt = seq(v32)[:, off:off + T].reshape(1, B, T, N_KV_HEADS, HEAD_DIM)
    kidx_prompt = seq(ki32)[:, off:off + T][None]
    gla_state_prompt = state_p[None]

    R = SAMPLE_ROWS
    n_tok_s = DB * R
    xs = jnp.pad(x_sample, ((0, 0), (0, R - S_s), (0, 0))).reshape(n_tok_s, D)
    q_s, qi_s, k_s, v_s, kiwi_s = _proj_attn_sample(xs, ln_g, ln_b, w, n_tok_s)
    sseq = lambda a: a.reshape((DB, R) + a.shape[1:])[:, :S_s]
    k_new, v_new = sseq(k_s), sseq(v_s)
    ki_new = sseq(kiwi_s)[..., :IDX_DIM]
    wi_new = sseq(kiwi_s)[..., IDX_DIM:IDX_DIM + IDX_HEADS]
    qi_stack = sseq(qi_s).reshape(DB, S_s, IDX_HEADS, IDX_DIM).transpose(0, 2, 1, 3).reshape(DB, IDX_HEADS * S_s, IDX_DIM)
    wcol = jnp.broadcast_to(wi_new.transpose(0, 2, 1).reshape(DB, IDX_HEADS * S_s, 1), (DB, IDX_HEADS * S_s, LANES))
    q_heads = sseq(q_s).reshape(DB, S_s, N_HEADS, HEAD_DIM).transpose(0, 2, 1, 3)
    kv_of_head = (jnp.arange(N_HEADS) // GROUP)[:, None] == jnp.arange(N_KV_HEADS)[None, :]
    qbd = jnp.where(kv_of_head[None, :, None, :, None], q_heads[:, :, :, None, :], jnp.zeros((), BF16))
    qbd = qbd.reshape(DB, N_HEADS * S_s, KV_WIDTH)
    pad_page = lambda a: jnp.pad(a, ((0, 0), (0, PAGE_SIZE - S_s), (0, 0))).astype(BF16).transpose(0, 2, 1)
    kinew_pg, knew_pg, vnew_pg = pad_page(ki_new), pad_page(k_new), pad_page(v_new)
    ck = cache_k[0].transpose(0, 2, 3, 1).reshape(-1, KV_WIDTH, PAGE_SIZE)
    cv = cache_v[0].transpose(0, 2, 3, 1).reshape(-1, KV_WIDTH, PAGE_SIZE)
    cki = cache_kidx[0].transpose(0, 2, 1)
    sc_past, sc_new = _idx_sample(page_table, qi_stack, wcol, kinew_pg, cki, s_q=S_s)
    thr, jmax = _sel_sample(sc_past, sc_new, topk=topk_sample)
    o_s = _att_sample(page_table, qbd, sc_past, sc_new, thr, jmax, knew_pg, vnew_pg, ck, cv, s_q=S_s)
    o_s = o_s.reshape(DB, N_KV_HEADS, GROUP, S_s, N_KV_HEADS, HEAD_DIM)
    att_s = jnp.stack([o_s[:, n, :, :, n, :] for n in range(N_KV_HEADS)], axis=1)
    att_s = att_s.transpose(0, 3, 1, 2, 4).reshape(DB, S_s, ATT_WIDTH).astype(BF16)
    att_s = jnp.pad(att_s, ((0, 0), (0, R - S_s), (0, 0))).reshape(n_tok_s, ATT_WIDTH)

    gq, gk, gv, og, la, gates_s = _proj_gla(xs, ln_g, ln_b, w, n_tok_s)
    gla_s, state_s = _gla(gq, gk, gv, og, la, gnorm, state_gla[0], batch=DB, t_pad=R, tb=R,
                          chunk=R, off=0, t_end=S_s)
    h1_s = _merge(xs, att_s, gla_s, gates_s, ln_g, ln_b, wa, wg, wo, l1g, l1b, alpha=alpha)
    y_s = _ffn(h1_s, wu, wd, l2g, l2b, alpha=alpha)

    y_sample = y_s.reshape(DB, R, D)[:, :S_s]
    k_sample = k_new.reshape(1, DB, S_s, N_KV_HEADS, HEAD_DIM)
    v_sample = v_new.reshape(1, DB, S_s, N_KV_HEADS, HEAD_DIM)
    kidx_sample = ki_new[None]
    gla_state_sample = state_s[None]
    return (y_prompt, y_sample, k_prompt, v_prompt, kidx_prompt, gla_state_prompt,
            k_sample, v_sample, kidx_sample, gla_state_sample)
```

```python
import functools
import math

import jax
import jax.numpy as jnp
from jax import lax
from jax.experimental import pallas as pl
from jax.experimental.pallas import tpu as pltpu

D_MODEL = 1024
PAGE_SIZE = 128
N_META = 16
N_HEADS = 16
HEAD_DIM = 64
N_KV_HEADS = 4
GROUP = N_HEADS // N_KV_HEADS
ATT_WIDTH = N_HEADS * HEAD_DIM
KV_WIDTH = N_KV_HEADS * HEAD_DIM
IDX_HEADS = 8
IDX_DIM = 64
TOPK_MAX = 256
GLA_HEADS = 4
GLA_DK = D_MODEL // 2 // GLA_HEADS
GLA_DV = D_MODEL // GLA_HEADS
GLA_KW = GLA_HEADS * GLA_DK
GLA_VW = GLA_HEADS * GLA_DV
GLA_GATE_RANK = 16
GLA_TAU = 16.0
GLA_CHUNK = 64
N_BRANCHES = 2
D_FF = 4 * D_MODEL
LN_EPS = 1e-5
IN_SIZES = (ATT_WIDTH, KV_WIDTH, KV_WIDTH, IDX_HEADS * IDX_DIM, IDX_DIM, IDX_HEADS,
            GLA_KW, GLA_KW, GLA_VW, GLA_VW, GLA_GATE_RANK, N_BRANCHES * D_MODEL)

LANES = 128
SUBLANES = 8
VMEM_LIMIT_BYTES = 56 * 1024 * 1024
PLANE_KEYS = 32
COUNT_LANES = 4
Q_BLOCK = LANES
KEY_CHUNK = 3 * LANES
SAMPLE_ROWS = 16
PAGES_PER_STEP = 16
PAGE_RING = 3
IDX_PAGES_PER_DOT = 4
SEL_SEQS = 16
TM_PROJ_GLA = 512
PROJ_ATTN_CHUNKS = 2
TM_TOKEN = 512
GLA_TOKENS_PER_STEP = 6 * GLA_CHUNK

F32 = jnp.float32
BF16 = jnp.bfloat16
NEG_INF = float("-inf")
INT_MIN = -2 ** 31
NT_DIMS = (((1,), (1,)), ((), ()))
TN_DIMS = (((0,), (0,)), ((), ()))


def _cparams(n_grid):
    return pltpu.CompilerParams(dimension_semantics=("arbitrary",) * n_grid,
                                vmem_limit_bytes=VMEM_LIMIT_BYTES)


def _full_spec(shape):
    nd = len(shape)
    return pl.BlockSpec(shape, lambda *_: (0,) * nd, pipeline_mode=pl.Buffered(1))


def _layer_norm(x, g, b):
    mu = jnp.mean(x, axis=-1, keepdims=True)
    xc = x - mu
    var = jnp.mean(xc * xc, axis=-1, keepdims=True)
    return xc * lax.rsqrt(var + LN_EPS) * g + b


def _bit_planes(words):
    a = list(words)
    for dist, mask in ((16, 0x0000FFFF), (8, 0x00FF00FF), (4, 0x0F0F0F0F), (2, 0x33333333), (1, 0x55555555)):
        k = 0
        while k < PLANE_KEYS:
            t = (a[k] ^ lax.shift_right_logical(a[k + dist], jnp.int32(dist))) & jnp.int32(mask)
            a[k] = a[k] ^ t
            a[k + dist] = a[k + dist] ^ lax.shift_left(t, jnp.int32(dist))
            k = (k + dist + 1) & ~dist
    return a


def _ukey_to_f32(u):
    bits = jnp.where(u < 0, u & jnp.int32(0x7FFFFFFF), ~u)
    return lax.bitcast_convert_type(bits, F32)


def _proj_attn_prompt_kernel(x_ref, g_ref, b_ref, wq_ref, wqi_ref, wv_ref, wwi_ref, wrow_ref,
                             qT_ref, qiT_ref, vT_ref, wiT_ref, k_ref, v_ref, ki_ref, kb_ref, kib_ref):
    hb = _layer_norm(x_ref[...], g_ref[...], b_ref[...]).astype(BF16)
    qT_ref[...] = lax.dot_general(wq_ref[...], hb, NT_DIMS, preferred_element_type=F32).astype(BF16)
    qiT_ref[...] = lax.dot_general(wqi_ref[...], hb, NT_DIMS, preferred_element_type=F32).astype(BF16)
    vt = lax.dot_general(wv_ref[...], hb, NT_DIMS, preferred_element_type=F32)
    row = lax.broadcasted_iota(jnp.int32, vt.shape, 0)
    vt = jnp.where(row % LANES == HEAD_DIM, 1.0, vt)
    for j in range(vT_ref.shape[0]):
        vT_ref[j] = vt[:, j * KEY_CHUNK:(j + 1) * KEY_CHUNK].astype(BF16)
    wi = lax.dot_general(wwi_ref[...], hb, NT_DIMS, preferred_element_type=F32)
    wiT_ref[...] = wi[:IDX_HEADS] * IDX_HEADS ** -0.5
    y = jnp.dot(hb, wrow_ref[...], preferred_element_type=F32)
    k = y[:, :KV_WIDTH]
    ki = y[:, 2 * KV_WIDTH:2 * KV_WIDTH + IDX_DIM]
    k_ref[...] = k
    v_ref[...] = y[:, KV_WIDTH:2 * KV_WIDTH]
    ki_ref[...] = ki
    kb_ref[...] = k.astype(BF16)
    kib_ref[...] = ki.astype(BF16)


def _proj_attn_prompt(x, ln_g, ln_b, w, n_tok):
    tm = _row_tile(n_tok, PROJ_ATTN_CHUNKS * KEY_CHUNK, KEY_CHUNK)
    n_steps = n_tok // tm
    cps = tm // KEY_CHUNK
    row = lambda n: pl.BlockSpec((tm, n), lambda i: (i, 0))
    col = lambda n: pl.BlockSpec((n, tm), lambda i: (0, i))
    out_shape = (
        jax.ShapeDtypeStruct((ATT_WIDTH, n_tok), BF16),
        jax.ShapeDtypeStruct((IDX_HEADS * IDX_DIM, n_tok), BF16),
        jax.ShapeDtypeStruct((n_tok // KEY_CHUNK, N_KV_HEADS * LANES, KEY_CHUNK), BF16),
        jax.ShapeDtypeStruct((IDX_HEADS, n_tok), F32),
        jax.ShapeDtypeStruct((n_tok, KV_WIDTH), F32),
        jax.ShapeDtypeStruct((n_tok, KV_WIDTH), F32),
        jax.ShapeDtypeStruct((n_tok, IDX_DIM), F32),
        jax.ShapeDtypeStruct((n_tok, KV_WIDTH), BF16),
        jax.ShapeDtypeStruct((n_tok, IDX_DIM), BF16),
    )
    out_specs = (col(ATT_WIDTH), col(IDX_HEADS * IDX_DIM),
                 pl.BlockSpec((cps, N_KV_HEADS * LANES, KEY_CHUNK), lambda i: (i, 0, 0)),
                 col(IDX_HEADS), row(KV_WIDTH), row(KV_WIDTH), row(IDX_DIM), row(KV_WIDTH), row(IDX_DIM))
    ws = (w["qT"], w["qiT"], w["vT"], w["wiT"], w["row_p"])
    return pl.pallas_call(
        _proj_attn_prompt_kernel,
        grid=(n_steps,),
        in_specs=[row(D_MODEL), _full_spec(ln_g.shape), _full_spec(ln_b.shape)] + [_full_spec(a.shape) for a in ws],
        out_specs=out_specs,
        out_shape=out_shape,
        compiler_params=_cparams(1),
        name="proj_attn_prompt",
    )(x, ln_g, ln_b, *ws)


def _proj_attn_sample_kernel(x_ref, g_ref, b_ref, w_ref, scale_ref, q_ref, qi_ref, k_ref, v_ref, kiwi_ref):
    hb = _layer_norm(x_ref[...], g_ref[...], b_ref[...]).astype(BF16)
    y = jnp.dot(hb, w_ref[...], preferred_element_type=F32)
    o = 0
    q_ref[...] = y[:, o:o + ATT_WIDTH].astype(BF16)
    o += ATT_WIDTH
    qi_ref[...] = y[:, o:o + IDX_HEADS * IDX_DIM].astype(BF16)
    o += IDX_HEADS * IDX_DIM
    k_ref[...] = y[:, o:o + KV_WIDTH]
    o += KV_WIDTH
    v_ref[...] = y[:, o:o + KV_WIDTH]
    o += KV_WIDTH
    kiwi_ref[...] = y[:, o:o + LANES] * scale_ref[...]


def _proj_attn_sample(x, ln_g, ln_b, w, n_tok):
    tm = _row_tile(n_tok, TM_TOKEN)
    row = lambda n: pl.BlockSpec((tm, n), lambda i: (i, 0))
    lane = lax.iota(jnp.int32, LANES)
    scale = jnp.where((lane >= IDX_DIM) & (lane < IDX_DIM + IDX_HEADS), IDX_HEADS ** -0.5, 1.0).astype(F32)[None]
    out_shape = (
        jax.ShapeDtypeStruct((n_tok, ATT_WIDTH), BF16),
        jax.ShapeDtypeStruct((n_tok, IDX_HEADS * IDX_DIM), BF16),
        jax.ShapeDtypeStruct((n_tok, KV_WIDTH), F32),
        jax.ShapeDtypeStruct((n_tok, KV_WIDTH), F32),
        jax.ShapeDtypeStruct((n_tok, LANES), F32),
    )
    return pl.pallas_call(
        _proj_attn_sample_kernel,
        grid=(n_tok // tm,),
        in_specs=[row(D_MODEL), _full_spec(ln_g.shape), _full_spec(ln_b.shape),
                  _full_spec(w["row_s"].shape), _full_spec(scale.shape)],
        out_specs=(row(ATT_WIDTH), row(IDX_HEADS * IDX_DIM), row(KV_WIDTH), row(KV_WIDTH), row(LANES)),
        out_shape=out_shape,
        compiler_params=_cparams(1),
        name="proj_attn_sample",
    )(x, ln_g, ln_b, w["row_s"], scale)


def _log_sigmoid(x):
    return jnp.minimum(x, 0.0) - jnp.log1p(jnp.exp(-jnp.abs(x)))


def _proj_gla_kernel(x_ref, g_ref, b_ref, wg_ref, wa1_ref, wa2_ref, ba_ref, wgt_ref,
                     gq_ref, gk_ref, gv_ref, og_ref, la_ref, gates_ref):
    hb = _layer_norm(x_ref[...], g_ref[...], b_ref[...]).astype(BF16)
    y = jnp.dot(hb, wg_ref[...], preferred_element_type=F32)
    gq_ref[...] = y[:, :GLA_KW]
    gk_ref[...] = y[:, GLA_KW:2 * GLA_KW]
    gv_ref[...] = y[:, 2 * GLA_KW:2 * GLA_KW + GLA_VW].astype(BF16)
    og_ref[...] = y[:, 2 * GLA_KW + GLA_VW:]
    a1 = jnp.dot(hb, wa1_ref[...], preferred_element_type=F32).astype(BF16)
    z = jnp.dot(a1, wa2_ref[...], preferred_element_type=F32) + ba_ref[...]
    la_ref[...] = _log_sigmoid(z) / GLA_TAU
    gt = jnp.dot(hb, wgt_ref[...], preferred_element_type=F32)
    gates_ref[...] = jax.nn.sigmoid(gt).astype(BF16)


def _proj_gla(x, ln_g, ln_b, w, n_tok):
    tm = _row_tile(n_tok, TM_PROJ_GLA)
    row = lambda n: pl.BlockSpec((tm, n), lambda i: (i, 0))
    ws = (w["gla"], w["a1"], w["a2"], w["ba"], w["gt"])
    out_shape = (
        jax.ShapeDtypeStruct((n_tok, GLA_KW), F32),
        jax.ShapeDtypeStruct((n_tok, GLA_KW), F32),
        jax.ShapeDtypeStruct((n_tok, GLA_VW), BF16),
        jax.ShapeDtypeStruct((n_tok, GLA_VW), F32),
        jax.ShapeDtypeStruct((n_tok, GLA_KW), F32),
        jax.ShapeDtypeStruct((n_tok, N_BRANCHES * D_MODEL), BF16),
    )
    return pl.pallas_call(
        _proj_gla_kernel,
        grid=(n_tok // tm,),
        in_specs=[row(D_MODEL), _full_spec(ln_g.shape), _full_spec(ln_b.shape)] + [_full_spec(a.shape) for a in ws],
        out_specs=(row(GLA_KW), row(GLA_KW), row(GLA_VW), row(GLA_VW), row(GLA_KW), row(N_BRANCHES * D_MODEL)),
        out_shape=out_shape,
        compiler_params=_cparams(1),
        name="proj_gla",
    )(x, ln_g, ln_b, *ws)


def _dsa_prompt_kernel(qT_ref, qiT_ref, wiT_ref, kb_ref, kib_ref, vT_ref, att_ref,
                       sc_ref, planes_ref, qn_ref, acc_ref, j_ref, sa_ref, sb_ref, mrun_ref, *, topk, off, n_pos_bits):
    kc = KEY_CHUNK
    groups_per_pair = 2 * kc // (SUBLANES * PLANE_KEYS)
    i = pl.program_id(1)

    @pl.when((pl.program_id(0) == 0) & (i == 0))
    def _():
        planes_ref[...] = jnp.zeros(planes_ref.shape, jnp.int32)

    n_chunks = (i * Q_BLOCK + Q_BLOCK + kc - 1) // kc
    qpos = i * Q_BLOCK + lax.broadcasted_iota(jnp.int32, (1, Q_BLOCK), 1)

    def key_pos(c):
        return c * kc + lax.broadcasted_iota(jnp.int32, (kc, Q_BLOCK), 0)

    def run_indexer():
        wi = wiT_ref[...]

        def chunk_scores(c):
            kic = kib_ref[c]
            acc = jnp.zeros((kc, Q_BLOCK), F32)
            for hp in range(IDX_HEADS // 2):
                r0 = 2 * hp * IDX_DIM
                rhs = jnp.concatenate([qiT_ref[r0:r0 + IDX_DIM, :], qiT_ref[r0 + IDX_DIM:r0 + 2 * IDX_DIM, :]], axis=1)
                s = jnp.maximum(jnp.dot(kic, rhs, preferred_element_type=F32), 0.0)
                acc = acc + s[:, :Q_BLOCK] * wi[2 * hp:2 * hp + 1, :] + s[:, Q_BLOCK:] * wi[2 * hp + 1:2 * hp + 2, :]
            kp = key_pos(c)
            valid = (kp >= off) & (kp <= qpos)
            sc = jnp.where(valid, acc, NEG_INF)
            sc_ref[c] = sc
            bits = lax.bitcast_convert_type(sc, jnp.int32)
            return (bits ^ ((bits >> 31) | jnp.int32(INT_MIN))).reshape(kc // SUBLANES, SUBLANES, Q_BLOCK)

        def idx_body(t, carry):
            c1 = 2 * t + 1
            keys0 = chunk_scores(2 * t)
            keys1 = jnp.where(c1 < n_chunks, chunk_scores(jnp.minimum(c1, n_chunks - 1)), 0)
            words = [keys0[k] for k in range(keys0.shape[0])] + [keys1[k] for k in range(keys1.shape[0])]
            for g in range(groups_per_pair):
                for b, plane in enumerate(_bit_planes(words[g * PLANE_KEYS:(g + 1) * PLANE_KEYS])):
                    planes_ref[groups_per_pair * t + g, b] = plane
            return carry

        lax.fori_loop(0, (n_chunks + 1) // 2, idx_body, 0)

    def chunk_counts(pred_c, accs):
        v = jnp.where(pred_c, 1.0, 0.0).reshape(COUNT_LANES, kc // (SUBLANES * COUNT_LANES), SUBLANES, Q_BLOCK)
        return tuple(a + jnp.sum(v[r], axis=0) for r, a in enumerate(accs))

    zero_accs = (jnp.zeros((SUBLANES, Q_BLOCK), F32),) * COUNT_LANES

    def total(accs):
        return jnp.sum(functools.reduce(lambda a, b: a + b, accs), axis=0, keepdims=True)

    def count(pred):
        return total(lax.fori_loop(0, n_chunks, lambda c, accs: chunk_counts(pred(c), accs), zero_accs))

    def search():
        n_groups = planes_ref.shape[0]

        def step(t, carry):
            res, want, alive = carry
            planes = [planes_ref[g, t] for g in range(n_groups)]
            ones = functools.reduce(lambda x, y: x + y, [lax.population_count(a & p) for a, p in zip(alive, planes)])
            cnt = jnp.sum(ones, axis=0, keepdims=True)
            take = cnt >= want
            res = jnp.where(take, res | jnp.left_shift(jnp.int32(1), 31 - t), res)
            want = jnp.where(take, want, want - cnt)
            keep = jnp.where(take, jnp.int32(0), jnp.int32(-1))
            return res, want, tuple(a & (p ^ keep) for a, p in zip(alive, planes))

        alive = tuple(jnp.full((SUBLANES, Q_BLOCK), jnp.where(g // groups_per_pair < (n_chunks + 1) // 2, -1, 0), jnp.int32)
                      for g in range(n_groups))
        start = (jnp.zeros((1, Q_BLOCK), jnp.int32), jnp.full((1, Q_BLOCK), topk, jnp.int32), alive)
        return lax.fori_loop(0, 32, step, start)[0]

    def finish_selection(res):
        few = (qpos - off + 1) <= topk
        thr = jnp.where(few, NEG_INF, _ukey_to_f32(res))
        cnt_gt = count(lambda c: sc_ref[c] > thr)
        cnt_ge = count(lambda c: sc_ref[c] >= thr)
        n_ties = topk - cnt_gt
        tie_rows = jnp.where(few, 0.0, jnp.where(cnt_ge > topk, 1.0, 0.0))
        j_ref[...] = jnp.full(j_ref.shape, 2 ** 30, jnp.int32)

        @pl.when(jnp.max(tie_rows) > 0.0)
        def _():
            def jbit_body(t, resj):
                trial = resj | jnp.left_shift(jnp.int32(1), n_pos_bits - 1 - t)
                below = count(lambda c: (sc_ref[c] == thr) & (key_pos(c) < trial))
                return jnp.where(below < n_ties, trial, resj)
            j_ref[...] = lax.fori_loop(0, n_pos_bits, jbit_body, jnp.zeros((1, Q_BLOCK), jnp.int32))

        jmax = j_ref[...]

        def bias_body(c, carry):
            sc = sc_ref[c]
            kp = key_pos(c)
            valid = (kp >= off) & (kp <= qpos)
            sel = (sc > thr) | ((sc == thr) & (kp <= jmax))
            sc_ref[c] = jnp.where(valid & sel, 0.0, NEG_INF)
            return carry

        lax.fori_loop(0, n_chunks, bias_body, 0)

    gw = GROUP * Q_BLOCK
    last = n_chunks - 1
    m_none = jnp.full((1, N_HEADS * Q_BLOCK), NEG_INF, F32)

    def score_stage(c, buf_ref, m_run):
        bias = jnp.concatenate([sc_ref[c]] * N_HEADS, axis=1)
        s = jnp.dot(kb_ref[c], qn_ref[...], preferred_element_type=F32) + bias
        buf_ref[...] = s
        return jnp.maximum(m_run, jnp.max(s, axis=0, keepdims=True))

    def prob_stage(c, buf_ref, m_before, m_with):
        m_safe = jnp.maximum(m_with, -1e30)
        alpha = jnp.exp2(m_before - m_safe)
        p = jnp.exp2(buf_ref[...] - m_safe).astype(BF16)
        vt = vT_ref[c]
        for n in range(N_KV_HEADS):
            cols = slice(n * gw, (n + 1) * gw)
            pv = jnp.dot(vt[n * LANES:(n + 1) * LANES, :], p[:, cols], preferred_element_type=F32)
            acc_ref[n] = acc_ref[n] * alpha[:, cols] + pv

    def attention_start():
        qn_ref[...] = jnp.zeros(qn_ref.shape, BF16)
        for h in range(N_HEADS):
            n = h // GROUP
            qn_ref[n * HEAD_DIM:(n + 1) * HEAD_DIM, h * Q_BLOCK:(h + 1) * Q_BLOCK] = qT_ref[h * HEAD_DIM:(h + 1) * HEAD_DIM, :]
        acc_ref[...] = jnp.zeros(acc_ref.shape, F32)
        return m_none, score_stage(0, sa_ref, m_none)

    def pair_step(t, m_prev, m_cur):
        c0 = 2 * t
        m_1 = score_stage(c0 + 1, sb_ref, m_cur)
        prob_stage(c0, sa_ref, m_prev, m_cur)
        m_2 = score_stage(c0 + 2, sa_ref, m_1)
        prob_stage(c0 + 1, sb_ref, m_cur, m_1)
        return m_1, m_2

    def attention_loop():
        n_pairs = (n_chunks - 1) // 2
        m_prev, m_cur = lax.fori_loop(0, n_pairs, lambda t, ms: pair_step(t, *ms), attention_start())
        mrun_ref[0] = m_prev
        mrun_ref[1] = m_cur
        c_a = 2 * n_pairs

        @pl.when(c_a == last)
        def _():
            prob_stage(c_a, sa_ref, mrun_ref[0], mrun_ref[1])

        @pl.when(c_a < last)
        def _():
            m_1 = score_stage(c_a + 1, sb_ref, mrun_ref[1])
            prob_stage(c_a, sa_ref, mrun_ref[0], mrun_ref[1])
            prob_stage(c_a + 1, sb_ref, mrun_ref[1], m_1)

    def attention_finish():
        heads = []
        for n in range(N_KV_HEADS):
            a = acc_ref[n]
            denom = a[HEAD_DIM:HEAD_DIM + 1, :]
            o = a[:HEAD_DIM, :] / jnp.where(denom > 0.0, denom, 1.0)
            for g in range(GROUP):
                heads.append(o[:, g * Q_BLOCK:(g + 1) * Q_BLOCK])
        att_ref[...] = jnp.concatenate(heads, axis=0).T.astype(BF16)

    run_indexer()
    finish_selection(search())
    attention_loop()
    attention_finish()


def _dsa_prompt(qT, qiT, wiT, kb3, kib3, vT3, *, batch, t_pad, topk, off):
    kc = KEY_CHUNK
    nqb = t_pad // Q_BLOCK
    ncb = t_pad // kc
    n_tok = batch * t_pad
    qcol = lambda n: pl.BlockSpec((n, Q_BLOCK), lambda b, i: (0, b * nqb + i))
    kern = functools.partial(_dsa_prompt_kernel, topk=topk, off=off,
                             n_pos_bits=max(1, math.ceil(math.log2(t_pad))))
    return pl.pallas_call(
        kern,
        grid=(batch, nqb),
        in_specs=[qcol(ATT_WIDTH), qcol(IDX_HEADS * IDX_DIM), qcol(IDX_HEADS),
                  pl.BlockSpec((ncb, kc, KV_WIDTH), lambda b, i: (b, 0, 0)),
                  pl.BlockSpec((ncb, kc, IDX_DIM), lambda b, i: (b, 0, 0)),
                  pl.BlockSpec((ncb, N_KV_HEADS * LANES, kc), lambda b, i: (b, 0, 0))],
        out_specs=pl.BlockSpec((Q_BLOCK, ATT_WIDTH), lambda b, i: (b * nqb + i, 0)),
        out_shape=jax.ShapeDtypeStruct((n_tok, ATT_WIDTH), BF16),
        scratch_shapes=[pltpu.VMEM((ncb, kc, Q_BLOCK), F32),
                        pltpu.VMEM((-(-ncb // 2) * (2 * kc // (SUBLANES * PLANE_KEYS)), PLANE_KEYS, SUBLANES, Q_BLOCK),
                                   jnp.int32),
                        pltpu.VMEM((KV_WIDTH, N_HEADS * Q_BLOCK), BF16),
                        pltpu.VMEM((N_KV_HEADS, LANES, GROUP * Q_BLOCK), F32),
                        pltpu.VMEM((1, Q_BLOCK), jnp.int32),
                        pltpu.VMEM((kc, N_HEADS * Q_BLOCK), F32),
                        pltpu.VMEM((kc, N_HEADS * Q_BLOCK), F32),
                        pltpu.VMEM((2, 1, N_HEADS * Q_BLOCK), F32)],
        compiler_params=_cparams(2),
        name="dsa_prompt",
    )(qT, qiT, wiT, kb3, kib3, vT3)


def _idx_sample_kernel(pt_ref, qi_ref, wcol_ref, kinew_ref, cki_hbm, past_ref, new_ref, kibuf, sem):
    n_pages = kibuf.shape[1]
    s_q = new_ref.shape[0]
    b = pl.program_id(0)
    slot = b % 2

    def page_copies(seq, dst_slot):
        return [pltpu.make_async_copy(cki_hbm.at[pt_ref[seq, r]], kibuf.at[dst_slot, r], sem.at[dst_slot])
                for r in range(n_pages)]

    @pl.when(b == 0)
    def _():
        for cp in page_copies(0, 0):
            cp.start()

    @pl.when(b + 1 < pl.num_programs(0))
    def _():
        for cp in page_copies(b + 1, 1 - slot):
            cp.start()

    qi = qi_ref[...]

    def scores(keys_t_bf16):
        s = jnp.dot(qi, keys_t_bf16, preferred_element_type=F32)
        wcol = jnp.concatenate([wcol_ref[...]] * (s.shape[1] // LANES), axis=1)
        s = jnp.maximum(s, 0.0) * wcol
        return jnp.sum(s.reshape(IDX_HEADS, s_q, s.shape[1]), axis=0)

    s_new = scores(kinew_ref[...])
    qrow = lax.broadcasted_iota(jnp.int32, s_new.shape, 0)
    kcol = lax.broadcasted_iota(jnp.int32, s_new.shape, 1)
    new_ref[...] = jnp.where(kcol <= qrow, s_new, NEG_INF)

    for cp in page_copies(b, slot):
        cp.wait()
    npg = PAGES_PER_STEP
    for r0 in range(0, n_pages, IDX_PAGES_PER_DOT):
        keys = jnp.concatenate([kibuf[slot, r0 + j].astype(BF16) for j in range(IDX_PAGES_PER_DOT)], axis=1)
        c, p = divmod(r0, npg)
        past_ref[c, :, p * PAGE_SIZE:(p + IDX_PAGES_PER_DOT) * PAGE_SIZE] = scores(keys)


def _idx_sample(page_table, qi_stack, wcol, kinew, cache_kidx, *, s_q):
    db, n_pages = page_table.shape
    npg = PAGES_PER_STEP
    n_steps = n_pages // npg
    rows = IDX_HEADS * s_q
    per_seq = lambda r, n: pl.BlockSpec((None, r, n), lambda b, pt: (b, 0, 0))
    grid_spec = pltpu.PrefetchScalarGridSpec(
        num_scalar_prefetch=1,
        grid=(db,),
        in_specs=[per_seq(rows, IDX_DIM), per_seq(rows, LANES), per_seq(IDX_DIM, PAGE_SIZE),
                  pl.BlockSpec(memory_space=pl.ANY)],
        out_specs=(pl.BlockSpec((n_steps, None, s_q, npg * PAGE_SIZE), lambda b, pt: (0, b, 0, 0)),
                   per_seq(s_q, PAGE_SIZE)),
        scratch_shapes=[pltpu.VMEM((2, n_pages, IDX_DIM, PAGE_SIZE), F32), pltpu.SemaphoreType.DMA((2,))],
    )
    return pl.pallas_call(
        _idx_sample_kernel,
        grid_spec=grid_spec,
        out_shape=(jax.ShapeDtypeStruct((n_steps, db, s_q, npg * PAGE_SIZE), F32),
                   jax.ShapeDtypeStruct((db, s_q, PAGE_SIZE), F32)),
        compiler_params=_cparams(1),
        name="idx_sample",
    )(page_table, qi_stack, wcol, kinew, cache_kidx)


def _sel_sample_kernel(past_ref, new_ref, thr_ref, j_ref, planes_ref, *, topk, n_pos_bits):
    n_c, g, s_q, w = past_ref.shape
    rows = g * s_q
    l_past = n_c * w
    lane_pos = lax.broadcasted_iota(jnp.int32, (rows, LANES), 1)

    def lane_tile_sum(v):
        return functools.reduce(lambda a, b: a + b, [v[:, t * LANES:(t + 1) * LANES] for t in range(v.shape[1] // LANES)])

    def count(pred):
        def body(cc, acc):
            x = past_ref[cc].reshape(rows, w)
            pos = cc * w + lax.broadcasted_iota(jnp.int32, (rows, w), 1)
            return acc + lane_tile_sum(jnp.where(pred(x, pos), 1.0, 0.0))
        acc = lax.fori_loop(0, n_c, body, jnp.zeros((rows, LANES), F32))
        acc = acc + jnp.where(pred(new_ref[...].reshape(rows, PAGE_SIZE), l_past + lane_pos), 1.0, 0.0)
        return jnp.sum(acc, axis=1, keepdims=True)

    def ukeys(x):
        bits = lax.bitcast_convert_type(x, jnp.int32)
        return bits ^ ((bits >> 31) | jnp.int32(INT_MIN))

    tiles = [(cc, k) for cc in range(n_c) for k in range(w // LANES)] + [None]
    n_groups = planes_ref.shape[0]
    no_key = jnp.zeros((rows, LANES), jnp.int32)
    for gi in range(n_groups):
        words = []
        for tile in (tiles[gi * PLANE_KEYS:(gi + 1) * PLANE_KEYS] + [False] * PLANE_KEYS)[:PLANE_KEYS]:
            if tile is None:
                words.append(ukeys(new_ref[...].reshape(rows, PAGE_SIZE)))
            elif tile is False:
                words.append(no_key)
            else:
                words.append(ukeys(past_ref[tile[0], :, :, tile[1] * LANES:(tile[1] + 1) * LANES].reshape(rows, LANES)))
        for t, plane in enumerate(_bit_planes(words)):
            planes_ref[gi, t] = plane

    def step(t, carry):
        res, want, alive = carry
        planes = [planes_ref[gi, t] for gi in range(n_groups)]
        ones = functools.reduce(lambda x, y: x + y, [lax.population_count(a & p) for a, p in zip(alive, planes)])
        cnt = jnp.sum(ones, axis=1, keepdims=True)
        take = cnt >= want
        res = jnp.where(take, res | jnp.left_shift(jnp.int32(1), 31 - t), res)
        want = jnp.where(take, want, want - cnt)
        keep = jnp.where(take, jnp.int32(0), jnp.int32(-1))
        return res, want, tuple(a & (p ^ keep) for a, p in zip(alive, planes))

    start = (jnp.zeros((rows, 1), jnp.int32), jnp.full((rows, 1), topk, jnp.int32),
             (jnp.full((rows, LANES), -1, jnp.int32),) * n_groups)
    res = lax.fori_loop(0, 32, step, start)[0]
    thr = _ukey_to_f32(res)
    n_ties = topk - count(lambda x, pos: x > thr)
    cnt_ge = count(lambda x, pos: x >= thr)
    thr_ref[...] = jnp.broadcast_to(thr, (rows, LANES))
    j_ref[...] = jnp.full((rows, LANES), 2 ** 30, jnp.int32)

    @pl.when(jnp.max(jnp.where(cnt_ge > topk, 1.0, 0.0)) > 0.0)
    def _():
        def jbit_body(t, resj):
            trial = resj | jnp.left_shift(jnp.int32(1), n_pos_bits - 1 - t)
            below = count(lambda x, pos: (x == thr) & (pos < trial))
            return jnp.where(below < n_ties, trial, resj)
        jmax = lax.fori_loop(0, n_pos_bits, jbit_body, jnp.zeros((rows, 1), jnp.int32))
        j_ref[...] = jnp.broadcast_to(jmax, (rows, LANES))


def _sel_sample(sc_past, sc_new, *, topk):
    n_c, db, s_q, w = sc_past.shape
    g = math.gcd(SEL_SEQS, db)
    kern = functools.partial(_sel_sample_kernel, topk=topk,
                             n_pos_bits=max(1, math.ceil(math.log2(n_c * w + PAGE_SIZE))))
    return pl.pallas_call(
        kern,
        grid=(db // g,),
        in_specs=[pl.BlockSpec((n_c, g, s_q, w), lambda i: (0, i, 0, 0)),
                  pl.BlockSpec((g, s_q, PAGE_SIZE), lambda i: (i, 0, 0))],
        out_specs=(pl.BlockSpec((g * s_q, LANES), lambda i: (i, 0)),
                   pl.BlockSpec((g * s_q, LANES), lambda i: (i, 0))),
        out_shape=(jax.ShapeDtypeStruct((db * s_q, LANES), F32),
                   jax.ShapeDtypeStruct((db * s_q, LANES), jnp.int32)),
        scratch_shapes=[pltpu.VMEM((-(-(n_c * w // LANES + 1) // PLANE_KEYS), PLANE_KEYS, g * s_q, LANES), jnp.int32)],
        compiler_params=_cparams(1),
        name="sel_sample",
    )(sc_past, sc_new)


def _att_sample_kernel(pt_ref, qbd_ref, sc_past_ref, sc_new_ref, thr_ref, j_ref, knew_ref, vnew_ref, ck_hbm, cv_hbm,
                       att_ref, kbuf, vbuf, ksem, vsem, acc_ref, m_ref, l_ref):
    ring, npg = kbuf.shape[:2]
    ahead = ring - 1
    n_steps = sc_past_ref.shape[0]
    b = pl.program_id(0)
    n_seq = pl.num_programs(0)
    qbd = qbd_ref[...]
    thr = thr_ref[...]
    jmax = j_ref[...]

    def page_copies(seq, step, slot):
        cps = []
        for r in range(npg):
            page = pt_ref[seq, step * npg + r]
            cps.append(pltpu.make_async_copy(ck_hbm.at[page], kbuf.at[slot, r], ksem.at[slot]))
            cps.append(pltpu.make_async_copy(cv_hbm.at[page], vbuf.at[slot, r], vsem.at[slot]))
        return cps

    @pl.when(b == 0)
    def _():
        for g in range(ahead):
            for cp in page_copies(0, g, g):
                cp.start()

    acc_ref[...] = jnp.zeros(acc_ref.shape, F32)
    m_ref[...] = jnp.full(m_ref.shape, NEG_INF, F32)
    l_ref[...] = jnp.zeros(l_ref.shape, F32)

    def masked_scores(keys_t_bf16, sc, pos0):
        pos = pos0 + lax.broadcasted_iota(jnp.int32, sc.shape, 1)
        tile = lambda a: jnp.concatenate([a] * (sc.shape[1] // LANES), axis=1)
        sel = (sc > tile(thr)) | ((sc == tile(thr)) & (pos <= tile(jmax)))
        bias = jnp.where(sel, 0.0, NEG_INF)
        s = jnp.dot(qbd, keys_t_bf16, preferred_element_type=F32)
        return s + jnp.concatenate([bias] * N_HEADS, axis=0)

    def accumulate(s_list, vt_list):
        m_old = m_ref[...]
        m_new = jnp.maximum(m_old, jnp.max(functools.reduce(jnp.maximum, s_list), axis=1, keepdims=True))
        m_safe = jnp.maximum(m_new, -1e30)
        alpha = jnp.exp2(m_old - m_safe)
        acc = acc_ref[...] * alpha
        p_list = [jnp.exp2(s - m_safe) for s in s_list]
        for p, vt in zip(p_list, vt_list):
            acc = acc + lax.dot_general(p.astype(BF16), vt, NT_DIMS, preferred_element_type=F32)
        acc_ref[...] = acc
        l_ref[...] = l_ref[...] * alpha + jnp.sum(functools.reduce(lambda a, b: a + b, p_list), axis=1, keepdims=True)
        m_ref[...] = m_new

    for c in range(n_steps):
        group = b * n_steps + c
        slot = group % ring
        slot_ahead = (group + ahead) % ring
        if c + ahead < n_steps:
            for cp in page_copies(b, c + ahead, slot_ahead):
                cp.start()
        else:
            @pl.when(b + 1 < n_seq)
            def _():
                for cp in page_copies(b + 1, c + ahead - n_steps, slot_ahead):
                    cp.start()
        for cp in page_copies(b, c, slot):
            cp.wait()
        s_list, v_list = [], []
        for p in range(0, npg, 2):
            sc = sc_past_ref[c, :, p * PAGE_SIZE:(p + 2) * PAGE_SIZE]
            keys = jnp.concatenate([kbuf[slot, p].astype(BF16), kbuf[slot, p + 1].astype(BF16)], axis=1)
            s_list.append(masked_scores(keys, sc, (c * npg + p) * PAGE_SIZE))
            v_list.append(jnp.concatenate([vbuf[slot, p].astype(BF16), vbuf[slot, p + 1].astype(BF16)], axis=1))
        accumulate(s_list, v_list)

    accumulate([masked_scores(knew_ref[...], sc_new_ref[...], n_steps * npg * PAGE_SIZE)], [vnew_ref[...]])
    att_ref[...] = acc_ref[...] / l_ref[...]


def _att_sample(page_table, qbd, sc_past, sc_new, thr, jmax, knew, vnew, cache_k, cache_v, *, s_q):
    db, n_pages = page_table.shape
    npg = PAGES_PER_STEP
    n_steps = n_pages // npg
    rows = N_HEADS * s_q
    assert n_steps >= PAGE_RING - 1
    per_seq = lambda r, n: pl.BlockSpec((None, r, n), lambda b, pt: (b, 0, 0))
    page_buf = pltpu.VMEM((PAGE_RING, npg, KV_WIDTH, PAGE_SIZE), F32)
    ring_sem = pltpu.SemaphoreType.DMA((PAGE_RING,))
    grid_spec = pltpu.PrefetchScalarGridSpec(
        num_scalar_prefetch=1,
        grid=(db,),
        in_specs=[per_seq(rows, KV_WIDTH),
                  pl.BlockSpec((n_steps, None, s_q, npg * PAGE_SIZE), lambda b, pt: (0, b, 0, 0)),
                  per_seq(s_q, PAGE_SIZE),
                  pl.BlockSpec((s_q, LANES), lambda b, pt: (b, 0)),
                  pl.BlockSpec((s_q, LANES), lambda b, pt: (b, 0)),
                  per_seq(KV_WIDTH, PAGE_SIZE), per_seq(KV_WIDTH, PAGE_SIZE),
                  pl.BlockSpec(memory_space=pl.ANY), pl.BlockSpec(memory_space=pl.ANY)],
        out_specs=per_seq(rows, KV_WIDTH),
        scratch_shapes=[page_buf, page_buf, ring_sem, ring_sem,
                        pltpu.VMEM((rows, KV_WIDTH), F32), pltpu.VMEM((rows, 1), F32), pltpu.VMEM((rows, 1), F32)],
    )
    return pl.pallas_call(
        _att_sample_kernel,
        grid_spec=grid_spec,
        out_shape=jax.ShapeDtypeStruct((db, rows, KV_WIDTH), F32),
        compiler_params=_cparams(1),
        name="att_sample",
    )(page_table, qbd, sc_past, sc_new, thr, jmax, knew, vnew, cache_k, cache_v)


def _gla_kernel(*refs, chunk, n_chunks, off, t_end, has_s0):
    if has_s0:
        gq_ref, gk_ref, gv_ref, og_ref, la_ref, gn_ref, s0_ref, o_ref, sfin_ref, st_ref = refs
    else:
        gq_ref, gk_ref, gv_ref, og_ref, la_ref, gn_ref, o_ref, sfin_ref, st_ref = refs
    j = pl.program_id(1)
    tb = chunk * n_chunks

    @pl.when(j == 0)
    def _():
        for hh in range(GLA_HEADS):
            st_ref[hh] = s0_ref[hh].T if has_s0 else jnp.zeros((GLA_DV, GLA_DK), F32)

    r_i = lax.broadcasted_iota(jnp.int32, (chunk, chunk), 0)
    c_i = lax.broadcasted_iota(jnp.int32, (chunk, chunk), 1)
    causal = r_i >= c_i
    tril = jnp.where(causal, 1.0, 0.0)
    gn = gn_ref[...]

    def chunk_body(c, carry):
        r0 = pl.multiple_of(c * chunk, chunk)
        rows = pl.ds(r0, chunk)
        pos = j * tb + r0 + lax.broadcasted_iota(jnp.int32, (chunk, GLA_KW), 0)
        valid = (pos >= off) & (pos < t_end)
        la = jnp.where(valid, la_ref[rows, :], 0.0)
        k = jnp.where(valid, gk_ref[rows, :], 0.0)
        q = gq_ref[rows, :] * GLA_DK ** -0.5
        v = gv_ref[rows, :]
        b = jnp.dot(tril, la, preferred_element_type=F32, precision=lax.Precision.HIGHEST)
        b_last = b[chunk - 1:chunk, :]
        qd = (q * jnp.exp(b)).astype(BF16)
        kd = (k * jnp.exp(-b)).astype(BF16)
        ke = (k * jnp.exp(b_last - b)).astype(BF16)
        decay = jnp.exp(b_last)
        outs = []
        for hh in range(GLA_HEADS):
            ks = slice(hh * GLA_DK, (hh + 1) * GLA_DK)
            vh = v[:, hh * GLA_DV:(hh + 1) * GLA_DV]
            a = lax.dot_general(qd[:, ks], kd[:, ks], NT_DIMS, preferred_element_type=F32)
            a = jnp.where(causal, a, 0.0).astype(BF16)
            st = st_ref[hh]
            o = (jnp.dot(a, vh, preferred_element_type=F32)
                 + lax.dot_general(qd[:, ks], st.astype(BF16), NT_DIMS, preferred_element_type=F32))
            u_t = lax.dot_general(vh, ke[:, ks], TN_DIMS, preferred_element_type=F32)
            st_ref[hh] = decay[:, ks] * st + u_t
            o = o * lax.rsqrt(jnp.mean(o * o, axis=-1, keepdims=True) + LN_EPS) * gn
            outs.append(o)
        og = og_ref[rows, :]
        o_ref[rows, :] = (jnp.concatenate(outs, axis=1) * (og * jax.nn.sigmoid(og))).astype(BF16)
        return carry

    per_step = next(k for k in (6, 3, 2, 1) if n_chunks % k == 0)

    def step_body(t, carry):
        for k in range(per_step):
            chunk_body(per_step * t + k, carry)
        return carry

    lax.fori_loop(0, n_chunks // per_step, step_body, 0)

    @pl.when(j == pl.num_programs(1) - 1)
    def _():
        for hh in range(GLA_HEADS):
            sfin_ref[hh] = st_ref[hh].T


def _gla(gq, gk, gv, og, la, gnorm, s0, *, batch, t_pad, tb, chunk, off, t_end):
    n_steps = t_pad // tb
    tok = lambda n: pl.BlockSpec((tb, n), lambda b, j: (b * n_steps + j, 0))
    state = pl.BlockSpec((None, GLA_HEADS, GLA_DK, GLA_DV), lambda b, j: (b, 0, 0, 0))
    has_s0 = s0 is not None
    kern = functools.partial(_gla_kernel, chunk=chunk, n_chunks=tb // chunk, off=off, t_end=t_end, has_s0=has_s0)
    in_specs = [tok(GLA_KW), tok(GLA_KW), tok(GLA_VW), tok(GLA_VW), tok(GLA_KW), _full_spec(gnorm.shape)]
    args = [gq, gk, gv, og, la, gnorm]
    if has_s0:
        in_specs.append(state)
        args.append(s0)
    return pl.pallas_call(
        kern,
        grid=(batch, n_steps),
        in_specs=in_specs,
        out_specs=(tok(GLA_VW), state),
        out_shape=(jax.ShapeDtypeStruct((batch * t_pad, GLA_VW), BF16),
                   jax.ShapeDtypeStruct((batch, GLA_HEADS, GLA_DK, GLA_DV), F32)),
        scratch_shapes=[pltpu.VMEM((GLA_HEADS, GLA_DV, GLA_DK), F32)],
        compiler_params=_cparams(2),
        name="gla",
    )(*args)


def _merge_kernel(x_ref, att_ref, gla_ref, gates_ref, lng_ref, lnb_ref, wa_ref, wg_ref, wo_ref,
                  l1g_ref, l1b_ref, h1_ref, *, alpha):
    h = _layer_norm(x_ref[...], lng_ref[...], lnb_ref[...])
    pa = jnp.dot(att_ref[...], wa_ref[...], preferred_element_type=F32)
    pg = jnp.dot(gla_ref[...], wg_ref[...], preferred_element_type=F32)
    gates = gates_ref[...].astype(F32)
    merged = gates[:, :D_MODEL] * pa + gates[:, D_MODEL:] * pg
    mix = jnp.dot(merged.astype(BF16), wo_ref[...], preferred_element_type=F32)
    h1_ref[...] = _layer_norm(alpha * h + mix, l1g_ref[...], l1b_ref[...])


def _merge(x, att, gla, gates, ln_g, ln_b, wa, wg, wo, l1g, l1b, *, alpha):
    n_tok = x.shape[0]
    tm = _row_tile(n_tok, TM_TOKEN)
    row = lambda n: pl.BlockSpec((tm, n), lambda i: (i, 0))
    consts = (ln_g, ln_b, wa, wg, wo, l1g, l1b)
    return pl.pallas_call(
        functools.partial(_merge_kernel, alpha=alpha),
        grid=(n_tok // tm,),
        in_specs=[row(D_MODEL), row(ATT_WIDTH), row(GLA_VW), row(N_BRANCHES * D_MODEL)]
                 + [_full_spec(a.shape) for a in consts],
        out_specs=row(D_MODEL),
        out_shape=jax.ShapeDtypeStruct((n_tok, D_MODEL), F32),
        compiler_params=_cparams(1),
        name="merge",
    )(x, att, gla, gates, *consts)


def _ffn_kernel(h_ref, wu_ref, wd_ref, g_ref, b_ref, y_ref, *, alpha, n_split):
    h = h_ref[...]
    hb = h.astype(BF16)
    w = D_FF // n_split
    ff = jnp.zeros(h.shape, F32)
    for s in range(n_split):
        u = jnp.dot(hb, wu_ref[:, s * w:(s + 1) * w], preferred_element_type=F32)
        u = jnp.square(jnp.maximum(u, 0.0)).astype(BF16)
        ff = ff + jnp.dot(u, wd_ref[s * w:(s + 1) * w, :], preferred_element_type=F32)
    y_ref[...] = _layer_norm(alpha * h + ff, g_ref[...], b_ref[...])


def _ffn(h1, wu, wd, g, b, *, alpha):
    n_tok = h1.shape[0]
    tm = _row_tile(n_tok, TM_TOKEN)
    row = pl.BlockSpec((tm, D_MODEL), lambda i: (i, 0))
    return pl.pallas_call(
        functools.partial(_ffn_kernel, alpha=alpha, n_split=4),
        grid=(n_tok // tm,),
        in_specs=[row] + [_full_spec(a.shape) for a in (wu, wd, g, b)],
        out_specs=row,
        out_shape=jax.ShapeDtypeStruct((n_tok, D_MODEL), F32),
        compiler_params=_cparams(1),
        name="ffn",
    )(h1, wu, wd, g, b)


def _ffn_window(h1, wu, wd, g, b, *, alpha, start, length):
    batch, rows, _ = h1.shape
    tm = _row_tile(length, TM_TOKEN)
    consts = (wu, wd, g, b)
    assert rows % SUBLANES == 0 and start % SUBLANES == 0
    first_row = lambda bi, j: pl.multiple_of(bi * rows + start + j * tm, SUBLANES)
    return pl.pallas_call(
        functools.partial(_ffn_kernel, alpha=alpha, n_split=4),
        grid=(batch, length // tm),
        in_specs=[pl.BlockSpec((pl.Element(tm), pl.Element(D_MODEL)), lambda bi, j: (first_row(bi, j), 0))]
                 + [_full_spec(a.shape) for a in consts],
        out_specs=pl.BlockSpec((None, tm, D_MODEL), lambda bi, j: (bi, j, 0)),
        out_shape=jax.ShapeDtypeStruct((batch, length, D_MODEL), F32),
        compiler_params=_cparams(2),
        name="ffn_window",
    )(h1.reshape(batch * rows, D_MODEL), *consts)


def _pack_weights(w_in, w_gla_a2, b_gla_a):
    points = []
    acc = 0
    for s in IN_SIZES[:-1]:
        acc += s
        points.append(acc)
    wq, wk, wv, wqi, wki, wwi, wgq, wgk, wgv, wog, wa1, wgt = jnp.split(w_in, points, axis=-1)
    wq = wq * (math.log2(math.e) * HEAD_DIM ** -0.5)
    wqi = wqi * IDX_DIM ** -0.5
    pad_cols = lambda a, n: jnp.pad(a, ((0, 0), (0, n - a.shape[1])))
    wv_heads = wv.T.reshape(N_KV_HEADS, HEAD_DIM, D_MODEL)
    wv_aug = jnp.pad(wv_heads, ((0, 0), (0, LANES - HEAD_DIM), (0, 0))).reshape(N_KV_HEADS * LANES, D_MODEL)
    w = {
        "qT": wq.T, "qiT": wqi.T, "vT": wv_aug,
        "wiT": jnp.pad(wwi.T, ((0, 2 * SUBLANES - IDX_HEADS), (0, 0))),
        "row_p": pad_cols(jnp.concatenate([wk, wv, wki], axis=1), 2 * KV_WIDTH + LANES),
        "row_s": pad_cols(jnp.concatenate([wq, wqi, wk, wv, wki, wwi], axis=1),
                          ATT_WIDTH + IDX_HEADS * IDX_DIM + 2 * KV_WIDTH + LANES),
        "gla": jnp.concatenate([wgq, wgk, wgv, wog], axis=1),
        "a1": pad_cols(wa1, LANES),
        "a2": jnp.pad(w_gla_a2, ((0, LANES - GLA_GATE_RANK), (0, 0))),
        "gt": wgt,
    }
    w = {name: a.astype(BF16) for name, a in w.items()}
    w["ba"] = b_gla_a.astype(F32)[None]
    return w


def _round_up(x, m):
    return -(-x // m) * m


def _row_tile(n, pref, unit=2 * SUBLANES):
    best = unit
    for t in range(unit, min(n, pref) + 1, unit):
        if n % t == 0:
            best = t
    assert n % best == 0
    return best


def kernel(x_prompt, x_sample, cache_k, cache_v, cache_kidx, state_gla, page_table, meta_tokens, ln_in_g, ln_in_b, w_in, w_gla_a2, b_gla_a, gla_norm_g, w_proj_attn, w_proj_gla, w_out, ln1_g, ln1_b, w_ff_up, w_ff_down, ln2_g, ln2_b):
    depth = w_in.shape[0]
    assert depth == 1, "single-layer step only"
    B, S_p, D = x_prompt.shape
    DB, S_s, _ = x_sample.shape
    n_pages = page_table.shape[1]
    past = n_pages * PAGE_SIZE
    assert D == D_MODEL and S_p % GLA_CHUNK == 0 and S_s <= SUBLANES and n_pages % PAGES_PER_STEP == 0
    topk_prompt = min(TOPK_MAX, S_p // 4)
    topk_sample = min(TOPK_MAX, (past + S_s) // 4)
    alpha = (2 * depth) ** 0.25

    row = lambda a: a.astype(F32).reshape(1, -1)
    ln_g, ln_b = row(ln_in_g), row(ln_in_b)
    w = _pack_weights(w_in[0], w_gla_a2[0], b_gla_a[0])
    wa, wg, wo = (a[0].astype(BF16) for a in (w_proj_attn, w_proj_gla, w_out))
    wu, wd = w_ff_up[0].astype(BF16), w_ff_down[0].astype(BF16)
    gnorm = row(gla_norm_g[0])
    l1g, l1b, l2g, l2b = row(ln1_g[0]), row(ln1_b[0]), row(ln2_g[0]), row(ln2_b[0])

    T = S_p + N_META
    off = (-N_META) % GLA_CHUNK
    t_pad = _round_up(off + T, math.lcm(KEY_CHUNK, GLA_CHUNK))
    gla_tb = _row_tile(t_pad, GLA_TOKENS_PER_STEP, GLA_CHUNK)
    n_tok = B * t_pad
    meta = jnp.broadcast_to(meta_tokens.astype(x_prompt.dtype)[None], (B, N_META, D))
    xp = jnp.concatenate([jnp.zeros((B, off, D), x_prompt.dtype), meta, x_prompt,
                          jnp.zeros((B, t_pad - off - T, D), x_prompt.dtype)], axis=1).reshape(n_tok, D)

    qT, qiT, vT3, wiT, k32, v32, ki32, kb, kib = _proj_attn_prompt(xp, ln_g, ln_b, w, n_tok)
    att_p = _dsa_prompt(qT, qiT, wiT,
                        kb.reshape(n_tok // KEY_CHUNK, KEY_CHUNK, KV_WIDTH),
                        kib.reshape(n_tok // KEY_CHUNK, KEY_CHUNK, IDX_DIM), vT3,
                        batch=B, t_pad=t_pad, topk=topk_prompt, off=off)
    gq, gk, gv, og, la, gates_p = _proj_gla(xp, ln_g, ln_b, w, n_tok)
    gla_p, state_p = _gla(gq, gk, gv, og, la, gnorm, None, batch=B, t_pad=t_pad, tb=gla_tb,
                          chunk=GLA_CHUNK, off=off, t_end=off + T)
    h1_p = _merge(xp, att_p, gla_p, gates_p, ln_g, ln_b, wa, wg, wo, l1g, l1b, alpha=alpha)
    seq = lambda a: a.reshape((B, t_pad) + a.shape[1:])
    y_prompt = _ffn_window(seq(h1_p), wu, wd, l2g, l2b, alpha=alpha, start=off + N_META, length=S_p)

    k_prompt = seq(k32)[:, off:off + T].reshape(1, B, T, N_KV_HEADS, HEAD_DIM)
    v_prompt = seq(v32)[:, off:off + T].reshape(1, B, T, N_KV_HEADS, HEAD_DIM)
    kidx_prompt = seq(ki32)[:, off:off + T][None]
    gla_state_prompt = state_p[None]

    R = SAMPLE_ROWS
    n_tok_s = DB * R
    xs = jnp.pad(x_sample, ((0, 0), (0, R - S_s), (0, 0))).reshape(n_tok_s, D)
    q_s, qi_s, k_s, v_s, kiwi_s = _proj_attn_sample(xs, ln_g, ln_b, w, n_tok_s)
    sseq = lambda a: a.reshape((DB, R) + a.shape[1:])[:, :S_s]
    k_new, v_new = sseq(k_s), sseq(v_s)
    ki_new = sseq(kiwi_s)[..., :IDX_DIM]
    wi_new = sseq(kiwi_s)[..., IDX_DIM:IDX_DIM + IDX_HEADS]
    qi_stack = sseq(qi_s).reshape(DB, S_s, IDX_HEADS, IDX_DIM).transpose(0, 2, 1, 3).reshape(DB, IDX_HEADS * S_s, IDX_DIM)
    wcol = jnp.broadcast_to(wi_new.transpose(0, 2, 1).reshape(DB, IDX_HEADS * S_s, 1), (DB, IDX_HEADS * S_s, LANES))
    q_heads = sseq(q_s).reshape(DB, S_s, N_HEADS, HEAD_DIM).transpose(0, 2, 1, 3)
    kv_of_head = (jnp.arange(N_HEADS) // GROUP)[:, None] == jnp.arange(N_KV_HEADS)[None, :]
    qbd = jnp.where(kv_of_head[None, :, None, :, None], q_heads[:, :, :, None, :], jnp.zeros((), BF16))
    qbd = qbd.reshape(DB, N_HEADS * S_s, KV_WIDTH)
    pad_page = lambda a: jnp.pad(a, ((0, 0), (0, PAGE_SIZE - S_s), (0, 0))).astype(BF16).transpose(0, 2, 1)
    kinew_pg, knew_pg, vnew_pg = pad_page(ki_new), pad_page(k_new), pad_page(v_new)
    ck = cache_k[0].transpose(0, 2, 3, 1).reshape(-1, KV_WIDTH, PAGE_SIZE)
    cv = cache_v[0].transpose(0, 2, 3, 1).reshape(-1, KV_WIDTH, PAGE_SIZE)
    cki = cache_kidx[0].transpose(0, 2, 1)
    sc_past, sc_new = _idx_sample(page_table, qi_stack, wcol, kinew_pg, cki, s_q=S_s)
    thr, jmax = _sel_sample(sc_past, sc_new, topk=topk_sample)
    o_s = _att_sample(page_table, qbd, sc_past, sc_new, thr, jmax, knew_pg, vnew_pg, ck, cv, s_q=S_s)
    o_s = o_s.reshape(DB, N_KV_HEADS, GROUP, S_s, N_KV_HEADS, HEAD_DIM)
    att_s = jnp.stack([o_s[:, n, :, :, n, :] for n in range(N_KV_HEADS)], axis=1)
    att_s = att_s.transpose(0, 3, 1, 2, 4).reshape(DB, S_s, ATT_WIDTH).astype(BF16)
    att_s = jnp.pad(att_s, ((0, 0), (0, R - S_s), (0, 0))).reshape(n_tok_s, ATT_WIDTH)

    gq, gk, gv, og, la, gates_s = _proj_gla(xs, ln_g, ln_b, w, n_tok_s)
    gla_s, state_s = _gla(gq, gk, gv, og, la, gnorm, state_gla[0], batch=DB, t_pad=R, tb=R,
                          chunk=R, off=0, t_end=S_s)
    h1_s = _merge(xs, att_s, gla_s, gates_s, ln_g, ln_b, wa, wg, wo, l1g, l1b, alpha=alpha)
    y_s = _ffn(h1_s, wu, wd, l2g, l2b, alpha=alpha)

    y_sample = y_s.reshape(DB, R, D)[:, :S_s]
    k_sample = k_new.reshape(1, DB, S_s, N_KV_HEADS, HEAD_DIM)
    v_sample = v_new.reshape(1, DB, S_s, N_KV_HEADS, HEAD_DIM)
    kidx_sample = ki_new[None]
    gla_state_sample = state_s[None]
    return (y_prompt, y_sample, k_prompt, v_prompt, kidx_prompt, gla_state_prompt,
            k_sample, v_sample, kidx_sample, gla_state_sample)
```

```python
import functools
import math

import jax
import jax.numpy as jnp
from jax import lax
from jax.experimental import pallas as pl
from jax.experimental.pallas import tpu as pltpu

D_MODEL = 1024
PAGE_SIZE = 128
N_META = 16
N_HEADS = 16
HEAD_DIM = 64
N_KV_HEADS = 4
GROUP = N_HEADS // N_KV_HEADS
ATT_WIDTH = N_HEADS * HEAD_DIM
KV_WIDTH = N_KV_HEADS * HEAD_DIM
IDX_HEADS = 8
IDX_DIM = 64
TOPK_MAX = 256
GLA_HEADS = 4
GLA_DK = D_MODEL // 2 // GLA_HEADS
GLA_DV = D_MODEL // GLA_HEADS
GLA_KW = GLA_HEADS * GLA_DK
GLA_VW = GLA_HEADS * GLA_DV
GLA_GATE_RANK = 16
GLA_TAU = 16.0
GLA_CHUNK = 64
N_BRANCHES = 2
D_FF = 4 * D_MODEL
LN_EPS = 1e-5
IN_SIZES = (ATT_WIDTH, KV_WIDTH, KV_WIDTH, IDX_HEADS * IDX_DIM, IDX_DIM, IDX_HEADS,
            GLA_KW, GLA_KW, GLA_VW, GLA_VW, GLA_GATE_RANK, N_BRANCHES * D_MODEL)

LANES = 128
SUBLANES = 8
VMEM_LIMIT_BYTES = 56 * 1024 * 1024
PLANE_KEYS = 32
COUNT_LANES = 4
Q_BLOCK = LANES
KEY_CHUNK = 3 * LANES
SAMPLE_ROWS = 16
PAGES_PER_STEP = 16
PAGE_RING = 3
IDX_PAGES_PER_DOT = 4
SEL_SEQS = 16
TM_PROJ_GLA = 512
PROJ_ATTN_CHUNKS = 2
TM_TOKEN = 512
GLA_TOKENS_PER_STEP = 6 * GLA_CHUNK

F32 = jnp.float32
BF16 = jnp.bfloat16
NEG_INF = float("-inf")
INT_MIN = -2 ** 31
NT_DIMS = (((1,), (1,)), ((), ()))
TN_DIMS = (((0,), (0,)), ((), ()))


def _cparams(n_grid):
    return pltpu.CompilerParams(dimension_semantics=("arbitrary",) * n_grid,
                                vmem_limit_bytes=VMEM_LIMIT_BYTES)


def _full_spec(shape):
    nd = len(shape)
    return pl.BlockSpec(shape, lambda *_: (0,) * nd, pipeline_mode=pl.Buffered(1))


def _layer_norm(x, g, b):
    mu = jnp.mean(x, axis=-1, keepdims=True)
    xc = x - mu
    var = jnp.mean(xc * xc, axis=-1, keepdims=True)
    return xc * lax.rsqrt(var + LN_EPS) * g + b


def _bit_planes(words):
    a = list(words)
    for dist, mask in ((16, 0x0000FFFF), (8, 0x00FF00FF), (4, 0x0F0F0F0F), (2, 0x33333333), (1, 0x55555555)):
        k = 0
        while k < PLANE_KEYS:
            t = (a[k] ^ lax.shift_right_logical(a[k + dist], jnp.int32(dist))) & jnp.int32(mask)
            a[k] = a[k] ^ t
            a[k + dist] = a[k + dist] ^ lax.shift_left(t, jnp.int32(dist))
            k = (k + dist + 1) & ~dist
    return a


def _ukey_to_f32(u):
    bits = jnp.where(u < 0, u & jnp.int32(0x7FFFFFFF), ~u)
    return lax.bitcast_convert_type(bits, F32)


def _proj_attn_prompt_kernel(x_ref, g_ref, b_ref, wq_ref, wqi_ref, wv_ref, wwi_ref, wrow_ref,
                             qT_ref, qiT_ref, vT_ref, wiT_ref, k_ref, v_ref, ki_ref, kb_ref, kib_ref):
    hb = _layer_norm(x_ref[...], g_ref[...], b_ref[...]).astype(BF16)
    qT_ref[...] = lax.dot_general(wq_ref[...], hb, NT_DIMS, preferred_element_type=F32).astype(BF16)
    qiT_ref[...] = lax.dot_general(wqi_ref[...], hb, NT_DIMS, preferred_element_type=F32).astype(BF16)
    vt = lax.dot_general(wv_ref[...], hb, NT_DIMS, preferred_element_type=F32)
    row = lax.broadcasted_iota(jnp.int32, vt.shape, 0)
    vt = jnp.where(row % LANES == HEAD_DIM, 1.0, vt)
    for j in range(vT_ref.shape[0]):
        vT_ref[j] = vt[:, j * KEY_CHUNK:(j + 1) * KEY_CHUNK].astype(BF16)
    wi = lax.dot_general(wwi_ref[...], hb, NT_DIMS, preferred_element_type=F32)
    wiT_ref[...] = wi[:IDX_HEADS] * IDX_HEADS ** -0.5
    y = jnp.dot(hb, wrow_ref[...], preferred_element_type=F32)
    k = y[:, :KV_WIDTH]
    ki = y[:, 2 * KV_WIDTH:2 * KV_WIDTH + IDX_DIM]
    k_ref[...] = k
    v_ref[...] = y[:, KV_WIDTH:2 * KV_WIDTH]
    ki_ref[...] = ki
    kb_ref[...] = k.astype(BF16)
    kib_ref[...] = ki.astype(BF16)


def _proj_attn_prompt(x, ln_g, ln_b, w, n_tok):
    tm = _row_tile(n_tok, PROJ_ATTN_CHUNKS * KEY_CHUNK, KEY_CHUNK)
    n_steps = n_tok // tm
    cps = tm // KEY_CHUNK
    row = lambda n: pl.BlockSpec((tm, n), lambda i: (i, 0))
    col = lambda n: pl.BlockSpec((n, tm), lambda i: (0, i))
    out_shape = (
        jax.ShapeDtypeStruct((ATT_WIDTH, n_tok), BF16),
        jax.ShapeDtypeStruct((IDX_HEADS * IDX_DIM, n_tok), BF16),
        jax.ShapeDtypeStruct((n_tok // KEY_CHUNK, N_KV_HEADS * LANES, KEY_CHUNK), BF16),
        jax.ShapeDtypeStruct((IDX_HEADS, n_tok), F32),
        jax.ShapeDtypeStruct((n_tok, KV_WIDTH), F32),
        jax.ShapeDtypeStruct((n_tok, KV_WIDTH), F32),
        jax.ShapeDtypeStruct((n_tok, IDX_DIM), F32),
        jax.ShapeDtypeStruct((n_tok, KV_WIDTH), BF16),
        jax.ShapeDtypeStruct((n_tok, IDX_DIM), BF16),
    )
    out_specs = (col(ATT_WIDTH), col(IDX_HEADS * IDX_DIM),
                 pl.BlockSpec((cps, N_KV_HEADS * LANES, KEY_CHUNK), lambda i: (i, 0, 0)),
                 col(IDX_HEADS), row(KV_WIDTH), row(KV_WIDTH), row(IDX_DIM), row(KV_WIDTH), row(IDX_DIM))
    ws = (w["qT"], w["qiT"], w["vT"], w["wiT"], w["row_p"])
    return pl.pallas_call(
        _proj_attn_prompt_kernel,
        grid=(n_steps,),
        in_specs=[row(D_MODEL), _full_spec(ln_g.shape), _full_spec(ln_b.shape)] + [_full_spec(a.shape) for a in ws],
        out_specs=out_specs,
        out_shape=out_shape,
        compiler_params=_cparams(1),
        name="proj_attn_prompt",
    )(x, ln_g, ln_b, *ws)


def _proj_attn_sample_kernel(x_ref, g_ref, b_ref, w_ref, scale_ref, q_ref, qi_ref, k_ref, v_ref, kiwi_ref):
    hb = _layer_norm(x_ref[...], g_ref[...], b_ref[...]).astype(BF16)
    y = jnp.dot(hb, w_ref[...], preferred_element_type=F32)
    o = 0
    q_ref[...] = y[:, o:o + ATT_WIDTH].astype(BF16)
    o += ATT_WIDTH
    qi_ref[...] = y[:, o:o + IDX_HEADS * IDX_DIM].astype(BF16)
    o += IDX_HEADS * IDX_DIM
    k_ref[...] = y[:, o:o + KV_WIDTH]
    o += KV_WIDTH
    v_ref[...] = y[:, o:o + KV_WIDTH]
    o += KV_WIDTH
    kiwi_ref[...] = y[:, o:o + LANES] * scale_ref[...]


def _proj_attn_sample(x, ln_g, ln_b, w, n_tok):
    tm = _row_tile(n_tok, TM_TOKEN)
    row = lambda n: pl.BlockSpec((tm, n), lambda i: (i, 0))
    lane = lax.iota(jnp.int32, LANES)
    scale = jnp.where((lane >= IDX_DIM) & (lane < IDX_DIM + IDX_HEADS), IDX_HEADS ** -0.5, 1.0).astype(F32)[None]
    out_shape = (
        jax.ShapeDtypeStruct((n_tok, ATT_WIDTH), BF16),
        jax.ShapeDtypeStruct((n_tok, IDX_HEADS * IDX_DIM), BF16),
        jax.ShapeDtypeStruct((n_tok, KV_WIDTH), F32),
        jax.ShapeDtypeStruct((n_tok, KV_WIDTH), F32),
        jax.ShapeDtypeStruct((n_tok, LANES), F32),
    )
    return pl.pallas_call(
        _proj_attn_sample_kernel,
        grid=(n_tok // tm,),
        in_specs=[row(D_MODEL), _full_spec(ln_g.shape), _full_spec(ln_b.shape),
                  _full_spec(w["row_s"].shape), _full_spec(scale.shape)],
        out_specs=(row(ATT_WIDTH), row(IDX_HEADS * IDX_DIM), row(KV_WIDTH), row(KV_WIDTH), row(LANES)),
        out_shape=out_shape,
        compiler_params=_cparams(1),
        name="proj_attn_sample",
    )(x, ln_g, ln_b, w["row_s"], scale)


def _log_sigmoid(x):
    return jnp.minimum(x, 0.0) - jnp.log1p(jnp.exp(-jnp.abs(x)))


def _proj_gla_kernel(x_ref, g_ref, b_ref, wg_ref, wa1_ref, wa2_ref, ba_ref, wgt_ref,
                     gq_ref, gk_ref, gv_ref, og_ref, la_ref, gates_ref):
    hb = _layer_norm(x_ref[...], g_ref[...], b_ref[...]).astype(BF16)
    y = jnp.dot(hb, wg_ref[...], preferred_element_type=F32)
    gq_ref[...] = y[:, :GLA_KW]
    gk_ref[...] = y[:, GLA_KW:2 * GLA_KW]
    gv_ref[...] = y[:, 2 * GLA_KW:2 * GLA_KW + GLA_VW].astype(BF16)
    og_ref[...] = y[:, 2 * GLA_KW + GLA_VW:]
    a1 = jnp.dot(hb, wa1_ref[...], preferred_element_type=F32).astype(BF16)
    z = jnp.dot(a1, wa2_ref[...], preferred_element_type=F32) + ba_ref[...]
    la_ref[...] = _log_sigmoid(z) / GLA_TAU
    gt = jnp.dot(hb, wgt_ref[...], preferred_element_type=F32)
    gates_ref[...] = jax.nn.sigmoid(gt).astype(BF16)


def _proj_gla(x, ln_g, ln_b, w, n_tok):
    tm = _row_tile(n_tok, TM_PROJ_GLA)
    row = lambda n: pl.BlockSpec((tm, n), lambda i: (i, 0))
    ws = (w["gla"], w["a1"], w["a2"], w["ba"], w["gt"])
    out_shape = (
        jax.ShapeDtypeStruct((n_tok, GLA_KW), F32),
        jax.ShapeDtypeStruct((n_tok, GLA_KW), F32),
        jax.ShapeDtypeStruct((n_tok, GLA_VW), BF16),
        jax.ShapeDtypeStruct((n_tok, GLA_VW), F32),
        jax.ShapeDtypeStruct((n_tok, GLA_KW), F32),
        jax.ShapeDtypeStruct((n_tok, N_BRANCHES * D_MODEL), BF16),
    )
    return pl.pallas_call(
        _proj_gla_kernel,
        grid=(n_tok // tm,),
        in_specs=[row(D_MODEL), _full_spec(ln_g.shape), _full_spec(ln_b.shape)] + [_full_spec(a.shape) for a in ws],
        out_specs=(row(GLA_KW), row(GLA_KW), row(GLA_VW), row(GLA_VW), row(GLA_KW), row(N_BRANCHES * D_MODEL)),
        out_shape=out_shape,
        compiler_params=_cparams(1),
        name="proj_gla",
    )(x, ln_g, ln_b, *ws)


def _dsa_prompt_kernel(qT_ref, qiT_ref, wiT_ref, kb_ref, kib_ref, vT_ref, att_ref,
                       sc_ref, planes_ref, qn_ref, acc_ref, j_ref, sa_ref, sb_ref, mrun_ref, res_ref,
                       *, topk, off, n_pos_bits):
    kc = KEY_CHUNK
    groups_per_pair = 2 * kc // (SUBLANES * PLANE_KEYS)
    i = pl.program_id(1)

    @pl.when((pl.program_id(0) == 0) & (i == 0))
    def _():
        planes_ref[...] = jnp.zeros(planes_ref.shape, jnp.int32)
        qn_ref[...] = jnp.zeros(qn_ref.shape, BF16)

    n_chunks = (i * Q_BLOCK + Q_BLOCK + kc - 1) // kc
    qpos = i * Q_BLOCK + lax.broadcasted_iota(jnp.int32, (1, Q_BLOCK), 1)

    def key_pos(c):
        return c * kc + lax.broadcasted_iota(jnp.int32, (kc, Q_BLOCK), 0)

    def run_indexer():
        wi = wiT_ref[...]

        def chunk_scores(c):
            kic = kib_ref[c]
            acc = jnp.zeros((kc, Q_BLOCK), F32)
            for hp in range(IDX_HEADS // 2):
                r0 = 2 * hp * IDX_DIM
                rhs = jnp.concatenate([qiT_ref[r0:r0 + IDX_DIM, :], qiT_ref[r0 + IDX_DIM:r0 + 2 * IDX_DIM, :]], axis=1)
                s = jnp.maximum(jnp.dot(kic, rhs, preferred_element_type=F32), 0.0)
                acc = acc + s[:, :Q_BLOCK] * wi[2 * hp:2 * hp + 1, :] + s[:, Q_BLOCK:] * wi[2 * hp + 1:2 * hp + 2, :]
            kp = key_pos(c)
            valid = (kp >= off) & (kp <= qpos)
            sc = jnp.where(valid, acc, NEG_INF)
            sc_ref[c] = sc
            bits = lax.bitcast_convert_type(sc, jnp.int32)
            return (bits ^ ((bits >> 31) | jnp.int32(INT_MIN))).reshape(kc // SUBLANES, SUBLANES, Q_BLOCK)

        def idx_body(t, carry):
            c1 = 2 * t + 1
            keys0 = chunk_scores(2 * t)
            keys1 = jnp.where(c1 < n_chunks, chunk_scores(jnp.minimum(c1, n_chunks - 1)), 0)
            words = [keys0[k] for k in range(keys0.shape[0])] + [keys1[k] for k in range(keys1.shape[0])]
            for g in range(groups_per_pair):
                for b, plane in enumerate(_bit_planes(words[g * PLANE_KEYS:(g + 1) * PLANE_KEYS])):
                    planes_ref[groups_per_pair * t + g, b] = plane
            return carry

        lax.fori_loop(0, (n_chunks + 1) // 2, idx_body, 0)

    def chunk_counts(pred_c, accs):
        v = jnp.where(pred_c, 1.0, 0.0).reshape(COUNT_LANES, kc // (SUBLANES * COUNT_LANES), SUBLANES, Q_BLOCK)
        return tuple(a + jnp.sum(v[r], axis=0) for r, a in enumerate(accs))

    zero_accs = (jnp.zeros((SUBLANES, Q_BLOCK), F32),) * COUNT_LANES

    def total(accs):
        return jnp.sum(functools.reduce(lambda a, b: a + b, accs), axis=0, keepdims=True)

    def count(pred):
        return total(lax.fori_loop(0, n_chunks, lambda c, accs: chunk_counts(pred(c), accs), zero_accs))

    def search():
        n_groups = planes_ref.shape[0]

        def step(t, carry):
            res, want, alive = carry
            planes = [planes_ref[g, t] for g in range(n_groups)]
            ones = functools.reduce(lambda x, y: x + y, [lax.population_count(a & p) for a, p in zip(alive, planes)])
            cnt = jnp.sum(ones, axis=0, keepdims=True)
            take = cnt >= want
            res = jnp.where(take, res | jnp.left_shift(jnp.int32(1), 31 - t), res)
            want = jnp.where(take, want, want - cnt)
            keep = jnp.where(take, jnp.int32(0), jnp.int32(-1))
            return res, want, tuple(a & (p ^ keep) for a, p in zip(alive, planes))

        alive = tuple(jnp.full((SUBLANES, Q_BLOCK), jnp.where(g // groups_per_pair < (n_chunks + 1) // 2, -1, 0), jnp.int32)
                      for g in range(n_groups))
        start = (jnp.zeros((1, Q_BLOCK), jnp.int32), jnp.full((1, Q_BLOCK), topk, jnp.int32), alive)
        return lax.fori_loop(0, 32, step, start)[0]

    def search_by_compares():
        def bit_body(t, res):
            trial = res | jnp.left_shift(jnp.int32(1), 31 - t)
            cand = _ukey_to_f32(trial)
            cnt = count(lambda c: sc_ref[c] >= cand)
            return jnp.where(cnt >= topk, trial, res)
        return lax.fori_loop(0, 32, bit_body, jnp.zeros((1, Q_BLOCK), jnp.int32))

    few = (qpos - off + 1) <= topk

    def threshold_counts(res):
        thr = jnp.where(few, NEG_INF, _ukey_to_f32(res))

        def both_counts(c, accs):
            sc = sc_ref[c]
            return chunk_counts(sc > thr, accs[:COUNT_LANES]) + chunk_counts(sc >= thr, accs[COUNT_LANES:])

        accs = lax.fori_loop(0, n_chunks, both_counts, zero_accs + zero_accs)
        return thr, total(accs[:COUNT_LANES]), total(accs[COUNT_LANES:])

    def finish_selection():
        res_ref[...] = search()
        _, above, not_below = threshold_counts(res_ref[...])
        consistent = few | ((above < topk) & (not_below >= topk))

        @pl.when(jnp.min(jnp.where(consistent, 1.0, 0.0)) < 1.0)
        def _():
            res_ref[...] = search_by_compares()

        thr, cnt_gt, cnt_ge = threshold_counts(res_ref[...])
        n_ties = topk - cnt_gt
        tie_rows = jnp.where(few, 0.0, jnp.where(cnt_ge > topk, 1.0, 0.0))
        j_ref[...] = jnp.full(j_ref.shape, 2 ** 30, jnp.int32)

        @pl.when(jnp.max(tie_rows) > 0.0)
        def _():
            def jbit_body(t, resj):
                trial = resj | jnp.left_shift(jnp.int32(1), n_pos_bits - 1 - t)
                below = count(lambda c: (sc_ref[c] == thr) & (key_pos(c) < trial))
                return jnp.where(below < n_ties, trial, resj)
            j_ref[...] = lax.fori_loop(0, n_pos_bits, jbit_body, jnp.zeros((1, Q_BLOCK), jnp.int32))

        jmax = j_ref[...]

        def bias_body(c, carry):
            sc = sc_ref[c]
            kp = key_pos(c)
            valid = (kp >= off) & (kp <= qpos)
            sel = (sc > thr) | ((sc == thr) & (kp <= jmax))
            sc_ref[c] = jnp.where(valid & sel, 0.0, NEG_INF)
            return carry

        lax.fori_loop(0, n_chunks, bias_body, 0)

    gw = GROUP * Q_BLOCK
    last = n_chunks - 1
    m_none = jnp.full((1, N_HEADS * Q_BLOCK), NEG_INF, F32)

    def score_stage(c, buf_ref, m_run):
        bias = jnp.concatenate([sc_ref[c]] * N_HEADS, axis=1)
        s = jnp.dot(kb_ref[c], qn_ref[...], preferred_element_type=F32) + bias
        buf_ref[...] = s
        return jnp.maximum(m_run, jnp.max(s, axis=0, keepdims=True))

    def prob_stage(c, buf_ref, m_before, m_with):
        m_safe = jnp.maximum(m_with, -1e30)
        alpha = jnp.exp2(m_before - m_safe)
        p = jnp.exp2(buf_ref[...] - m_safe).astype(BF16)
        vt = vT_ref[c]
        for n in range(N_KV_HEADS):
            cols = slice(n * gw, (n + 1) * gw)
            pv = jnp.dot(vt[n * LANES:(n + 1) * LANES, :], p[:, cols], preferred_element_type=F32)
            acc_ref[n] = acc_ref[n] * alpha[:, cols] + pv

    def attention_start():
        for h in range(N_HEADS):
            n = h // GROUP
            qn_ref[n * HEAD_DIM:(n + 1) * HEAD_DIM, h * Q_BLOCK:(h + 1) * Q_BLOCK] = qT_ref[h * HEAD_DIM:(h + 1) * HEAD_DIM, :]
        acc_ref[...] = jnp.zeros(acc_ref.shape, F32)
        return m_none, score_stage(0, sa_ref, m_none)

    def pair_step(t, m_prev, m_cur):
        c0 = 2 * t
        m_1 = score_stage(c0 + 1, sb_ref, m_cur)
        prob_stage(c0, sa_ref, m_prev, m_cur)
        m_2 = score_stage(c0 + 2, sa_ref, m_1)
        prob_stage(c0 + 1, sb_ref, m_cur, m_1)
        return m_1, m_2

    def attention_loop():
        n_pairs = (n_chunks - 1) // 2
        m_prev, m_cur = lax.fori_loop(0, n_pairs, lambda t, ms: pair_step(t, *ms), attention_start())
        mrun_ref[0] = m_prev
        mrun_ref[1] = m_cur
        c_a = 2 * n_pairs

        @pl.when(c_a == last)
        def _():
            prob_stage(c_a, sa_ref, mrun_ref[0], mrun_ref[1])

        @pl.when(c_a < last)
        def _():
            m_1 = score_stage(c_a + 1, sb_ref, mrun_ref[1])
            prob_stage(c_a, sa_ref, mrun_ref[0], mrun_ref[1])
            prob_stage(c_a + 1, sb_ref, mrun_ref[1], m_1)

    def attention_finish():
        heads = []
        for n in range(N_KV_HEADS):
            a = acc_ref[n]
            denom = a[HEAD_DIM:HEAD_DIM + 1, :]
            o = a[:HEAD_DIM, :] / jnp.where(denom > 0.0, denom, 1.0)
            for g in range(GROUP):
                heads.append(o[:, g * Q_BLOCK:(g + 1) * Q_BLOCK])
        att_ref[...] = jnp.concatenate(heads, axis=0).T.astype(BF16)

    run_indexer()
    finish_selection()
    attention_loop()
    attention_finish()


def _dsa_prompt(qT, qiT, wiT, kb3, kib3, vT3, *, batch, t_pad, topk, off):
    kc = KEY_CHUNK
    nqb = t_pad // Q_BLOCK
    ncb = t_pad // kc
    n_tok = batch * t_pad
    qcol = lambda n: pl.BlockSpec((n, Q_BLOCK), lambda b, i: (0, b * nqb + i))
    kern = functools.partial(_dsa_prompt_kernel, topk=topk, off=off,
                             n_pos_bits=max(1, math.ceil(math.log2(t_pad))))
    return pl.pallas_call(
        kern,
        grid=(batch, nqb),
        in_specs=[qcol(ATT_WIDTH), qcol(IDX_HEADS * IDX_DIM), qcol(IDX_HEADS),
                  pl.BlockSpec((ncb, kc, KV_WIDTH), lambda b, i: (b, 0, 0)),
                  pl.BlockSpec((ncb, kc, IDX_DIM), lambda b, i: (b, 0, 0)),
                  pl.BlockSpec((ncb, N_KV_HEADS * LANES, kc), lambda b, i: (b, 0, 0))],
        out_specs=pl.BlockSpec((Q_BLOCK, ATT_WIDTH), lambda b, i: (b * nqb + i, 0)),
        out_shape=jax.ShapeDtypeStruct((n_tok, ATT_WIDTH), BF16),
        scratch_shapes=[pltpu.VMEM((ncb, kc, Q_BLOCK), F32),
                        pltpu.VMEM((-(-ncb // 2) * (2 * kc // (SUBLANES * PLANE_KEYS)), PLANE_KEYS, SUBLANES, Q_BLOCK),
                                   jnp.int32),
                        pltpu.VMEM((KV_WIDTH, N_HEADS * Q_BLOCK), BF16),
                        pltpu.VMEM((N_KV_HEADS, LANES, GROUP * Q_BLOCK), F32),
                        pltpu.VMEM((1, Q_BLOCK), jnp.int32),
                        pltpu.VMEM((kc, N_HEADS * Q_BLOCK), F32),
                        pltpu.VMEM((kc, N_HEADS * Q_BLOCK), F32),
                        pltpu.VMEM((2, 1, N_HEADS * Q_BLOCK), F32),
                        pltpu.VMEM((1, Q_BLOCK), jnp.int32)],
        compiler_params=_cparams(2),
        name="dsa_prompt",
    )(qT, qiT, wiT, kb3, kib3, vT3)


def _idx_sample_kernel(pt_ref, qi_ref, wcol_ref, kinew_ref, cki_hbm, past_ref, new_ref, kibuf, sem):
    ring, n_pages = kibuf.shape[:2]
    ahead = ring - 1
    s_q = new_ref.shape[0]
    b = pl.program_id(0)
    n_seq = pl.num_programs(0)
    slot = b % ring

    def page_copies(seq, dst_slot):
        return [pltpu.make_async_copy(cki_hbm.at[pt_ref[seq, r]], kibuf.at[dst_slot, r], sem.at[dst_slot])
                for r in range(n_pages)]

    @pl.when(b == 0)
    def _():
        for s in range(ahead):
            @pl.when(s < n_seq)
            def _():
                for cp in page_copies(s, s):
                    cp.start()

    @pl.when(b + ahead < n_seq)
    def _():
        for cp in page_copies(b + ahead, (b + ahead) % ring):
            cp.start()

    qi = qi_ref[...]

    def scores(keys_t_bf16):
        s = jnp.dot(qi, keys_t_bf16, preferred_element_type=F32)
        wcol = jnp.concatenate([wcol_ref[...]] * (s.shape[1] // LANES), axis=1)
        s = jnp.maximum(s, 0.0) * wcol
        return jnp.sum(s.reshape(IDX_HEADS, s_q, s.shape[1]), axis=0)

    s_new = scores(kinew_ref[...])
    qrow = lax.broadcasted_iota(jnp.int32, s_new.shape, 0)
    kcol = lax.broadcasted_iota(jnp.int32, s_new.shape, 1)
    new_ref[...] = jnp.where(kcol <= qrow, s_new, NEG_INF)

    for cp in page_copies(b, slot):
        cp.wait()
    npg = PAGES_PER_STEP
    for r0 in range(0, n_pages, IDX_PAGES_PER_DOT):
        keys = jnp.concatenate([kibuf[slot, r0 + j].astype(BF16) for j in range(IDX_PAGES_PER_DOT)], axis=1)
        c, p = divmod(r0, npg)
        past_ref[c, :, p * PAGE_SIZE:(p + IDX_PAGES_PER_DOT) * PAGE_SIZE] = scores(keys)


def _idx_sample(page_table, qi_stack, wcol, kinew, cache_kidx, *, s_q):
    db, n_pages = page_table.shape
    npg = PAGES_PER_STEP
    n_steps = n_pages // npg
    rows = IDX_HEADS * s_q
    per_seq = lambda r, n: pl.BlockSpec((None, r, n), lambda b, pt: (b, 0, 0))
    grid_spec = pltpu.PrefetchScalarGridSpec(
        num_scalar_prefetch=1,
        grid=(db,),
        in_specs=[per_seq(rows, IDX_DIM), per_seq(rows, LANES), per_seq(IDX_DIM, PAGE_SIZE),
                  pl.BlockSpec(memory_space=pl.ANY)],
        out_specs=(pl.BlockSpec((n_steps, None, s_q, npg * PAGE_SIZE), lambda b, pt: (0, b, 0, 0)),
                   per_seq(s_q, PAGE_SIZE)),
        scratch_shapes=[pltpu.VMEM((PAGE_RING, n_pages, IDX_DIM, PAGE_SIZE), F32),
                        pltpu.SemaphoreType.DMA((PAGE_RING,))],
    )
    return pl.pallas_call(
        _idx_sample_kernel,
        grid_spec=grid_spec,
        out_shape=(jax.ShapeDtypeStruct((n_steps, db, s_q, npg * PAGE_SIZE), F32),
                   jax.ShapeDtypeStruct((db, s_q, PAGE_SIZE), F32)),
        compiler_params=_cparams(1),
        name="idx_sample",
    )(page_table, qi_stack, wcol, kinew, cache_kidx)


def _sel_sample_kernel(past_ref, new_ref, thr_ref, j_ref, planes_ref, res_ref, *, topk, n_pos_bits):
    n_c, g, s_q, w = past_ref.shape
    rows = g * s_q
    l_past = n_c * w
    lane_pos = lax.broadcasted_iota(jnp.int32, (rows, LANES), 1)

    def lane_tile_sum(v):
        return functools.reduce(lambda a, b: a + b, [v[:, t * LANES:(t + 1) * LANES] for t in range(v.shape[1] // LANES)])

    def count(pred):
        def body(cc, acc):
            x = past_ref[cc].reshape(rows, w)
            pos = cc * w + lax.broadcasted_iota(jnp.int32, (rows, w), 1)
            return acc + lane_tile_sum(jnp.where(pred(x, pos), 1.0, 0.0))
        acc = lax.fori_loop(0, n_c, body, jnp.zeros((rows, LANES), F32))
        acc = acc + jnp.where(pred(new_ref[...].reshape(rows, PAGE_SIZE), l_past + lane_pos), 1.0, 0.0)
        return jnp.sum(acc, axis=1, keepdims=True)

    def ukeys(x):
        bits = lax.bitcast_convert_type(x, jnp.int32)
        return bits ^ ((bits >> 31) | jnp.int32(INT_MIN))

    tiles = [(cc, k) for cc in range(n_c) for k in range(w // LANES)] + [None]
    n_groups = planes_ref.shape[0]
    no_key = jnp.zeros((rows, LANES), jnp.int32)
    for gi in range(n_groups):
        words = []
        for tile in (tiles[gi * PLANE_KEYS:(gi + 1) * PLANE_KEYS] + [False] * PLANE_KEYS)[:PLANE_KEYS]:
            if tile is None:
                words.append(ukeys(new_ref[...].reshape(rows, PAGE_SIZE)))
            elif tile is False:
                words.append(no_key)
            else:
                words.append(ukeys(past_ref[tile[0], :, :, tile[1] * LANES:(tile[1] + 1) * LANES].reshape(rows, LANES)))
        for t, plane in enumerate(_bit_planes(words)):
            planes_ref[gi, t] = plane

    def step(t, carry):
        res, want, alive = carry
        planes = [planes_ref[gi, t] for gi in range(n_groups)]
        ones = functools.reduce(lambda x, y: x + y, [lax.population_count(a & p) for a, p in zip(alive, planes)])
        cnt = jnp.sum(ones, axis=1, keepdims=True)
        take = cnt >= want
        res = jnp.where(take, res | jnp.left_shift(jnp.int32(1), 31 - t), res)
        want = jnp.where(take, want, want - cnt)
        keep = jnp.where(take, jnp.int32(0), jnp.int32(-1))
        return res, want, tuple(a & (p ^ keep) for a, p in zip(alive, planes))

    start = (jnp.zeros((rows, 1), jnp.int32), jnp.full((rows, 1), topk, jnp.int32),
             (jnp.full((rows, LANES), -1, jnp.int32),) * n_groups)
    res = lax.fori_loop(0, 32, step, start)[0]

    res_ref[...] = jnp.broadcast_to(res, (rows, LANES))
    thr = _ukey_to_f32(res)
    consistent = (count(lambda x, pos: x > thr) < topk) & (count(lambda x, pos: x >= thr) >= topk)

    @pl.when(jnp.min(jnp.where(consistent, 1.0, 0.0)) < 1.0)
    def _():
        def bit_body(t, r):
            trial = r | jnp.left_shift(jnp.int32(1), 31 - t)
            cand = _ukey_to_f32(trial)
            return jnp.where(count(lambda x, pos: x >= cand) >= topk, trial, r)
        r = lax.fori_loop(0, 32, bit_body, jnp.zeros((rows, 1), jnp.int32))
        res_ref[...] = jnp.broadcast_to(r, (rows, LANES))

    thr = _ukey_to_f32(res_ref[:, :1])
    n_ties = topk - count(lambda x, pos: x > thr)
    cnt_ge = count(lambda x, pos: x >= thr)
    thr_ref[...] = jnp.broadcast_to(thr, (rows, LANES))
    j_ref[...] = jnp.full((rows, LANES), 2 ** 30, jnp.int32)

    @pl.when(jnp.max(jnp.where(cnt_ge > topk, 1.0, 0.0)) > 0.0)
    def _():
        def jbit_body(t, resj):
            trial = resj | jnp.left_shift(jnp.int32(1), n_pos_bits - 1 - t)
            below = count(lambda x, pos: (x == thr) & (pos < trial))
            return jnp.where(below < n_ties, trial, resj)
        jmax = lax.fori_loop(0, n_pos_bits, jbit_body, jnp.zeros((rows, 1), jnp.int32))
        j_ref[...] = jnp.broadcast_to(jmax, (rows, LANES))


def _sel_sample(sc_past, sc_new, *, topk):
    n_c, db, s_q, w = sc_past.shape
    g = math.gcd(SEL_SEQS, db)
    kern = functools.partial(_sel_sample_kernel, topk=topk,
                             n_pos_bits=max(1, math.ceil(math.log2(n_c * w + PAGE_SIZE))))
    return pl.pallas_call(
        kern,
        grid=(db // g,),
        in_specs=[pl.BlockSpec((n_c, g, s_q, w), lambda i: (0, i, 0, 0)),
                  pl.BlockSpec((g, s_q, PAGE_SIZE), lambda i: (i, 0, 0))],
        out_specs=(pl.BlockSpec((g * s_q, LANES), lambda i: (i, 0)),
                   pl.BlockSpec((g * s_q, LANES), lambda i: (i, 0))),
        out_shape=(jax.ShapeDtypeStruct((db * s_q, LANES), F32),
                   jax.ShapeDtypeStruct((db * s_q, LANES), jnp.int32)),
        scratch_shapes=[pltpu.VMEM((-(-(n_c * w // LANES + 1) // PLANE_KEYS), PLANE_KEYS, g * s_q, LANES), jnp.int32),
                        pltpu.VMEM((g * s_q, LANES), jnp.int32)],
        compiler_params=_cparams(1),
        name="sel_sample",
    )(sc_past, sc_new)


def _att_sample_kernel(pt_ref, qbd_ref, sc_past_ref, sc_new_ref, thr_ref, j_ref, knew_ref, vnew_ref, ck_hbm, cv_hbm,
                       att_ref, kbuf, vbuf, ksem, vsem, acc_ref, m_ref, l_ref):
    ring, npg = kbuf.shape[:2]
    ahead = ring - 1
    n_steps = sc_past_ref.shape[0]
    b = pl.program_id(0)
    n_seq = pl.num_programs(0)
    qbd = qbd_ref[...]
    thr = thr_ref[...]
    jmax = j_ref[...]

    def page_copies(seq, step, slot):
        cps = []
        for r in range(npg):
            page = pt_ref[seq, step * npg + r]
            cps.append(pltpu.make_async_copy(ck_hbm.at[page], kbuf.at[slot, r], ksem.at[slot]))
            cps.append(pltpu.make_async_copy(cv_hbm.at[page], vbuf.at[slot, r], vsem.at[slot]))
        return cps

    @pl.when(b == 0)
    def _():
        for g in range(ahead):
            for cp in page_copies(0, g, g):
                cp.start()

    acc_ref[...] = jnp.zeros(acc_ref.shape, F32)
    m_ref[...] = jnp.full(m_ref.shape, NEG_INF, F32)
    l_ref[...] = jnp.zeros(l_ref.shape, F32)

    def masked_scores(keys_t_bf16, sc, pos0):
        pos = pos0 + lax.broadcasted_iota(jnp.int32, sc.shape, 1)
        tile = lambda a: jnp.concatenate([a] * (sc.shape[1] // LANES), axis=1)
        sel = (sc > tile(thr)) | ((sc == tile(thr)) & (pos <= tile(jmax)))
        bias = jnp.where(sel, 0.0, NEG_INF)
        s = jnp.dot(qbd, keys_t_bf16, preferred_element_type=F32)
        return s + jnp.concatenate([bias] * N_HEADS, axis=0)

    def accumulate(s_list, vt_list):
        m_old = m_ref[...]
        m_new = jnp.maximum(m_old, jnp.max(functools.reduce(jnp.maximum, s_list), axis=1, keepdims=True))
        m_safe = jnp.maximum(m_new, -1e30)
        alpha = jnp.exp2(m_old - m_safe)
        acc = acc_ref[...] * alpha
        p_list = [jnp.exp2(s - m_safe) for s in s_list]
        for p, vt in zip(p_list, vt_list):
            acc = acc + lax.dot_general(p.astype(BF16), vt, NT_DIMS, preferred_element_type=F32)
        acc_ref[...] = acc
        l_ref[...] = l_ref[...] * alpha + jnp.sum(functools.reduce(lambda a, b: a + b, p_list), axis=1, keepdims=True)
        m_ref[...] = m_new

    for c in range(n_steps):
        group = b * n_steps + c
        slot = group % ring
        slot_ahead = (group + ahead) % ring
        if c + ahead < n_steps:
            for cp in page_copies(b, c + ahead, slot_ahead):
                cp.start()
        else:
            @pl.when(b + 1 < n_seq)
            def _():
                for cp in page_copies(b + 1, c + ahead - n_steps, slot_ahead):
                    cp.start()
        for cp in page_copies(b, c, slot):
            cp.wait()
        s_list, v_list = [], []
        for p in range(0, npg, 2):
            sc = sc_past_ref[c, :, p * PAGE_SIZE:(p + 2) * PAGE_SIZE]
            keys = jnp.concatenate([kbuf[slot, p].astype(BF16), kbuf[slot, p + 1].astype(BF16)], axis=1)
            s_list.append(masked_scores(keys, sc, (c * npg + p) * PAGE_SIZE))
            v_list.append(jnp.concatenate([vbuf[slot, p].astype(BF16), vbuf[slot, p + 1].astype(BF16)], axis=1))
        accumulate(s_list, v_list)

    accumulate([masked_scores(knew_ref[...], sc_new_ref[...], n_steps * npg * PAGE_SIZE)], [vnew_ref[...]])
    att_ref[...] = acc_ref[...] / l_ref[...]


def _att_sample(page_table, qbd, sc_past, sc_new, thr, jmax, knew, vnew, cache_k, cache_v, *, s_q):
    db, n_pages = page_table.shape
    npg = PAGES_PER_STEP
    n_steps = n_pages // npg
    rows = N_HEADS * s_q
    assert n_steps >= PAGE_RING - 1
    per_seq = lambda r, n: pl.BlockSpec((None, r, n), lambda b, pt: (b, 0, 0))
    page_buf = pltpu.VMEM((PAGE_RING, npg, KV_WIDTH, PAGE_SIZE), F32)
    ring_sem = pltpu.SemaphoreType.DMA((PAGE_RING,))
    grid_spec = pltpu.PrefetchScalarGridSpec(
        num_scalar_prefetch=1,
        grid=(db,),
        in_specs=[per_seq(rows, KV_WIDTH),
                  pl.BlockSpec((n_steps, None, s_q, npg * PAGE_SIZE), lambda b, pt: (0, b, 0, 0)),
                  per_seq(s_q, PAGE_SIZE),
                  pl.BlockSpec((s_q, LANES), lambda b, pt: (b, 0)),
                  pl.BlockSpec((s_q, LANES), lambda b, pt: (b, 0)),
                  per_seq(KV_WIDTH, PAGE_SIZE), per_seq(KV_WIDTH, PAGE_SIZE),
                  pl.BlockSpec(memory_space=pl.ANY), pl.BlockSpec(memory_space=pl.ANY)],
        out_specs=per_seq(rows, KV_WIDTH),
        scratch_shapes=[page_buf, page_buf, ring_sem, ring_sem,
                        pltpu.VMEM((rows, KV_WIDTH), F32), pltpu.VMEM((rows, 1), F32), pltpu.VMEM((rows, 1), F32)],
    )
    return pl.pallas_call(
        _att_sample_kernel,
        grid_spec=grid_spec,
        out_shape=jax.ShapeDtypeStruct((db, rows, KV_WIDTH), F32),
        compiler_params=_cparams(1),
        name="att_sample",
    )(page_table, qbd, sc_past, sc_new, thr, jmax, knew, vnew, cache_k, cache_v)


def _gla_kernel(*refs, chunk, n_chunks, off, t_end, has_s0):
    if has_s0:
        gq_ref, gk_ref, gv_ref, og_ref, la_ref, gn_ref, s0_ref, o_ref, sfin_ref, st_ref = refs
    else:
        gq_ref, gk_ref, gv_ref, og_ref, la_ref, gn_ref, o_ref, sfin_ref, st_ref = refs
    j = pl.program_id(1)
    tb = chunk * n_chunks

    @pl.when(j == 0)
    def _():
        for hh in range(GLA_HEADS):
            st_ref[hh] = s0_ref[hh].T if has_s0 else jnp.zeros((GLA_DV, GLA_DK), F32)

    r_i = lax.broadcasted_iota(jnp.int32, (chunk, chunk), 0)
    c_i = lax.broadcasted_iota(jnp.int32, (chunk, chunk), 1)
    causal = r_i >= c_i
    tril = jnp.where(causal, 1.0, 0.0)
    gn = gn_ref[...]

    def chunk_body(c, carry):
        r0 = pl.multiple_of(c * chunk, chunk)
        rows = pl.ds(r0, chunk)
        pos = j * tb + r0 + lax.broadcasted_iota(jnp.int32, (chunk, GLA_KW), 0)
        valid = (pos >= off) & (pos < t_end)
        la = jnp.where(valid, la_ref[rows, :], 0.0)
        k = jnp.where(valid, gk_ref[rows, :], 0.0)
        q = gq_ref[rows, :] * GLA_DK ** -0.5
        v = gv_ref[rows, :]
        b = jnp.dot(tril, la, preferred_element_type=F32, precision=lax.Precision.HIGHEST)
        b_last = b[chunk - 1:chunk, :]
        qd = (q * jnp.exp(b)).astype(BF16)
        kd = (k * jnp.exp(-b)).astype(BF16)
        ke = (k * jnp.exp(b_last - b)).astype(BF16)
        decay = jnp.exp(b_last)
        outs = []
        for hh in range(GLA_HEADS):
            ks = slice(hh * GLA_DK, (hh + 1) * GLA_DK)
            vh = v[:, hh * GLA_DV:(hh + 1) * GLA_DV]
            a = lax.dot_general(qd[:, ks], kd[:, ks], NT_DIMS, preferred_element_type=F32)
            a = jnp.where(causal, a, 0.0).astype(BF16)
            st = st_ref[hh]
            o = (jnp.dot(a, vh, preferred_element_type=F32)
                 + lax.dot_general(qd[:, ks], st.astype(BF16), NT_DIMS, preferred_element_type=F32))
            u_t = lax.dot_general(vh, ke[:, ks], TN_DIMS, preferred_element_type=F32)
            st_ref[hh] = decay[:, ks] * st + u_t
            o = o * lax.rsqrt(jnp.mean(o * o, axis=-1, keepdims=True) + LN_EPS) * gn
            outs.append(o)
        og = og_ref[rows, :]
        o_ref[rows, :] = (jnp.concatenate(outs, axis=1) * (og * jax.nn.sigmoid(og))).astype(BF16)
        return carry

    per_step = next(k for k in (6, 3, 2, 1) if n_chunks % k == 0)

    def step_body(t, carry):
        for k in range(per_step):
            chunk_body(per_step * t + k, carry)
        return carry

    lax.fori_loop(0, n_chunks // per_step, step_body, 0)

    @pl.when(j == pl.num_programs(1) - 1)
    def _():
        for hh in range(GLA_HEADS):
            sfin_ref[hh] = st_ref[hh].T


def _gla(gq, gk, gv, og, la, gnorm, s0, *, batch, t_pad, tb, chunk, off, t_end):
    n_steps = t_pad // tb
    tok = lambda n: pl.BlockSpec((tb, n), lambda b, j: (b * n_steps + j, 0))
    state = pl.BlockSpec((None, GLA_HEADS, GLA_DK, GLA_DV), lambda b, j: (b, 0, 0, 0))
    has_s0 = s0 is not None
    kern = functools.partial(_gla_kernel, chunk=chunk, n_chunks=tb // chunk, off=off, t_end=t_end, has_s0=has_s0)
    in_specs = [tok(GLA_KW), tok(GLA_KW), tok(GLA_VW), tok(GLA_VW), tok(GLA_KW), _full_spec(gnorm.shape)]
    args = [gq, gk, gv, og, la, gnorm]
    if has_s0:
        in_specs.append(state)
        args.append(s0)
    return pl.pallas_call(
        kern,
        grid=(batch, n_steps),
        in_specs=in_specs,
        out_specs=(tok(GLA_VW), state),
        out_shape=(jax.ShapeDtypeStruct((batch * t_pad, GLA_VW), BF16),
                   jax.ShapeDtypeStruct((batch, GLA_HEADS, GLA_DK, GLA_DV), F32)),
        scratch_shapes=[pltpu.VMEM((GLA_HEADS, GLA_DV, GLA_DK), F32)],
        compiler_params=_cparams(2),
        name="gla",
    )(*args)


def _merge_kernel(x_ref, att_ref, gla_ref, gates_ref, lng_ref, lnb_ref, wa_ref, wg_ref, wo_ref,
                  l1g_ref, l1b_ref, h1_ref, *, alpha):
    h = _layer_norm(x_ref[...], lng_ref[...], lnb_ref[...])
    pa = jnp.dot(att_ref[...], wa_ref[...], preferred_element_type=F32)
    pg = jnp.dot(gla_ref[...], wg_ref[...], preferred_element_type=F32)
    gates = gates_ref[...].astype(F32)
    merged = gates[:, :D_MODEL] * pa + gates[:, D_MODEL:] * pg
    mix = jnp.dot(merged.astype(BF16), wo_ref[...], preferred_element_type=F32)
    h1_ref[...] = _layer_norm(alpha * h + mix, l1g_ref[...], l1b_ref[...])


def _merge(x, att, gla, gates, ln_g, ln_b, wa, wg, wo, l1g, l1b, *, alpha):
    n_tok = x.shape[0]
    tm = _row_tile(n_tok, TM_TOKEN)
    row = lambda n: pl.BlockSpec((tm, n), lambda i: (i, 0))
    consts = (ln_g, ln_b, wa, wg, wo, l1g, l1b)
    return pl.pallas_call(
        functools.partial(_merge_kernel, alpha=alpha),
        grid=(n_tok // tm,),
        in_specs=[row(D_MODEL), row(ATT_WIDTH), row(GLA_VW), row(N_BRANCHES * D_MODEL)]
                 + [_full_spec(a.shape) for a in consts],
        out_specs=row(D_MODEL),
        out_shape=jax.ShapeDtypeStruct((n_tok, D_MODEL), F32),
        compiler_params=_cparams(1),
        name="merge",
    )(x, att, gla, gates, *consts)


def _ffn_kernel(h_ref, wu_ref, wd_ref, g_ref, b_ref, y_ref, *, alpha, n_split):
    h = h_ref[...]
    hb = h.astype(BF16)
    w = D_FF // n_split
    ff = jnp.zeros(h.shape, F32)
    for s in range(n_split):
        u = jnp.dot(hb, wu_ref[:, s * w:(s + 1) * w], preferred_element_type=F32)
        u = jnp.square(jnp.maximum(u, 0.0)).astype(BF16)
        ff = ff + jnp.dot(u, wd_ref[s * w:(s + 1) * w, :], preferred_element_type=F32)
    y_ref[...] = _layer_norm(alpha * h + ff, g_ref[...], b_ref[...])


def _ffn(h1, wu, wd, g, b, *, alpha):
    n_tok = h1.shape[0]
    tm = _row_tile(n_tok, TM_TOKEN)
    row = pl.BlockSpec((tm, D_MODEL), lambda i: (i, 0))
    return pl.pallas_call(
        functools.partial(_ffn_kernel, alpha=alpha, n_split=4),
        grid=(n_tok // tm,),
        in_specs=[row] + [_full_spec(a.shape) for a in (wu, wd, g, b)],
        out_specs=row,
        out_shape=jax.ShapeDtypeStruct((n_tok, D_MODEL), F32),
        compiler_params=_cparams(1),
        name="ffn",
    )(h1, wu, wd, g, b)


def _ffn_window(h1, wu, wd, g, b, *, alpha, start, length):
    batch, rows, _ = h1.shape
    tm = _row_tile(length, TM_TOKEN)
    consts = (wu, wd, g, b)
    assert rows % SUBLANES == 0 and start % SUBLANES == 0
    first_row = lambda bi, j: pl.multiple_of(bi * rows + start + j * tm, SUBLANES)
    return pl.pallas_call(
        functools.partial(_ffn_kernel, alpha=alpha, n_split=4),
        grid=(batch, length // tm),
        in_specs=[pl.BlockSpec((pl.Element(tm), pl.Element(D_MODEL)), lambda bi, j: (first_row(bi, j), 0))]
                 + [_full_spec(a.shape) for a in consts],
        out_specs=pl.BlockSpec((None, tm, D_MODEL), lambda bi, j: (bi, j, 0)),
        out_shape=jax.ShapeDtypeStruct((batch, length, D_MODEL), F32),
        compiler_params=_cparams(2),
        name="ffn_window",
    )(h1.reshape(batch * rows, D_MODEL), *consts)


def _pack_weights(w_in, w_gla_a2, b_gla_a):
    points = []
    acc = 0
    for s in IN_SIZES[:-1]:
        acc += s
        points.append(acc)
    wq, wk, wv, wqi, wki, wwi, wgq, wgk, wgv, wog, wa1, wgt = jnp.split(w_in, points, axis=-1)
    wq = wq * (math.log2(math.e) * HEAD_DIM ** -0.5)
    wqi = wqi * IDX_DIM ** -0.5
    pad_cols = lambda a, n: jnp.pad(a, ((0, 0), (0, n - a.shape[1])))
    wv_heads = wv.T.reshape(N_KV_HEADS, HEAD_DIM, D_MODEL)
    wv_aug = jnp.pad(wv_heads, ((0, 0), (0, LANES - HEAD_DIM), (0, 0))).reshape(N_KV_HEADS * LANES, D_MODEL)
    w = {
        "qT": wq.T, "qiT": wqi.T, "vT": wv_aug,
        "wiT": jnp.pad(wwi.T, ((0, 2 * SUBLANES - IDX_HEADS), (0, 0))),
        "row_p": pad_cols(jnp.concatenate([wk, wv, wki], axis=1), 2 * KV_WIDTH + LANES),
        "row_s": pad_cols(jnp.concatenate([wq, wqi, wk, wv, wki, wwi], axis=1),
                          ATT_WIDTH + IDX_HEADS * IDX_DIM + 2 * KV_WIDTH + LANES),
        "gla": jnp.concatenate([wgq, wgk, wgv, wog], axis=1),
        "a1": pad_cols(wa1, LANES),
        "a2": jnp.pad(w_gla_a2, ((0, LANES - GLA_GATE_RANK), (0, 0))),
        "gt": wgt,
    }
    w = {name: a.astype(BF16) for name, a in w.items()}
    w["ba"] = b_gla_a.astype(F32)[None]
    return w


def _round_up(x, m):
    return -(-x // m) * m


def _row_tile(n, pref, unit=2 * SUBLANES):
    best = unit
    for t in range(unit, min(n, pref) + 1, unit):
        if n % t == 0:
            best = t
    assert n % best == 0
    return best


def kernel(x_prompt, x_sample, cache_k, cache_v, cache_kidx, state_gla, page_table, meta_tokens, ln_in_g, ln_in_b, w_in, w_gla_a2, b_gla_a, gla_norm_g, w_proj_attn, w_proj_gla, w_out, ln1_g, ln1_b, w_ff_up, w_ff_down, ln2_g, ln2_b):
    depth = w_in.shape[0]
    assert depth == 1, "single-layer step only"
    B, S_p, D = x_prompt.shape
    DB, S_s, _ = x_sample.shape
    n_pages = page_table.shape[1]
    past = n_pages * PAGE_SIZE
    assert D == D_MODEL and S_p % GLA_CHUNK == 0 and S_s <= SUBLANES and n_pages % PAGES_PER_STEP == 0
    topk_prompt = min(TOPK_MAX, S_p // 4)
    topk_sample = min(TOPK_MAX, (past + S_s) // 4)
    alpha = (2 * depth) ** 0.25

    row = lambda a: a.astype(F32).reshape(1, -1)
    ln_g, ln_b = row(ln_in_g), row(ln_in_b)
    w = _pack_weights(w_in[0], w_gla_a2[0], b_gla_a[0])
    wa, wg, wo = (a[0].astype(BF16) for a in (w_proj_attn, w_proj_gla, w_out))
    wu, wd = w_ff_up[0].astype(BF16), w_ff_down[0].astype(BF16)
    gnorm = row(gla_norm_g[0])
    l1g, l1b, l2g, l2b = row(ln1_g[0]), row(ln1_b[0]), row(ln2_g[0]), row(ln2_b[0])

    T = S_p + N_META
    off = (-N_META) % GLA_CHUNK
    t_pad = _round_up(off + T, math.lcm(KEY_CHUNK, GLA_CHUNK))
    gla_tb = _row_tile(t_pad, GLA_TOKENS_PER_STEP, GLA_CHUNK)
    n_tok = B * t_pad
    meta = jnp.broadcast_to(meta_tokens.astype(x_prompt.dtype)[None], (B, N_META, D))
    xp = jnp.concatenate([jnp.zeros((B, off, D), x_prompt.dtype), meta, x_prompt,
                          jnp.zeros((B, t_pad - off - T, D), x_prompt.dtype)], axis=1).reshape(n_tok, D)

    qT, qiT, vT3, wiT, k32, v32, ki32, kb, kib = _proj_attn_prompt(xp, ln_g, ln_b, w, n_tok)
    att_p = _dsa_prompt(qT, qiT, wiT,
                        kb.reshape(n_tok // KEY_CHUNK, KEY_CHUNK, KV_WIDTH),
                        kib.reshape(n_tok // KEY_CHUNK, KEY_CHUNK, IDX_DIM), vT3,
                        batch=B, t_pad=t_pad, topk=topk_prompt, off=off)
    gq, gk, gv, og, la, gates_p = _proj_gla(xp, ln_g, ln_b, w, n_tok)
    gla_p, state_p = _gla(gq, gk, gv, og, la, gnorm, None, batch=B, t_pad=t_pad, tb=gla_tb,
                          chunk=GLA_CHUNK, off=off, t_end=off + T)
    h1_p = _merge(xp, att_p, gla_p, gates_p, ln_g, ln_b, wa, wg, wo, l1g, l1b, alpha=alpha)
    seq = lambda a: a.reshape((B, t_pad) + a.shape[1:])
    y_prompt = _ffn_window(seq(h1_p), wu, wd, l2g, l2b, alpha=alpha, start=off + N_META, length=S_p)

    k_prompt = seq(k32)[:, off:off + T].reshape(1, B, T, N_KV_HEADS, HEAD_DIM)
    v_prompt = seq(v32)[:, off:off + T].reshape(1, B, T, N_KV_HEADS, HEAD_DIM)
    kidx_prompt = seq(ki32)[:, off:off + T][None]
    gla_state_prompt = state_p[None]

    R = SAMPLE_ROWS
    n_tok_s = DB * R
    xs = jnp.pad(x_sample, ((0, 0), (0, R - S_s), (0, 0))).reshape(n_tok_s, D)
    q_s, qi_s, k_s, v_s, kiwi_s = _proj_attn_sample(xs, ln_g, ln_b, w, n_tok_s)
    sseq = lambda a: a.reshape((DB, R) + a.shape[1:])[:, :S_s]
    k_new, v_new = sseq(k_s), sseq(v_s)
    ki_new = sseq(kiwi_s)[..., :IDX_DIM]
    wi_new = sseq(kiwi_s)[..., IDX_DIM:IDX_DIM + IDX_HEADS]
    qi_stack = sseq(qi_s).reshape(DB, S_s, IDX_HEADS, IDX_DIM).transpose(0, 2, 1, 3).reshape(DB, IDX_HEADS * S_s, IDX_DIM)
    wcol = jnp.broadcast_to(wi_new.transpose(0, 2, 1).reshape(DB, IDX_HEADS * S_s, 1), (DB, IDX_HEADS * S_s, LANES))
    q_heads = sseq(q_s).reshape(DB, S_s, N_HEADS, HEAD_DIM).transpose(0, 2, 1, 3)
    kv_of_head = (jnp.arange(N_HEADS) // GROUP)[:, None] == jnp.arange(N_KV_HEADS)[None, :]
    qbd = jnp.where(kv_of_head[None, :, None, :, None], q_heads[:, :, :, None, :], jnp.zeros((), BF16))
    qbd = qbd.reshape(DB, N_HEADS * S_s, KV_WIDTH)
    pad_page = lambda a: jnp.pad(a, ((0, 0), (0, PAGE_SIZE - S_s), (0, 0))).astype(BF16).transpose(0, 2, 1)
    kinew_pg, knew_pg, vnew_pg = pad_page(ki_new), pad_page(k_new), pad_page(v_new)
    ck = cache_k[0].transpose(0, 2, 3, 1).reshape(-1, KV_WIDTH, PAGE_SIZE)
    cv = cache_v[0].transpose(0, 2, 3, 1).reshape(-1, KV_WIDTH, PAGE_SIZE)
    cki = cache_kidx[0].transpose(0, 2, 1)
    sc_past, sc_new = _idx_sample(page_table, qi_stack, wcol, kinew_pg, cki, s_q=S_s)
    thr, jmax = _sel_sample(sc_past, sc_new, topk=topk_sample)
    o_s = _att_sample(page_table, qbd, sc_past, sc_new, thr, jmax, knew_pg, vnew_pg, ck, cv, s_q=S_s)
    o_s = o_s.reshape(DB, N_KV_HEADS, GROUP, S_s, N_KV_HEADS, HEAD_DIM)
    att_s = jnp.stack([o_s[:, n, :, :, n, :] for n in range(N_KV_HEADS)], axis=1)
    att_s = att_s.transpose(0, 3, 1, 2, 4).reshape(DB, S_s, ATT_WIDTH).astype(BF16)
    att_s = jnp.pad(att_s, ((0, 0), (0, R - S_s), (0, 0))).reshape(n_tok_s, ATT_WIDTH)

    gq, gk, gv, og, la, gates_s = _proj_gla(xs, ln_g, ln_b, w, n_tok_s)
    gla_s, state_s = _gla(gq, gk, gv, og, la, gnorm, state_gla[0], batch=DB, t_pad=R, tb=R,
                          chunk=R, off=0, t_end=S_s)
    h1_s = _merge(xs, att_s, gla_s, gates_s, ln_g, ln_b, wa, wg, wo, l1g, l1b, alpha=alpha)
    y_s = _ffn(h1_s, wu, wd, l2g, l2b, alpha=alpha)

    y_sample = y_s.reshape(DB, R, D)[:, :S_s]
    k_sample = k_new.reshape(1, DB, S_s, N_KV_HEADS, HEAD_DIM)
    v_sample = v_new.reshape(1, DB, S_s, N_KV_HEADS, HEAD_DIM)
    kidx_sample = ki_new[None]
    gla_state_sample = state_s[None]
    return (y_prompt, y_sample, k_prompt, v_prompt, kidx_prompt, gla_state_prompt,
            k_sample, v_sample, kidx_sample, gla_state_sample)
```
